```python
import math
import jax, jax.numpy as jnp
from jax import lax
import numpy as np

D_MODEL = 2048
BATCH = 4
SEQ = 2048
DEPTH = 1
DEC_BATCH = 128
DEC_SEQ = 4
PAST_LEN = 16384
PAGE_SIZE = 128

GDN_HEADS = 8
GDN_DK = 128
GDN_DV = 128
GDN_CONV = 4
MLSTM_HEADS = 8
MLSTM_DK = 128
MLSTM_DV = 128
D_FF = 5632
FFN_CONV = 3
CHUNK = 64
EPS = 1e-6

GDN_QK = GDN_HEADS * GDN_DK
GDN_V = GDN_HEADS * GDN_DV
GDN_QKV = 2 * GDN_QK + GDN_V
ML_QK = MLSTM_HEADS * MLSTM_DK
ML_V = MLSTM_HEADS * MLSTM_DV
ML_QKV = 2 * ML_QK + ML_V
IN_SIZES = (GDN_QKV, GDN_V, GDN_HEADS, GDN_HEADS, ML_QKV, MLSTM_HEADS, MLSTM_HEADS, ML_V, D_MODEL, D_MODEL)
IN_COLS = sum(IN_SIZES)

kernel_name = 'hybrid_gdn_mlstm_convffn_step'


def _rmsnorm(x, w):
    xf = x.astype(jnp.float32)
    y = xf * lax.rsqrt(jnp.mean(xf * xf, axis=-1, keepdims=True) + EPS)
    return y * w.astype(jnp.float32)


def _l2norm(x):
    return x * lax.rsqrt(jnp.sum(x * x, axis=-1, keepdims=True) + EPS)


def _causal_dwconv(x, buf, w):
    K = w.shape[0]
    T = x.shape[1]
    xp = jnp.concatenate([buf.astype(x.dtype), x], axis=1)
    y = sum(xp[:, j:j + T] * w[j] for j in range(K))
    return y, xp[:, T:]


def _gated_delta_chunked(q, k, v, g, beta, S0):
    B, H, T, DK = q.shape
    DV = v.shape[-1]
    L = math.gcd(T, CHUNK)
    N = T // L

    def blk(a):
        return a.reshape(B, H, N, L, *a.shape[3:])

    q, k, v, g, beta = blk(q), blk(k), blk(v), blk(g), blk(beta)
    G = jnp.cumsum(g, axis=-1)
    incl = jnp.tril(jnp.ones((L, L), dtype=bool))
    strict = jnp.tril(jnp.ones((L, L), dtype=bool), -1)
    decay = jnp.exp(jnp.where(incl, G[..., :, None] - G[..., None, :], -jnp.inf))
    a_strict = jnp.where(strict, beta[..., :, None] * jnp.einsum('bhnid,bhnjd->bhnij', k, k) * decay, 0.0)
    ut = jnp.eye(L, dtype=q.dtype) + a_strict
    u = lax.linalg.triangular_solve(ut, v * beta[..., None], left_side=True, lower=True, unit_diagonal=True)
    w = lax.linalg.triangular_solve(ut, k * (beta * jnp.exp(G))[..., None], left_side=True, lower=True, unit_diagonal=True)
    qk = jnp.einsum('bhnid,bhnjd->bhnij', q, k) * decay

    def step(S, xs):
        qc, kc, uc, wc, qkc, Gc = xs
        v_new = uc - jnp.einsum('bhld,bhdv->bhlv', wc, S)
        o = (jnp.einsum('bhld,bhdv->bhlv', qc * jnp.exp(Gc)[..., None], S)
             + jnp.einsum('bhij,bhjv->bhiv', qkc, v_new))
        g_last = Gc[..., -1]
        S = (S * jnp.exp(g_last)[..., None, None]
             + jnp.einsum('bhld,bhlv->bhdv', kc * jnp.exp(g_last[..., None] - Gc)[..., None], v_new))
        return S, o

    xs = tuple(jnp.moveaxis(a, 2, 0) for a in (q, k, u, w, qk, G))
    S, o = lax.scan(step, S0, xs)
    return jnp.moveaxis(o, 0, 2).reshape(B, H, T, DV), S


def _mlstm_chunked(q, k, v, ig, lf, C0, n0, m0):
    B, H, T, DK = q.shape
    DV = v.shape[-1]
    L = math.gcd(T, CHUNK)
    N = T // L

    def blk(a):
        return a.reshape(B, H, N, L, *a.shape[3:])

    q, k, v, ig, lf = blk(q), blk(k), blk(v), blk(ig), blk(lf)
    b = jnp.cumsum(lf, axis=-1)
    incl = jnp.tril(jnp.ones((L, L), dtype=bool))
    d_log = jnp.where(incl, b[..., :, None] - b[..., None, :] + ig[..., None, :], -jnp.inf)
    d_max = jnp.max(d_log, axis=-1)
    e_log = b[..., -1:] - b + ig
    e_max = jnp.max(e_log, axis=-1)
    qk = jnp.einsum('bhnid,bhnjd->bhnij', q, k)

    def step(carry, xs):
        C, n, m = carry
        qc, kc, vc, bc, dc, dmc, ec, emc, qkc = xs
        inter = bc + m[..., None]
        mt = jnp.maximum(inter, dmc)
        wi = jnp.exp(inter - mt)
        p = jnp.exp(dc - mt[..., None]) * qkc
        num = (wi[..., None] * jnp.einsum('bhvd,bhld->bhlv', C, qc)
               + jnp.einsum('bhls,bhsv->bhlv', p, vc))
        den = wi * jnp.einsum('bhd,bhld->bhl', n, qc) + jnp.sum(p, axis=-1)
        h = num / jnp.maximum(jnp.abs(den), jnp.exp(-mt))[..., None]
        b_last = bc[..., -1]
        m_new = jnp.maximum(b_last + m, emc)
        fw = jnp.exp(b_last + m - m_new)
        sw = jnp.exp(ec - m_new[..., None])
        C = fw[..., None, None] * C + jnp.einsum('bhl,bhlv,bhld->bhvd', sw, vc, kc)
        n = fw[..., None] * n + jnp.einsum('bhl,bhld->bhd', sw, kc)
        return (C, n, m_new), h

    xs = tuple(jnp.moveaxis(a, 2, 0) for a in (q, k, v, b, d_log, d_max, e_log, e_max, qk))
    (C, n, m), h = lax.scan(step, (C0, n0, m0), xs)
    return jnp.moveaxis(h, 0, 2).reshape(B, H, T, DV), C, n, m


def _hybrid_layer(x, gdn_conv_buf, gdn_S, ml_C, ml_n, ml_m, ffn_conv_buf,
                  norm_mix_w, w_in, gdn_conv_w, gdn_A_log, gdn_dt_bias, gdn_norm_w, w_branch_a,
                  ml_b_i, ml_b_f, w_branch_b, w_out, norm_ffn_w, w_up, ffn_conv_w, w_down):
    B, T, _ = x.shape
    f32 = jnp.float32
    h = _rmsnorm(x, norm_mix_w).astype(x.dtype)
    proj = h @ w_in
    idx = [int(i) for i in np.cumsum(IN_SIZES)[:-1]]
    g_qkv, g_z, g_a, g_b, m_qkv, m_i, m_f, m_o, gate_a, gate_b = jnp.split(proj, idx, axis=-1)

    def heads(a, n_heads, d):
        return a.reshape(B, T, n_heads, d).transpose(0, 2, 1, 3).astype(f32)

    c, new_gdn_conv = _causal_dwconv(g_qkv, gdn_conv_buf, gdn_conv_w)
    c = jax.nn.silu(c)
    gq, gk, gv = jnp.split(c, [GDN_QK, 2 * GDN_QK], axis=-1)
    gq = _l2norm(heads(gq, GDN_HEADS, GDN_DK)) * (GDN_DK ** -0.5)
    gk = _l2norm(heads(gk, GDN_HEADS, GDN_DK))
    gv = heads(gv, GDN_HEADS, GDN_DV)
    g = (-jnp.exp(gdn_A_log.astype(f32)) * jax.nn.softplus(g_a.astype(f32) + gdn_dt_bias.astype(f32))).transpose(0, 2, 1)
    beta = jax.nn.sigmoid(g_b.astype(f32)).transpose(0, 2, 1)
    go, new_gdn_S = _gated_delta_chunked(gq, gk, gv, g, beta, gdn_S.astype(f32))
    go = _rmsnorm(go.transpose(0, 2, 1, 3), gdn_norm_w) * jax.nn.silu(g_z.astype(f32).reshape(B, T, GDN_HEADS, GDN_DV))
    y_a = go.reshape(B, T, GDN_V).astype(x.dtype) @ w_branch_a

    mq, mk, mv = jnp.split(m_qkv, [ML_QK, 2 * ML_QK], axis=-1)
    mq = heads(mq, MLSTM_HEADS, MLSTM_DK)
    mk = heads(mk, MLSTM_HEADS, MLSTM_DK) * (MLSTM_DK ** -0.5)
    mv = heads(mv, MLSTM_HEADS, MLSTM_DV)
    ig = (m_i.astype(f32) + ml_b_i.astype(f32)).transpose(0, 2, 1)
    lf = jax.nn.log_sigmoid(m_f.astype(f32) + ml_b_f.astype(f32)).transpose(0, 2, 1)
    mh, new_C, new_n, new_m = _mlstm_chunked(mq, mk, mv, ig, lf, ml_C.astype(f32), ml_n.astype(f32), ml_m.astype(f32))
    mh = mh.transpose(0, 2, 1, 3).reshape(B, T, ML_V) * jax.nn.sigmoid(m_o.astype(f32))
    y_b = mh.astype(x.dtype) @ w_branch_b

    mixed = jax.nn.sigmoid(gate_a) * y_a + jax.nn.sigmoid(gate_b) * y_b
    x = x + mixed @ w_out

    h = _rmsnorm(x, norm_ffn_w).astype(x.dtype)
    u, new_ffn_conv = _causal_dwconv(h @ w_up, ffn_conv_buf, ffn_conv_w)
    u_gate, u_val = jnp.split(u, 2, axis=-1)
    x = x + (jax.nn.silu(u_gate) * u_val) @ w_down
    return x, (new_gdn_conv, new_gdn_S, new_C, new_n, new_m, new_ffn_conv)


def setup_inputs(seed: int = 0) -> dict:
    key = jax.random.key(seed)
    ks = jax.random.split(key, 26)
    f32 = jnp.float32

    def nrm(k, shape, scale):
        return jax.random.normal(k, shape, f32) * scale

    x_prompt = nrm(ks[0], (BATCH, SEQ, D_MODEL), 1.0)
    x_sample = nrm(ks[1], (DEC_BATCH, DEC_SEQ, D_MODEL), 1.0)
    state_gdn_conv = nrm(ks[2], (DEPTH, DEC_BATCH, GDN_CONV - 1, GDN_QKV), 1.0)
    state_gdn_S = nrm(ks[3], (DEPTH, DEC_BATCH, GDN_HEADS, GDN_DK, GDN_DV), 0.3)
    state_ml_C = nrm(ks[4], (DEPTH, DEC_BATCH, MLSTM_HEADS, MLSTM_DV, MLSTM_DK), 0.3)
    state_ml_n = nrm(ks[5], (DEPTH, DEC_BATCH, MLSTM_HEADS, MLSTM_DK), 0.3)
    state_ml_m = jax.random.uniform(ks[6], (DEPTH, DEC_BATCH, MLSTM_HEADS), f32, 0.0, 4.0)
    state_ffn_conv = nrm(ks[7], (DEPTH, DEC_BATCH, FFN_CONV - 1, 2 * D_FF), 1.0)

    norm_mix_w = 1.0 + nrm(ks[8], (DEPTH, D_MODEL), 0.02)
    w_in = nrm(ks[9], (DEPTH, D_MODEL, IN_COLS), D_MODEL ** -0.5)
    gdn_conv_w = nrm(ks[10], (DEPTH, GDN_CONV, GDN_QKV), GDN_CONV ** -0.5)
    gdn_A_log = jnp.log(jax.random.uniform(ks[11], (DEPTH, GDN_HEADS), f32, 1.0, 16.0))
    dt = jnp.exp(jax.random.uniform(ks[12], (DEPTH, GDN_HEADS), f32, math.log(1e-3), math.log(1e-1)))
    gdn_dt_bias = dt + jnp.log(-jnp.expm1(-dt))
    gdn_norm_w = 1.0 + nrm(ks[13], (DEPTH, GDN_DV), 0.02)
    w_branch_a = nrm(ks[14], (DEPTH, GDN_V, D_MODEL), GDN_V ** -0.5)
    ml_b_i = nrm(ks[15], (DEPTH, MLSTM_HEADS), 0.1)
    ml_b_f = jnp.linspace(3.0, 6.0, MLSTM_HEADS, dtype=f32)[None, :] + nrm(ks[16], (DEPTH, MLSTM_HEADS), 0.1)
    w_branch_b = nrm(ks[17], (DEPTH, ML_V, D_MODEL), ML_V ** -0.5)
    w_out = nrm(ks[18], (DEPTH, D_MODEL, D_MODEL), D_MODEL ** -0.5)
    norm_ffn_w = 1.0 + nrm(ks[19], (DEPTH, D_MODEL), 0.02)
    w_up = nrm(ks[20], (DEPTH, D_MODEL, 2 * D_FF), D_MODEL ** -0.5)
    ffn_conv_w = nrm(ks[21], (DEPTH, FFN_CONV, 2 * D_FF), FFN_CONV ** -0.5)
    w_down = nrm(ks[22], (DEPTH, D_FF, D_MODEL), D_FF ** -0.5)
    norm_final_w = 1.0 + nrm(ks[23], (D_MODEL,), 0.02)
    return {
        'x_prompt': x_prompt, 'x_sample': x_sample,
        'state_gdn_conv': state_gdn_conv, 'state_gdn_S': state_gdn_S,
        'state_ml_C': state_ml_C, 'state_ml_n': state_ml_n, 'state_ml_m': state_ml_m,
        'state_ffn_conv': state_ffn_conv,
        'norm_mix_w': norm_mix_w, 'w_in': w_in, 'gdn_conv_w': gdn_conv_w, 'gdn_A_log': gdn_A_log,
        'gdn_dt_bias': gdn_dt_bias, 'gdn_norm_w': gdn_norm_w, 'w_branch_a': w_branch_a,
        'ml_b_i': ml_b_i, 'ml_b_f': ml_b_f, 'w_branch_b': w_branch_b, 'w_out': w_out,
        'norm_ffn_w': norm_ffn_w, 'w_up': w_up, 'ffn_conv_w': ffn_conv_w, 'w_down': w_down,
        'norm_final_w': norm_final_w,
    }


def reference(x_prompt, x_sample, state_gdn_conv, state_gdn_S, state_ml_C, state_ml_n, state_ml_m, state_ffn_conv,
              norm_mix_w, w_in, gdn_conv_w, gdn_A_log, gdn_dt_bias, gdn_norm_w, w_branch_a,
              ml_b_i, ml_b_f, w_branch_b, w_out, norm_ffn_w, w_up, ffn_conv_w, w_down, norm_final_w):
    f32 = jnp.float32
    bp = x_prompt.shape[0]
    yp, ys = x_prompt, x_sample
    new_p, new_s = [], []
    for l in range(DEPTH):
        w = (norm_mix_w[l], w_in[l], gdn_conv_w[l], gdn_A_log[l], gdn_dt_bias[l], gdn_norm_w[l], w_branch_a[l],
             ml_b_i[l], ml_b_f[l], w_branch_b[l], w_out[l], norm_ffn_w[l], w_up[l], ffn_conv_w[l], w_down[l])
        p_init = (jnp.zeros((bp, GDN_CONV - 1, GDN_QKV), x_prompt.dtype),
                  jnp.zeros((bp, GDN_HEADS, GDN_DK, GDN_DV), f32),
                  jnp.zeros((bp, MLSTM_HEADS, MLSTM_DV, MLSTM_DK), f32),
                  jnp.zeros((bp, MLSTM_HEADS, MLSTM_DK), f32),
                  jnp.zeros((bp, MLSTM_HEADS), f32),
                  jnp.zeros((bp, FFN_CONV - 1, 2 * D_FF), x_prompt.dtype))
        yp, sp = _hybrid_layer(yp, *p_init, *w)
        ys, ss = _hybrid_layer(ys, state_gdn_conv[l], state_gdn_S[l], state_ml_C[l], state_ml_n[l], state_ml_m[l],
                               state_ffn_conv[l], *w)
        new_p.append(sp)
        new_s.append(ss)
    y_prompt = _rmsnorm(yp, norm_final_w).astype(x_prompt.dtype)
    y_sample = _rmsnorm(ys, norm_final_w).astype(x_sample.dtype)

    def stack(states, i):
        return jnp.stack([s[i] for s in states])

    return (y_prompt, y_sample,
            stack(new_p, 0), stack(new_p, 1), stack(new_p, 2), stack(new_p, 3), stack(new_p, 4), stack(new_p, 5),
            stack(new_s, 0), stack(new_s, 1), stack(new_s, 2), stack(new_s, 3), stack(new_s, 4), stack(new_s, 5))
```

```python
import functools
import math

import jax
import jax.numpy as jnp
from jax import lax
from jax.experimental import pallas as pl
from jax.experimental.pallas import tpu as pltpu

F32 = jnp.float32
BF16 = jnp.bfloat16

D_MODEL = 2048
HEADS = 8
DH = 128
QK = HEADS * DH
QKV = 3 * QK
D_FF = 5632
GDN_CONV = 4
FFN_CONV = 3
EPS = 1e-6
CHUNK = 64
N_GATES = 4 * HEADS
SUBLANES = 8
NEG = -1e30

BIG_COLS = 2 * QKV + 2 * QK + 2 * D_MODEL

VMEM_LIMIT = 56 * 1024 * 1024


def _params(*sem):
    return pltpu.CompilerParams(dimension_semantics=sem, vmem_limit_bytes=VMEM_LIMIT)


def _dot(a, b):
    return jnp.dot(a.astype(BF16), b.astype(BF16), preferred_element_type=F32)


def _dot_nt(a, b):
    return lax.dot_general(a.astype(BF16), b.astype(BF16), (((1,), (1,)), ((), ())),
                           preferred_element_type=F32)


def _dot_tn(a, b):
    return lax.dot_general(a.astype(BF16), b.astype(BF16), (((0,), (0,)), ((), ())),
                           preferred_element_type=F32)


def _split(a):
    hi = a.astype(BF16)
    lo = (a - hi.astype(F32)).astype(BF16)
    return hi, lo


def _dot3(a, b):
    ah, al = _split(a)
    bh, bl = _split(b)
    d = functools.partial(jnp.dot, preferred_element_type=F32)
    return d(ah, bh) + (d(ah, bl) + d(al, bh))


def _softplus(x):
    return jnp.maximum(x, 0.0) + jnp.log1p(jnp.exp(-jnp.abs(x)))


def _rms(x, w):
    return x * lax.rsqrt(jnp.mean(x * x, axis=-1, keepdims=True) + EPS) * w


def _inproj_kernel(x_ref, nw_ref, wbig_ref, wsm_ref, gbias_ref, alog_ref, big_ref, gates_ref, h_scr):
    @pl.when(pl.program_id(1) == 0)
    def _():
        hb = _rms(x_ref[...], nw_ref[...]).astype(BF16)
        h_scr[...] = hb
        raw = jnp.dot(hb, wsm_ref[...], preferred_element_type=F32)
        lane = lax.broadcasted_iota(jnp.int32, raw.shape, 1)
        z = raw + gbias_ref[...]
        g = -jnp.exp(alog_ref[...]) * _softplus(z)
        beta = jax.nn.sigmoid(raw)
        lf = -_softplus(-z)
        gates_ref[...] = jnp.where(lane < HEADS, g,
                                   jnp.where(lane < 2 * HEADS, beta,
                                             jnp.where(lane < 3 * HEADS, z, lf)))

    big_ref[...] = jnp.dot(h_scr[...], wbig_ref[...], preferred_element_type=F32)


def _inproj(x, nw, wbig, wsm, gbias, alog, *, tm, tn):
    m = x.shape[0]
    return pl.pallas_call(
        _inproj_kernel,
        grid=(m // tm, BIG_COLS // tn),
        in_specs=[
            pl.BlockSpec((tm, D_MODEL), lambda i, j: (i, 0)),
            pl.BlockSpec((1, D_MODEL), lambda i, j: (0, 0)),
            pl.BlockSpec((D_MODEL, tn), lambda i, j: (0, j)),
            pl.BlockSpec((D_MODEL, N_GATES), lambda i, j: (0, 0)),
            pl.BlockSpec((1, N_GATES), lambda i, j: (0, 0)),
            pl.BlockSpec((1, N_GATES), lambda i, j: (0, 0)),
        ],
        out_specs=[
            pl.BlockSpec((tm, tn), lambda i, j: (i, j)),
            pl.BlockSpec((tm, N_GATES), lambda i, j: (i, 0)),
        ],
        out_shape=[jax.ShapeDtypeStruct((m, BIG_COLS), F32),
                   jax.ShapeDtypeStruct((m, N_GATES), F32)],
        scratch_shapes=[pltpu.VMEM((tm, D_MODEL), BF16)],
        compiler_params=_params("parallel", "arbitrary"),
        name="inproj",
    )(x, nw, wbig, wsm, gbias, alog)


def _chunk_masks(rows):
    ri = lax.broadcasted_iota(jnp.int32, (rows, rows), 0)
    ci = lax.broadcasted_iota(jnp.int32, (rows, rows), 1)
    return ri == ci, ci <= ri, ci < ri


def _row_of(col, eye):
    return jnp.sum(jnp.where(eye, col, 0.0), axis=0, keepdims=True)


def _cumsum_col(col, eye, incl):
    return jnp.sum(jnp.where(incl, _row_of(col, eye), 0.0), axis=1, keepdims=True)


def _conv_silu(hist_ref, cw_ref, col0, rows):
    cols = slice(col0, col0 + DH)
    acc = hist_ref[pl.ds(SUBLANES - GDN_CONV + 1, rows), cols] * cw_ref[0:1, cols]
    for j in range(1, GDN_CONV):
        acc = acc + hist_ref[pl.ds(SUBLANES - GDN_CONV + 1 + j, rows), cols] * cw_ref[j:j + 1, cols]
    return acc * jax.nn.sigmoid(acc)


def _gdn_kernel(qkv_ref, z_ref, gates_ref, cst_ref, s0_ref, cw_ref, nw_ref, o_ref, s_ref, hist,
                *, rows, valid, levels):
    c = pl.program_id(1)
    keep = GDN_CONV - 1

    @pl.when(c == 0)
    def _():
        hist[pl.ds(SUBLANES - keep, keep), :] = cst_ref[0]
        s_ref[...] = s0_ref[...]

    hist[pl.ds(SUBLANES, rows), :] = qkv_ref[...]

    eye, incl, strict = _chunk_masks(rows)
    rvalid = lax.broadcasted_iota(jnp.int32, (rows, 1), 0) < valid
    gates = gates_ref[...]
    eye_f = eye.astype(F32)

    for h in range(HEADS):
        q = _conv_silu(hist, cw_ref, h * DH, rows)
        k = _conv_silu(hist, cw_ref, QK + h * DH, rows)
        v = _conv_silu(hist, cw_ref, 2 * QK + h * DH, rows)
        q = q * lax.rsqrt(jnp.sum(q * q, axis=-1, keepdims=True) + EPS) * (DH ** -0.5)
        k = k * lax.rsqrt(jnp.sum(k * k, axis=-1, keepdims=True) + EPS)
        g = jnp.where(rvalid, gates[:, h:h + 1], 0.0)
        beta = jnp.where(rvalid, gates[:, HEADS + h:HEADS + h + 1], 0.0)

        gc = _cumsum_col(g, eye, incl)
        gr = _row_of(gc, eye)
        decay = jnp.where(incl, jnp.exp(jnp.where(incl, gc - gr, 0.0)), 0.0)
        kk = _dot_nt(k, k)
        a = jnp.where(strict, beta * kk * decay, 0.0)
        bk = -a
        t = eye_f + bk
        for _ in range(1, levels):
            bk = _dot3(bk, bk)
            t = t + _dot3(t, bk)
        eg = jnp.exp(gc)
        uw = _dot3(t, jnp.concatenate([v * beta, k * (beta * eg)], axis=1))
        u, w = uw[:, :DH], uw[:, DH:]
        qk = _dot_nt(q, k) * decay

        s = s_ref[0, h]
        ws = _dot(jnp.concatenate([w, q * eg], axis=0), s)
        v_new = u - ws[:rows]
        o = ws[rows:] + _dot(qk, v_new)
        g_last = gc[rows - 1:rows, :]
        s_ref[0, h] = s * jnp.exp(g_last) + _dot_tn(k * jnp.exp(g_last - gc), v_new)

        zh = z_ref[:, h * DH:(h + 1) * DH]
        o_ref[:, h * DH:(h + 1) * DH] = _rms(o, nw_ref[...]) * (zh * jax.nn.sigmoid(zh))

    hist[pl.ds(SUBLANES - keep, keep), :] = hist[pl.ds(SUBLANES + rows - keep, keep), :]


def _gdn(big, gates, conv_state, s0, conv_w, norm_w, *, nb, nc, rows, valid):
    levels = max(1, math.ceil(math.log2(valid)))
    kern = functools.partial(_gdn_kernel, rows=rows, valid=valid, levels=levels)
    return pl.pallas_call(
        kern,
        grid=(nb, nc),
        in_specs=[
            pl.BlockSpec((rows, QKV), lambda b, c: (b * nc + c, 0)),
            pl.BlockSpec((rows, QK), lambda b, c: (b * nc + c, 2 * QKV // QK)),
            pl.BlockSpec((rows, N_GATES), lambda b, c: (b * nc + c, 0)),
            pl.BlockSpec((1, GDN_CONV - 1, QKV), lambda b, c: (b, 0, 0)),
            pl.BlockSpec((1, HEADS, DH, DH), lambda b, c: (b, 0, 0, 0)),
            pl.BlockSpec((GDN_CONV, QKV), lambda b, c: (0, 0)),
            pl.BlockSpec((1, DH), lambda b, c: (0, 0)),
        ],
        out_specs=[
            pl.BlockSpec((rows, QK), lambda b, c: (b * nc + c, 0)),
            pl.BlockSpec((1, HEADS, DH, DH), lambda b, c: (b, 0, 0, 0)),
        ],
        out_shape=[jax.ShapeDtypeStruct((nb * nc * rows, QK), F32),
                   jax.ShapeDtypeStruct((nb, HEADS, DH, DH), F32)],
        scratch_shapes=[pltpu.VMEM((SUBLANES + rows, QKV), F32)],
        compiler_params=_params("parallel", "arbitrary"),
        name="gdn",
    )(big, big, gates, conv_state, s0, conv_w, norm_w)


def _mlstm_kernel(qkv_ref, og_ref, gates_ref, c0_ref, n0_ref, m0_ref, h_ref, c_ref, n_ref, m_ref,
                  *, rows, valid):
    @pl.when(pl.program_id(1) == 0)
    def _():
        c_ref[...] = c0_ref[...]
        n_ref[...] = n0_ref[...]
        m_ref[...] = m0_ref[...]

    eye, incl, _ = _chunk_masks(rows)
    rvalid = lax.broadcasted_iota(jnp.int32, (rows, 1), 0) < valid
    gates = gates_ref[...]
    m_all = m_ref[0]

    for h in range(HEADS):
        q = qkv_ref[:, h * DH:(h + 1) * DH]
        k = qkv_ref[:, QK + h * DH:QK + (h + 1) * DH] * (DH ** -0.5)
        v = qkv_ref[:, 2 * QK + h * DH:2 * QK + (h + 1) * DH]
        ig = jnp.where(rvalid, gates[:, 2 * HEADS + h:2 * HEADS + h + 1], NEG)
        lf = jnp.where(rvalid, gates[:, 3 * HEADS + h:3 * HEADS + h + 1], 0.0)

        bc = _cumsum_col(lf, eye, incl)
        br = _row_of(bc, eye)
        igr = _row_of(ig, eye)
        d_log = jnp.where(incl, bc - br + igr, NEG)
        d_max = jnp.max(d_log, axis=1, keepdims=True)
        b_last = bc[rows - 1:rows, :]
        e_log = b_last - bc + ig
        e_max = jnp.max(e_log, axis=0, keepdims=True)
        qk = _dot_nt(q, k)

        cm = c_ref[0, h]
        nv = n_ref[0, h:h + 1, :]
        m = m_all[:, h:h + 1]
        inter = bc + m
        mt = jnp.maximum(inter, d_max)
        wi = jnp.exp(inter - mt)
        p = jnp.where(incl, jnp.exp(d_log - mt), 0.0) * qk
        num = wi * _dot_nt(q, cm) + _dot(p, v)
        den = wi * jnp.sum(q * nv, axis=-1, keepdims=True) + jnp.sum(p, axis=-1, keepdims=True)
        hv = num / jnp.maximum(jnp.abs(den), jnp.exp(-mt))

        m_new = jnp.maximum(b_last + m, e_max)
        fw = jnp.exp(b_last + m - m_new)
        sw = jnp.exp(e_log - m_new)
        c_ref[0, h] = fw * cm + _dot_tn(sw * v, k)
        n_ref[0, h:h + 1, :] = fw * nv + jnp.sum(sw * k, axis=0, keepdims=True)
        m_ref[0, :, h:h + 1] = m_new

        og = og_ref[:, h * DH:(h + 1) * DH]
        h_ref[:, h * DH:(h + 1) * DH] = hv * jax.nn.sigmoid(og)


def _mlstm(big, gates, c0, n0, m0, *, nb, nc, rows, valid):
    kern = functools.partial(_mlstm_kernel, rows=rows, valid=valid)
    return pl.pallas_call(
        kern,
        grid=(nb, nc),
        in_specs=[
            pl.BlockSpec((rows, QKV), lambda b, c: (b * nc + c, 1)),
            pl.BlockSpec((rows, QK), lambda b, c: (b * nc + c, 2 * QKV // QK + 1)),
            pl.BlockSpec((rows, N_GATES), lambda b, c: (b * nc + c, 0)),
            pl.BlockSpec((1, HEADS, DH, DH), lambda b, c: (b, 0, 0, 0)),
            pl.BlockSpec((1, HEADS, DH), lambda b, c: (b, 0, 0)),
            pl.BlockSpec((1, 1, HEADS), lambda b, c: (b, 0, 0)),
        ],
        out_specs=[
            pl.BlockSpec((rows, QK), lambda b, c: (b * nc + c, 0)),
            pl.BlockSpec((1, HEADS, DH, DH), lambda b, c: (b, 0, 0, 0)),
            pl.BlockSpec((1, HEADS, DH), lambda b, c: (b, 0, 0)),
            pl.BlockSpec((1, 1, HEADS), lambda b, c: (b, 0, 0)),
        ],
        out_shape=[jax.ShapeDtypeStruct((nb * nc * rows, QK), F32),
                   jax.ShapeDtypeStruct((nb, HEADS, DH, DH), F32),
                   jax.ShapeDtypeStruct((nb, HEADS, DH), F32),
                   jax.ShapeDtypeStruct((nb, 1, HEADS), F32)],
        compiler_params=_params("parallel", "arbitrary"),
        name="mlstm",
    )(big, big, gates, c0, n0, m0)


def _merge_kernel(go_ref, mh_ref, ga_ref, gb_ref, x_ref, wa_ref, wb_ref, wo_ref, x1_ref, mix_scr):
    @pl.when(pl.program_id(1) == 0)
    def _():
        ya = jnp.dot(go_ref[...].astype(BF16), wa_ref[...], preferred_element_type=F32)
        yb = jnp.dot(mh_ref[...].astype(BF16), wb_ref[...], preferred_element_type=F32)
        mixed = jax.nn.sigmoid(ga_ref[...]) * ya + jax.nn.sigmoid(gb_ref[...]) * yb
        mix_scr[...] = mixed.astype(BF16)

    x1_ref[...] = x_ref[...] + jnp.dot(mix_scr[...], wo_ref[...], preferred_element_type=F32)


def _merge(go, mh, big, x, wa, wb, wo, *, tm, tn):
    m = x.shape[0]
    ga_blk = (2 * QKV + 2 * QK) // D_MODEL
    return pl.pallas_call(
        _merge_kernel,
        grid=(m // tm, D_MODEL // tn),
        in_specs=[
            pl.BlockSpec((tm, QK), lambda i, j: (i, 0)),
            pl.BlockSpec((tm, QK), lambda i, j: (i, 0)),
            pl.BlockSpec((tm, D_MODEL), lambda i, j: (i, ga_blk)),
            pl.BlockSpec((tm, D_MODEL), lambda i, j: (i, ga_blk + 1)),
            pl.BlockSpec((tm, tn), lambda i, j: (i, j)),
            pl.BlockSpec((QK, D_MODEL), lambda i, j: (0, 0)),
            pl.BlockSpec((QK, D_MODEL), lambda i, j: (0, 0)),
            pl.BlockSpec((D_MODEL, tn), lambda i, j: (0, j)),
        ],
        out_specs=pl.BlockSpec((tm, tn), lambda i, j: (i, j)),
        out_shape=jax.ShapeDtypeStruct((m, D_MODEL), F32),
        scratch_shapes=[pltpu.VMEM((tm, D_MODEL), BF16)],
        compiler_params=_params("parallel", "arbitrary"),
        name="merge",
    )(go, mh, big, big, x, wa, wb, wo)


def _ffn_kernel(*refs, tm, tf, seq_tiles, with_state):
    if with_state:
        (x1_ref, nw_ref, wg_ref, wv_ref, cwg_ref, cwv_ref, wd_ref, fnw_ref, stg_ref, stv_ref,
         y_ref, upg_ref, upv_ref, h2_scr, hist_g, hist_v, acc_scr, st_g, st_v) = refs
    else:
        (x1_ref, halo_ref, nw_ref, wg_ref, wv_ref, cwg_ref, cwv_ref, wd_ref, fnw_ref,
         y_ref, upg_ref, upv_ref, h2_scr, hist_g, hist_v, acc_scr) = refs
    i = pl.program_id(0)
    f = pl.program_id(1)
    pad = SUBLANES

    @pl.when(f == 0)
    def _():
        if with_state:
            h2_scr[pl.ds(0, pad), :] = jnp.zeros((pad, D_MODEL), BF16)
        else:
            live = (i % seq_tiles != 0).astype(F32)
            h2_scr[pl.ds(0, pad), :] = (_rms(halo_ref[...], nw_ref[...]) * live).astype(BF16)
        h2_scr[pl.ds(pad, tm), :] = _rms(x1_ref[...], nw_ref[...]).astype(BF16)
        acc_scr[...] = jnp.zeros_like(acc_scr)

    h2 = h2_scr[...]
    hist_g[...] = jnp.dot(h2, wg_ref[...], preferred_element_type=F32)
    hist_v[...] = jnp.dot(h2, wv_ref[...], preferred_element_type=F32)

    if with_state:
        upg_ref[...] = hist_g[pl.ds(pad, tm), :]
        upv_ref[...] = hist_v[pl.ds(pad, tm), :]
        st_g[pl.ds(0, tm), :] = stg_ref[...]
        st_v[pl.ds(0, tm), :] = stv_ref[...]
        st_g[pl.ds(tm, pad), :] = jnp.zeros((pad, tf), F32)
        st_v[pl.ds(tm, pad), :] = jnp.zeros((pad, tf), F32)
        rmod = lax.broadcasted_iota(jnp.int32, (tm, 1), 0) % SUBLANES
    else:
        upg_ref[...] = hist_g[pl.ds(tm, pad), :]
        upv_ref[...] = hist_v[pl.ds(tm, pad), :]

    def conv(hist, cw_ref, st):
        prev2 = hist[pl.ds(pad - 2, tm), :]
        prev1 = hist[pl.ds(pad - 1, tm), :]
        if with_state:
            prev2 = jnp.where(rmod < 2, st[pl.ds(0, tm), :], prev2)
            prev1 = jnp.where(rmod < 1, st[pl.ds(1, tm), :], prev1)
        return (prev2 * cw_ref[0:1, :] + prev1 * cw_ref[1:2, :]) + hist[pl.ds(pad, tm), :] * cw_ref[2:3, :]

    ug = conv(hist_g, cwg_ref, st_g if with_state else None)
    uv = conv(hist_v, cwv_ref, st_v if with_state else None)
    act = (ug * jax.nn.sigmoid(ug) * uv).astype(BF16)
    acc_scr[...] += jnp.dot(act, wd_ref[...], preferred_element_type=F32)

    @pl.when(f == pl.num_programs(1) - 1)
    def _():
        y_ref[...] = _rms(x1_ref[...] + acc_scr[...], fnw_ref[...])


def _ffn(x1, nw, wup, cw, wd, fnw, state=None, *, tm, tf, seq_tiles):
    m = x1.shape[0]
    nf = D_FF // tf
    with_state = state is not None
    tr = tm if with_state else SUBLANES
    kern = functools.partial(_ffn_kernel, tm=tm, tf=tf, seq_tiles=seq_tiles, with_state=with_state)
    in_specs = [pl.BlockSpec((tm, D_MODEL), lambda i, f: (i, 0))]
    args = [x1]
    if not with_state:
        in_specs.append(pl.BlockSpec((SUBLANES, D_MODEL),
                                     lambda i, f: (jnp.maximum(i * (tm // SUBLANES) - 1, 0), 0)))
        args.append(x1)
    in_specs += [
        pl.BlockSpec((1, D_MODEL), lambda i, f: (0, 0)),
        pl.BlockSpec((D_MODEL, tf), lambda i, f: (0, f)),
        pl.BlockSpec((D_MODEL, tf), lambda i, f: (0, f + nf)),
        pl.BlockSpec((FFN_CONV, tf), lambda i, f: (0, f)),
        pl.BlockSpec((FFN_CONV, tf), lambda i, f: (0, f + nf)),
        pl.BlockSpec((tf, D_MODEL), lambda i, f: (f, 0)),
        pl.BlockSpec((1, D_MODEL), lambda i, f: (0, 0)),
    ]
    args += [nw, wup, wup, cw, cw, wd, fnw]
    scratch = [pltpu.VMEM((SUBLANES + tm, D_MODEL), BF16),
               pltpu.VMEM((SUBLANES + tm, tf), F32),
               pltpu.VMEM((SUBLANES + tm, tf), F32),
               pltpu.VMEM((tm, D_MODEL), F32)]
    if with_state:
        in_specs += [pl.BlockSpec((tm, tf), lambda i, f: (i, f)),
                     pl.BlockSpec((tm, tf), lambda i, f: (i, f + nf))]
        args += [state, state]
        scratch += [pltpu.VMEM((tm + SUBLANES, tf), F32), pltpu.VMEM((tm + SUBLANES, tf), F32)]
    n_tiles = m // tm
    return pl.pallas_call(
        kern,
        grid=(n_tiles, nf),
        in_specs=in_specs,
        out_specs=[
            pl.BlockSpec((tm, D_MODEL), lambda i, f: (i, 0)),
            pl.BlockSpec((tr, tf), lambda i, f: (i, f)),
            pl.BlockSpec((tr, tf), lambda i, f: (i, f)),
        ],
        out_shape=[jax.ShapeDtypeStruct((m, D_MODEL), F32),
                   jax.ShapeDtypeStruct((n_tiles * tr, D_FF), F32),
                   jax.ShapeDtypeStruct((n_tiles * tr, D_FF), F32)],
        scratch_shapes=scratch,
        compiler_params=_params("parallel", "arbitrary"),
        name="ffn",
    )(*args)


def _layer(x, nb, nc, rows, valid, states, w, *, tm_in, tm_merge, tm_ffn, seq_tiles, ffn_state):
    conv_state, s0, c0, n0, m0 = states
    big, gates = _inproj(x, w["norm_mix"], w["w_big"], w["w_small"], w["gate_bias"], w["a_log"],
                         tm=tm_in, tn=512)
    go, s_new = _gdn(big, gates, conv_state, s0, w["gdn_conv_w"], w["gdn_norm"],
                     nb=nb, nc=nc, rows=rows, valid=valid)
    mh, c_new, n_new, m_new = _mlstm(big, gates, c0, n0, m0, nb=nb, nc=nc, rows=rows, valid=valid)
    x1 = _merge(go, mh, big, x, w["w_a"], w["w_b"], w["w_out"], tm=tm_merge, tn=512)
    y, upg, upv = _ffn(x1, w["norm_ffn"], w["w_up"], w["ffn_conv_w"], w["w_down"], w["norm_final"],
                       ffn_state, tm=tm_ffn, tf=512, seq_tiles=seq_tiles)
    return y, big, (s_new, c_new, n_new, m_new.reshape(nb, HEADS)), jnp.concatenate([upg, upv], axis=-1)


def kernel(x_prompt, x_sample, state_gdn_conv, state_gdn_S, state_ml_C, state_ml_n, state_ml_m, state_ffn_conv,
           norm_mix_w, w_in, gdn_conv_w, gdn_A_log, gdn_dt_bias, gdn_norm_w, w_branch_a, ml_b_i, ml_b_f,
           w_branch_b, w_out, norm_ffn_w, w_up, ffn_conv_w, w_down, norm_final_w):
    assert w_in.shape[0] == 1, "single-layer step"
    bp, tp, _ = x_prompt.shape
    bs, ts, _ = x_sample.shape
    assert ts <= SUBLANES and tp % CHUNK == 0

    wi = w_in[0]
    o = 0
    parts = {}
    for name, size in (("gqkv", QKV), ("gz", QK), ("ga", HEADS), ("gb", HEADS), ("mqkv", QKV),
                       ("mi", HEADS), ("mf", HEADS), ("mo", QK), ("gA", D_MODEL), ("gB", D_MODEL)):
        parts[name] = wi[:, o:o + size]
        o += size
    zeros8 = jnp.zeros((HEADS,), F32)
    w = {
        "norm_mix": norm_mix_w[0][None, :],
        "w_big": jnp.concatenate([parts[n] for n in ("gqkv", "mqkv", "gz", "mo", "gA", "gB")], axis=1).astype(BF16),
        "w_small": jnp.concatenate([parts[n] for n in ("ga", "gb", "mi", "mf")], axis=1).astype(BF16),
        "gate_bias": jnp.concatenate([gdn_dt_bias[0], zeros8, ml_b_i[0], ml_b_f[0]])[None, :],
        "a_log": jnp.concatenate([gdn_A_log[0], zeros8, zeros8, zeros8])[None, :],
        "gdn_conv_w": gdn_conv_w[0],
        "gdn_norm": gdn_norm_w[0][None, :],
        "w_a": w_branch_a[0].astype(BF16),
        "w_b": w_branch_b[0].astype(BF16),
        "w_out": w_out[0].astype(BF16),
        "norm_ffn": norm_ffn_w[0][None, :],
        "w_up": w_up[0].astype(BF16),
        "ffn_conv_w": ffn_conv_w[0],
        "w_down": w_down[0].astype(BF16),
        "norm_final": norm_final_w[None, :],
    }

    xp = x_prompt.reshape(bp * tp, D_MODEL)
    p_states = (jnp.zeros((bp, GDN_CONV - 1, QKV), F32), jnp.zeros((bp, HEADS, DH, DH), F32),
                jnp.zeros((bp, HEADS, DH, DH), F32), jnp.zeros((bp, HEADS, DH), F32),
                jnp.zeros((bp, 1, HEADS), F32))
    tm_ffn = 512
    yp, big_p, (ps, pc, pn, pm), up_p = _layer(
        xp, bp, tp // CHUNK, CHUNK, CHUNK, p_states, w,
        tm_in=1024, tm_merge=256, tm_ffn=tm_ffn, seq_tiles=tp // tm_ffn, ffn_state=None)
    y_prompt = yp.reshape(bp, tp, D_MODEL)
    p_gdn_conv = big_p[:, :QKV].reshape(bp, tp, QKV)[:, tp - (GDN_CONV - 1):, :]
    up_p = up_p.reshape(bp, tp // tm_ffn, SUBLANES, 2 * D_FF)
    p_ffn_conv = up_p[:, -1, SUBLANES - (FFN_CONV - 1):, :]

    padr = SUBLANES - ts
    xs = jnp.pad(x_sample, ((0, 0), (0, padr), (0, 0))).reshape(bs * SUBLANES, D_MODEL)
    st_ffn = jnp.pad(state_ffn_conv[0], ((0, 0), (0, SUBLANES - (FFN_CONV - 1)), (0, 0)))
    st_ffn = st_ffn.reshape(bs * SUBLANES, 2 * D_FF)
    s_states = (state_gdn_conv[0], state_gdn_S[0], state_ml_C[0], state_ml_n[0],
                state_ml_m[0].reshape(bs, 1, HEADS))
    ys, big_s, (ss, sc, sn, sm), up_s = _layer(
        xs, bs, 1, SUBLANES, ts, s_states, w,
        tm_in=bs * SUBLANES, tm_merge=256, tm_ffn=512, seq_tiles=1, ffn_state=st_ffn)
    y_sample = ys.reshape(bs, SUBLANES, D_MODEL)[:, :ts, :]
    xq = jnp.concatenate([state_gdn_conv[0], big_s[:, :QKV].reshape(bs, SUBLANES, QKV)[:, :ts, :]], axis=1)
    s_gdn_conv = xq[:, ts:, :]
    xf = jnp.concatenate([state_ffn_conv[0], up_s.reshape(bs, SUBLANES, 2 * D_FF)[:, :ts, :]], axis=1)
    s_ffn_conv = xf[:, ts:, :]

    return (y_prompt, y_sample,
            p_gdn_conv[None], ps[None], pc[None], pn[None], pm[None], p_ffn_conv[None],
            s_gdn_conv[None], ss[None], sc[None], sn[None], sm[None], s_ffn_conv[None])
```

```python
import functools
import math

import jax
import jax.numpy as jnp
from jax import lax
from jax.experimental import pallas as pl
from jax.experimental.pallas import tpu as pltpu

F32 = jnp.float32
BF16 = jnp.bfloat16

D_MODEL = 2048
HEADS = 8
DH = 128
QK = HEADS * DH
QKV = 3 * QK
D_FF = 5632
GDN_CONV = 4
FFN_CONV = 3
EPS = 1e-6
CHUNK = 64
N_GATES = 4 * HEADS
SUBLANES = 8
NEG = -1e30

BIG_COLS = 2 * QKV + 2 * QK + 2 * D_MODEL

VMEM_LIMIT = 56 * 1024 * 1024


def _params(*sem):
    return pltpu.CompilerParams(dimension_semantics=sem, vmem_limit_bytes=VMEM_LIMIT)


def _dot(a, b):
    return jnp.dot(a.astype(BF16), b.astype(BF16), preferred_element_type=F32)


def _dot_nt(a, b):
    return lax.dot_general(a.astype(BF16), b.astype(BF16), (((1,), (1,)), ((), ())),
                           preferred_element_type=F32)


def _dot_tn(a, b):
    return lax.dot_general(a.astype(BF16), b.astype(BF16), (((0,), (0,)), ((), ())),
                           preferred_element_type=F32)


def _softplus(x):
    return jnp.maximum(x, 0.0) + jnp.log1p(jnp.exp(-jnp.abs(x)))


def _rms(x, w):
    return x * lax.rsqrt(jnp.mean(x * x, axis=-1, keepdims=True) + EPS) * w


def _inproj_kernel(x_ref, nw_ref, wbig_ref, wsm_ref, gbias_ref, alog_ref, big_ref, gates_ref, h_scr):
    @pl.when(pl.program_id(1) == 0)
    def _():
        hb = _rms(x_ref[...], nw_ref[...]).astype(BF16)
        h_scr[...] = hb
        raw = jnp.dot(hb, wsm_ref[...], preferred_element_type=F32)
        lane = lax.broadcasted_iota(jnp.int32, raw.shape, 1)
        z = raw + gbias_ref[...]
        g = -jnp.exp(alog_ref[...]) * _softplus(z)
        beta = jax.nn.sigmoid(raw)
        lf = -_softplus(-z)
        gates_ref[...] = jnp.where(lane < HEADS, g,
                                   jnp.where(lane < 2 * HEADS, beta,
                                             jnp.where(lane < 3 * HEADS, z, lf)))

    big_ref[...] = jnp.dot(h_scr[...], wbig_ref[...], preferred_element_type=F32)


def _inproj(x, nw, wbig, wsm, gbias, alog, *, tm, tn):
    m = x.shape[0]
    return pl.pallas_call(
        _inproj_kernel,
        grid=(m // tm, BIG_COLS // tn),
        in_specs=[
            pl.BlockSpec((tm, D_MODEL), lambda i, j: (i, 0)),
            pl.BlockSpec((1, D_MODEL), lambda i, j: (0, 0)),
            pl.BlockSpec((D_MODEL, tn), lambda i, j: (0, j)),
            pl.BlockSpec((D_MODEL, N_GATES), lambda i, j: (0, 0)),
            pl.BlockSpec((1, N_GATES), lambda i, j: (0, 0)),
            pl.BlockSpec((1, N_GATES), lambda i, j: (0, 0)),
        ],
        out_specs=[
            pl.BlockSpec((tm, tn), lambda i, j: (i, j)),
            pl.BlockSpec((tm, N_GATES), lambda i, j: (i, 0)),
        ],
        out_shape=[jax.ShapeDtypeStruct((m, BIG_COLS), F32),
                   jax.ShapeDtypeStruct((m, N_GATES), F32)],
        scratch_shapes=[pltpu.VMEM((tm, D_MODEL), BF16)],
        compiler_params=_params("parallel", "arbitrary"),
        name="inproj",
    )(x, nw, wbig, wsm, gbias, alog)


def _group_masks(n, rows):
    r = n * rows
    shift = rows.bit_length() - 1
    ri = lax.broadcasted_iota(jnp.int32, (r, r), 0)
    ci = lax.broadcasted_iota(jnp.int32, (r, r), 1)
    same = lax.shift_right_logical(ri, shift) == lax.shift_right_logical(ci, shift)
    return ri == ci, same & (ci <= ri), same & (ci < ri)


def _row_block(nrows, rows, n):
    shift = rows.bit_length() - 1
    ri = lax.broadcasted_iota(jnp.int32, (nrows, 1), 0)
    return lax.shift_right_logical(ri, shift) & (n - 1)


def _row_of(col, eye):
    return jnp.sum(jnp.where(eye, col, 0.0), axis=0, keepdims=True)


def _cumsum_col(col, eye, incl):
    return jnp.sum(jnp.where(incl, _row_of(col, eye), 0.0), axis=1, keepdims=True)


def _per_row(vals, rows):
    return jnp.concatenate([jnp.broadcast_to(v, (rows, 1)) for v in vals], axis=0)


def _block_diag(x, rblk, n):
    return jnp.concatenate([jnp.where(rblk == c, x, 0.0) for c in range(n)], axis=1)


def _conv_silu(hist_ref, s, cw_ref, col0, rows):
    cols = slice(col0, col0 + DH)
    base = SUBLANES - GDN_CONV + 1
    acc = hist_ref[s, pl.ds(base, rows), cols] * cw_ref[0:1, cols]
    for j in range(1, GDN_CONV):
        acc = acc + hist_ref[s, pl.ds(base + j, rows), cols] * cw_ref[j:j + 1, cols]
    return acc * jax.nn.sigmoid(acc)


def _problems(nseq, gsz):
    probs = [(s, h) for s in range(nseq) for h in range(HEADS)]
    return [probs[i:i + gsz] for i in range(0, len(probs), gsz)]


def _gdn_kernel(qkv_ref, z_ref, gates_ref, cst_ref, s0_ref, cw_ref, nw_ref, o_ref, s_ref, hist,
                *, nseq, rows, valid, levels, gsz):
    c = pl.program_id(1)
    keep = GDN_CONV - 1

    @pl.when(c == 0)
    def _():
        hist[:, pl.ds(SUBLANES - keep, keep), :] = cst_ref[...]
        s_ref[...] = s0_ref[...]

    for s in range(nseq):
        hist[s, pl.ds(SUBLANES, rows), :] = qkv_ref[pl.ds(s * rows, rows), :]

    r = gsz * rows
    eye, incl, strict = _group_masks(gsz, rows)
    eye_f = eye.astype(F32)
    rvalid = (lax.broadcasted_iota(jnp.int32, (r, 1), 0) & (rows - 1)) < valid
    rblk = _row_block(r, rows, gsz)
    rblk2 = _row_block(2 * r, rows, gsz)
    gates = gates_ref[...]

    for group in _problems(nseq, gsz):
        def stack(fn):
            return jnp.concatenate([fn(s, h) for s, h in group], axis=0)

        q = stack(lambda s, h: _conv_silu(hist, s, cw_ref, h * DH, rows))
        k = stack(lambda s, h: _conv_silu(hist, s, cw_ref, QK + h * DH, rows))
        v = stack(lambda s, h: _conv_silu(hist, s, cw_ref, 2 * QK + h * DH, rows))
        q = q * lax.rsqrt(jnp.sum(q * q, axis=-1, keepdims=True) + EPS) * (DH ** -0.5)
        k = k * lax.rsqrt(jnp.sum(k * k, axis=-1, keepdims=True) + EPS)
        g = jnp.where(rvalid, stack(lambda s, h: gates[s * rows:(s + 1) * rows, h:h + 1]), 0.0)
        beta = jnp.where(rvalid, stack(lambda s, h: gates[s * rows:(s + 1) * rows, HEADS + h:HEADS + h + 1]), 0.0)

        gc = _cumsum_col(g, eye, incl)
        gr = _row_of(gc, eye)
        decay = jnp.where(incl, jnp.exp(jnp.where(incl, gc - gr, 0.0)), 0.0)
        kq = _dot_nt(jnp.concatenate([k, q], axis=0), k)
        a = jnp.where(strict, beta * kq[:r] * decay, 0.0)
        qk = kq[r:] * decay
        bk = -a
        t = eye_f + bk
        if levels >= 2:
            bk = _dot(bk, bk)
            for _ in range(2, levels):
                tb = _dot(jnp.concatenate([t, bk], axis=0), bk)
                t = t + tb[:r]
                bk = tb[r:]
            t = t + _dot(t, bk)
        eg = jnp.exp(gc)
        uw = _dot(t, jnp.concatenate([v * beta, k * (beta * eg)], axis=1))
        u, w = uw[:, :DH], uw[:, DH:]

        s_old = [s_ref[s, h] for s, h in group]
        ws = _dot(_block_diag(jnp.concatenate([w, q * eg], axis=0), rblk2, gsz),
                  jnp.concatenate(s_old, axis=0))
        v_new = u - ws[:r]
        o = ws[r:] + _dot(qk, v_new)
        g_last = [gc[(i + 1) * rows - 1:(i + 1) * rows, :] for i in range(gsz)]
        kd = k * jnp.exp(_per_row(g_last, rows) - gc)
        s_upd = _dot_tn(_block_diag(kd, rblk, gsz), v_new)
        for i, (s, h) in enumerate(group):
            s_ref[s, h] = s_old[i] * jnp.exp(g_last[i]) + s_upd[i * DH:(i + 1) * DH]

        zs = stack(lambda s, h: z_ref[pl.ds(s * rows, rows), h * DH:(h + 1) * DH])
        on = _rms(o, nw_ref[...]) * (zs * jax.nn.sigmoid(zs))
        for i, (s, h) in enumerate(group):
            o_ref[pl.ds(s * rows, rows), h * DH:(h + 1) * DH] = on[i * rows:(i + 1) * rows]

    for s in range(nseq):
        hist[s, pl.ds(SUBLANES - keep, keep), :] = hist[s, pl.ds(SUBLANES + rows - keep, keep), :]


def _gdn(big, gates, conv_state, s0, conv_w, norm_w, *, nb, nc, nseq, rows, valid, gsz):
    levels = max(1, math.ceil(math.log2(valid)))
    kern = functools.partial(_gdn_kernel, nseq=nseq, rows=rows, valid=valid, levels=levels, gsz=gsz)
    br = nseq * rows
    return pl.pallas_call(
        kern,
        grid=(nb // nseq, nc),
        in_specs=[
            pl.BlockSpec((br, QKV), lambda b, c: (b * nc + c, 0)),
            pl.BlockSpec((br, QK), lambda b, c: (b * nc + c, 2 * QKV // QK)),
            pl.BlockSpec((br, N_GATES), lambda b, c: (b * nc + c, 0)),
            pl.BlockSpec((nseq, GDN_CONV - 1, QKV), lambda b, c: (b, 0, 0)),
            pl.BlockSpec((nseq, HEADS, DH, DH), lambda b, c: (b, 0, 0, 0)),
            pl.BlockSpec((GDN_CONV, QKV), lambda b, c: (0, 0)),
            pl.BlockSpec((1, DH), lambda b, c: (0, 0)),
        ],
        out_specs=[
            pl.BlockSpec((br, QK), lambda b, c: (b * nc + c, 0)),
            pl.BlockSpec((nseq, HEADS, DH, DH), lambda b, c: (b, 0, 0, 0)),
        ],
        out_shape=[jax.ShapeDtypeStruct((nb * nc * rows, QK), F32),
                   jax.ShapeDtypeStruct((nb, HEADS, DH, DH), F32)],
        scratch_shapes=[pltpu.VMEM((nseq, SUBLANES + rows, QKV), F32)],
        compiler_params=_params("parallel", "arbitrary"),
        name="gdn",
    )(big, big, gates, conv_state, s0, conv_w, norm_w)


def _mlstm_kernel(qkv_ref, og_ref, gates_ref, c0_ref, n0_ref, m0_ref, h_ref, c_ref, n_ref, m_ref,
                  *, nseq, rows, valid, gsz):
    @pl.when(pl.program_id(1) == 0)
    def _():
        c_ref[...] = c0_ref[...]
        n_ref[...] = n0_ref[...]
        m_ref[...] = m0_ref[...]

    r = gsz * rows
    eye, incl, _ = _group_masks(gsz, rows)
    rvalid = (lax.broadcasted_iota(jnp.int32, (r, 1), 0) & (rows - 1)) < valid
    rblk = _row_block(r, rows, gsz)
    gates = gates_ref[...]
    m_all = m_ref[...]

    for group in _problems(nseq, gsz):
        def stack(fn):
            return jnp.concatenate([fn(s, h) for s, h in group], axis=0)

        q = stack(lambda s, h: qkv_ref[pl.ds(s * rows, rows), h * DH:(h + 1) * DH])
        k = stack(lambda s, h: qkv_ref[pl.ds(s * rows, rows), QK + h * DH:QK + (h + 1) * DH]) * (DH ** -0.5)
        v = stack(lambda s, h: qkv_ref[pl.ds(s * rows, rows), 2 * QK + h * DH:2 * QK + (h + 1) * DH])
        ig = jnp.where(rvalid, stack(lambda s, h: gates[s * rows:(s + 1) * rows, 2 * HEADS + h:2 * HEADS + h + 1]), NEG)
        lf = jnp.where(rvalid, stack(lambda s, h: gates[s * rows:(s + 1) * rows, 3 * HEADS + h:3 * HEADS + h + 1]), 0.0)

        bc = _cumsum_col(lf, eye, incl)
        br = _row_of(bc, eye)
        igr = _row_of(ig, eye)
        d_log = jnp.where(incl, bc - br + igr, NEG)
        d_max = jnp.max(d_log, axis=1, keepdims=True)
        b_last = [bc[(i + 1) * rows - 1:(i + 1) * rows, :] for i in range(gsz)]
        e_log = _per_row(b_last, rows) - bc + ig
        e_max = [jnp.max(e_log[i * rows:(i + 1) * rows], axis=0, keepdims=True) for i in range(gsz)]
        qk = _dot_nt(q, k)

        c_old = [c_ref[s, h] for s, h in group]
        n_old = [n_ref[s, h:h + 1, :] for s, h in group]
        m_old = [m_all[s, :, h:h + 1] for s, h in group]
        inter = bc + _per_row(m_old, rows)
        mt = jnp.maximum(inter, d_max)
        wi = jnp.exp(inter - mt)
        p = jnp.where(incl, jnp.exp(d_log - mt), 0.0) * qk
        qc = _dot_nt(_block_diag(q, rblk, gsz), jnp.concatenate(c_old, axis=1))
        num = wi * qc + _dot(p, v)
        n_rows = jnp.concatenate([jnp.broadcast_to(nv, (rows, DH)) for nv in n_old], axis=0)
        den = wi * jnp.sum(q * n_rows, axis=-1, keepdims=True) + jnp.sum(p, axis=-1, keepdims=True)
        hv = num / jnp.maximum(jnp.abs(den), jnp.exp(-mt))

        m_new = [jnp.maximum(b_last[i] + m_old[i], e_max[i]) for i in range(gsz)]
        fw = [jnp.exp(b_last[i] + m_old[i] - m_new[i]) for i in range(gsz)]
        sw = jnp.exp(e_log - _per_row(m_new, rows))
        c_upd = _dot_tn(sw * v, _block_diag(k, rblk, gsz))
        swk = sw * k
        for i, (s, h) in enumerate(group):
            c_ref[s, h] = fw[i] * c_old[i] + c_upd[:, i * DH:(i + 1) * DH]
            n_ref[s, h:h + 1, :] = fw[i] * n_old[i] + jnp.sum(swk[i * rows:(i + 1) * rows], axis=0, keepdims=True)
            m_ref[s, :, h:h + 1] = m_new[i]

        og = stack(lambda s, h: og_ref[pl.ds(s * rows, rows), h * DH:(h + 1) * DH])
        hg = hv * jax.nn.sigmoid(og)
        for i, (s, h) in enumerate(group):
            h_ref[pl.ds(s * rows, rows), h * DH:(h + 1) * DH] = hg[i * rows:(i + 1) * rows]


def _mlstm(big, gates, c0, n0, m0, *, nb, nc, nseq, rows, valid, gsz):
    kern = functools.partial(_mlstm_kernel, nseq=nseq, rows=rows, valid=valid, gsz=gsz)
    br = nseq * rows
    return pl.pallas_call(
        kern,
        grid=(nb // nseq, nc),
        in_specs=[
            pl.BlockSpec((br, QKV), lambda b, c: (b * nc + c, 1)),
            pl.BlockSpec((br, QK), lambda b, c: (b * nc + c, 2 * QKV // QK + 1)),
            pl.BlockSpec((br, N_GATES), lambda b, c: (b * nc + c, 0)),
            pl.BlockSpec((nseq, HEADS, DH, DH), lambda b, c: (b, 0, 0, 0)),
            pl.BlockSpec((nseq, HEADS, DH), lambda b, c: (b, 0, 0)),
            pl.BlockSpec((nseq, 1, HEADS), lambda b, c: (b, 0, 0)),
        ],
        out_specs=[
            pl.BlockSpec((br, QK), lambda b, c: (b * nc + c, 0)),
            pl.BlockSpec((nseq, HEADS, DH, DH), lambda b, c: (b, 0, 0, 0)),
            pl.BlockSpec((nseq, HEADS, DH), lambda b, c: (b, 0, 0)),
            pl.BlockSpec((nseq, 1, HEADS), lambda b, c: (b, 0, 0)),
        ],
        out_shape=[jax.ShapeDtypeStruct((nb * nc * rows, QK), F32),
                   jax.ShapeDtypeStruct((nb, HEADS, DH, DH), F32),
                   jax.ShapeDtypeStruct((nb, HEADS, DH), F32),
                   jax.ShapeDtypeStruct((nb, 1, HEADS), F32)],
        compiler_params=_params("parallel", "arbitrary"),
        name="mlstm",
    )(big, big, gates, c0, n0, m0)


def _merge_kernel(go_ref, mh_ref, ga_ref, gb_ref, x_ref, wa_ref, wb_ref, wo_ref, x1_ref, mix_scr):
    @pl.when(pl.program_id(1) == 0)
    def _():
        ya = jnp.dot(go_ref[...].astype(BF16), wa_ref[...], preferred_element_type=F32)
        yb = jnp.dot(mh_ref[...].astype(BF16), wb_ref[...], preferred_element_type=F32)
        mixed = jax.nn.sigmoid(ga_ref[...]) * ya + jax.nn.sigmoid(gb_ref[...]) * yb
        mix_scr[...] = mixed.astype(BF16)

    x1_ref[...] = x_ref[...] + jnp.dot(mix_scr[...], wo_ref[...], preferred_element_type=F32)


def _merge(go, mh, big, x, wa, wb, wo, *, tm, tn):
    m = x.shape[0]
    ga_blk = (2 * QKV + 2 * QK) // D_MODEL
    return pl.pallas_call(
        _merge_kernel,
        grid=(m // tm, D_MODEL // tn),
        in_specs=[
            pl.BlockSpec((tm, QK), lambda i, j: (i, 0)),
            pl.BlockSpec((tm, QK), lambda i, j: (i, 0)),
            pl.BlockSpec((tm, D_MODEL), lambda i, j: (i, ga_blk)),
            pl.BlockSpec((tm, D_MODEL), lambda i, j: (i, ga_blk + 1)),
            pl.BlockSpec((tm, tn), lambda i, j: (i, j)),
            pl.BlockSpec((QK, D_MODEL), lambda i, j: (0, 0)),
            pl.BlockSpec((QK, D_MODEL), lambda i, j: (0, 0)),
            pl.BlockSpec((D_MODEL, tn), lambda i, j: (0, j)),
        ],
        out_specs=pl.BlockSpec((tm, tn), lambda i, j: (i, j)),
        out_shape=jax.ShapeDtypeStruct((m, D_MODEL), F32),
        scratch_shapes=[pltpu.VMEM((tm, D_MODEL), BF16)],
        compiler_params=_params("parallel", "arbitrary"),
        name="merge",
    )(go, mh, big, big, x, wa, wb, wo)


def _ffn_kernel(*refs, tm, tf, seq_tiles, with_state):
    if with_state:
        (x1_ref, nw_ref, wg_ref, wv_ref, cwg_ref, cwv_ref, wd_ref, fnw_ref, stg_ref, stv_ref,
         y_ref, upg_ref, upv_ref, h2_scr, hist_g, hist_v, acc_scr, st_g, st_v) = refs
    else:
        (x1_ref, halo_ref, nw_ref, wg_ref, wv_ref, cwg_ref, cwv_ref, wd_ref, fnw_ref,
         y_ref, upg_ref, upv_ref, h2_scr, hist_g, hist_v, acc_scr) = refs
    i = pl.program_id(0)
    f = pl.program_id(1)
    pad = SUBLANES

    @pl.when(f == 0)
    def _():
        if with_state:
            h2_scr[pl.ds(0, pad), :] = jnp.zeros((pad, D_MODEL), BF16)
        else:
            live = (i % seq_tiles != 0).astype(F32)
            h2_scr[pl.ds(0, pad), :] = (_rms(halo_ref[...], nw_ref[...]) * live).astype(BF16)
        h2_scr[pl.ds(pad, tm), :] = _rms(x1_ref[...], nw_ref[...]).astype(BF16)
        acc_scr[...] = jnp.zeros_like(acc_scr)

    h2 = h2_scr[...]
    hist_g[...] = jnp.dot(h2, wg_ref[...], preferred_element_type=F32)
    hist_v[...] = jnp.dot(h2, wv_ref[...], preferred_element_type=F32)

    if with_state:
        upg_ref[...] = hist_g[pl.ds(pad, tm), :]
        upv_ref[...] = hist_v[pl.ds(pad, tm), :]
        st_g[pl.ds(0, tm), :] = stg_ref[...]
        st_v[pl.ds(0, tm), :] = stv_ref[...]
        st_g[pl.ds(tm, pad), :] = jnp.zeros((pad, tf), F32)
        st_v[pl.ds(tm, pad), :] = jnp.zeros((pad, tf), F32)
        rmod = lax.broadcasted_iota(jnp.int32, (tm, 1), 0) % SUBLANES
    else:
        upg_ref[...] = hist_g[pl.ds(tm, pad), :]
        upv_ref[...] = hist_v[pl.ds(tm, pad), :]

    def conv(hist, cw_ref, st):
        prev2 = hist[pl.ds(pad - 2, tm), :]
        prev1 = hist[pl.ds(pad - 1, tm), :]
        if with_state:
            prev2 = jnp.where(rmod < 2, st[pl.ds(0, tm), :], prev2)
            prev1 = jnp.where(rmod < 1, st[pl.ds(1, tm), :], prev1)
        return (prev2 * cw_ref[0:1, :] + prev1 * cw_ref[1:2, :]) + hist[pl.ds(pad, tm), :] * cw_ref[2:3, :]

    ug = conv(hist_g, cwg_ref, st_g if with_state else None)
    uv = conv(hist_v, cwv_ref, st_v if with_state else None)
    act = (ug * jax.nn.sigmoid(ug) * uv).astype(BF16)
    acc_scr[...] += jnp.dot(act, wd_ref[...], preferred_element_type=F32)

    @pl.when(f == pl.num_programs(1) - 1)
    def _():
        y_ref[...] = _rms(x1_ref[...] + acc_scr[...], fnw_ref[...])


def _ffn(x1, nw, wup, cw, wd, fnw, state=None, *, tm, tf, seq_tiles):
    m = x1.shape[0]
    nf = D_FF // tf
    with_state = state is not None
    tr = tm if with_state else SUBLANES
    kern = functools.partial(_ffn_kernel, tm=tm, tf=tf, seq_tiles=seq_tiles, with_state=with_state)
    in_specs = [pl.BlockSpec((tm, D_MODEL), lambda i, f: (i, 0))]
    args = [x1]
    if not with_state:
        in_specs.append(pl.BlockSpec((SUBLANES, D_MODEL),
                                     lambda i, f: (jnp.maximum(i * (tm // SUBLANES) - 1, 0), 0)))
        args.append(x1)
    in_specs += [
        pl.BlockSpec((1, D_MODEL), lambda i, f: (0, 0)),
        pl.BlockSpec((D_MODEL, tf), lambda i, f: (0, f)),
        pl.BlockSpec((D_MODEL, tf), lambda i, f: (0, f + nf)),
        pl.BlockSpec((FFN_CONV, tf), lambda i, f: (0, f)),
        pl.BlockSpec((FFN_CONV, tf), lambda i, f: (0, f + nf)),
        pl.BlockSpec((tf, D_MODEL), lambda i, f: (f, 0)),
        pl.BlockSpec((1, D_MODEL), lambda i, f: (0, 0)),
    ]
    args += [nw, wup, wup, cw, cw, wd, fnw]
    scratch = [pltpu.VMEM((SUBLANES + tm, D_MODEL), BF16),
               pltpu.VMEM((SUBLANES + tm, tf), F32),
               pltpu.VMEM((SUBLANES + tm, tf), F32),
               pltpu.VMEM((tm, D_MODEL), F32)]
    if with_state:
        in_specs += [pl.BlockSpec((tm, tf), lambda i, f: (i, f)),
                     pl.BlockSpec((tm, tf), lambda i, f: (i, f + nf))]
        args += [state, state]
        scratch += [pltpu.VMEM((tm + SUBLANES, tf), F32), pltpu.VMEM((tm + SUBLANES, tf), F32)]
    n_tiles = m // tm
    return pl.pallas_call(
        kern,
        grid=(n_tiles, nf),
        in_specs=in_specs,
        out_specs=[
            pl.BlockSpec((tm, D_MODEL), lambda i, f: (i, 0)),
            pl.BlockSpec((tr, tf), lambda i, f: (i, f)),
            pl.BlockSpec((tr, tf), lambda i, f: (i, f)),
        ],
        out_shape=[jax.ShapeDtypeStruct((m, D_MODEL), F32),
                   jax.ShapeDtypeStruct((n_tiles * tr, D_FF), F32),
                   jax.ShapeDtypeStruct((n_tiles * tr, D_FF), F32)],
        scratch_shapes=scratch,
        compiler_params=_params("parallel", "arbitrary"),
        name="ffn",
    )(*args)


def _layer(x, nb, nc, nseq, rows, valid, gsz, states, w, *, tm_in, tm_merge, tm_ffn, seq_tiles, ffn_state):
    conv_state, s0, c0, n0, m0 = states
    big, gates = _inproj(x, w["norm_mix"], w["w_big"], w["w_small"], w["gate_bias"], w["a_log"],
                         tm=tm_in, tn=512)
    go, s_new = _gdn(big, gates, conv_state, s0, w["gdn_conv_w"], w["gdn_norm"],
                     nb=nb, nc=nc, nseq=nseq, rows=rows, valid=valid, gsz=gsz)
    mh, c_new, n_new, m_new = _mlstm(big, gates, c0, n0, m0,
                                     nb=nb, nc=nc, nseq=nseq, rows=rows, valid=valid, gsz=gsz)
    x1 = _merge(go, mh, big, x, w["w_a"], w["w_b"], w["w_out"], tm=tm_merge, tn=512)
    y, upg, upv = _ffn(x1, w["norm_ffn"], w["w_up"], w["ffn_conv_w"], w["w_down"], w["norm_final"],
                       ffn_state, tm=tm_ffn, tf=512, seq_tiles=seq_tiles)
    return y, big, (s_new, c_new, n_new, m_new.reshape(nb, HEADS)), jnp.concatenate([upg, upv], axis=-1)


def kernel(x_prompt, x_sample, state_gdn_conv, state_gdn_S, state_ml_C, state_ml_n, state_ml_m, state_ffn_conv,
           norm_mix_w, w_in, gdn_conv_w, gdn_A_log, gdn_dt_bias, gdn_norm_w, w_branch_a, ml_b_i, ml_b_f,
           w_branch_b, w_out, norm_ffn_w, w_up, ffn_conv_w, w_down, norm_final_w):
    assert w_in.shape[0] == 1, "single-layer step"
    bp, tp, _ = x_prompt.shape
    bs, ts, _ = x_sample.shape
    assert ts <= SUBLANES and tp % CHUNK == 0

    wi = w_in[0]
    o = 0
    parts = {}
    for name, size in (("gqkv", QKV), ("gz", QK), ("ga", HEADS), ("gb", HEADS), ("mqkv", QKV),
                       ("mi", HEADS), ("mf", HEADS), ("mo", QK), ("gA", D_MODEL), ("gB", D_MODEL)):
        parts[name] = wi[:, o:o + size]
        o += size
    zeros8 = jnp.zeros((HEADS,), F32)
    w = {
        "norm_mix": norm_mix_w[0][None, :],
        "w_big": jnp.concatenate([parts[n] for n in ("gqkv", "mqkv", "gz", "mo", "gA", "gB")], axis=1).astype(BF16),
        "w_small": jnp.concatenate([parts[n] for n in ("ga", "gb", "mi", "mf")], axis=1).astype(BF16),
        "gate_bias": jnp.concatenate([gdn_dt_bias[0], zeros8, ml_b_i[0], ml_b_f[0]])[None, :],
        "a_log": jnp.concatenate([gdn_A_log[0], zeros8, zeros8, zeros8])[None, :],
        "gdn_conv_w": gdn_conv_w[0],
        "gdn_norm": gdn_norm_w[0][None, :],
        "w_a": w_branch_a[0].astype(BF16),
        "w_b": w_branch_b[0].astype(BF16),
        "w_out": w_out[0].astype(BF16),
        "norm_ffn": norm_ffn_w[0][None, :],
        "w_up": w_up[0].astype(BF16),
        "ffn_conv_w": ffn_conv_w[0],
        "w_down": w_down[0].astype(BF16),
        "norm_final": norm_final_w[None, :],
    }

    xp = x_prompt.reshape(bp * tp, D_MODEL)
    p_states = (jnp.zeros((bp, GDN_CONV - 1, QKV), F32), jnp.zeros((bp, HEADS, DH, DH), F32),
                jnp.zeros((bp, HEADS, DH, DH), F32), jnp.zeros((bp, HEADS, DH), F32),
                jnp.zeros((bp, 1, HEADS), F32))
    tm_ffn = 512
    yp, big_p, (ps, pc, pn, pm), up_p = _layer(
        xp, bp, tp // CHUNK, 1, CHUNK, CHUNK, 4, p_states, w,
        tm_in=1024, tm_merge=256, tm_ffn=tm_ffn, seq_tiles=tp // tm_ffn, ffn_state=None)
    y_prompt = yp.reshape(bp, tp, D_MODEL)
    p_gdn_conv = big_p[:, :QKV].reshape(bp, tp, QKV)[:, tp - (GDN_CONV - 1):, :]
    up_p = up_p.reshape(bp, tp // tm_ffn, SUBLANES, 2 * D_FF)
    p_ffn_conv = up_p[:, -1, SUBLANES - (FFN_CONV - 1):, :]

    padr = SUBLANES - ts
    xs = jnp.pad(x_sample, ((0, 0), (0, padr), (0, 0))).reshape(bs * SUBLANES, D_MODEL)
    st_ffn = jnp.pad(state_ffn_conv[0], ((0, 0), (0, SUBLANES - (FFN_CONV - 1)), (0, 0)))
    st_ffn = st_ffn.reshape(bs * SUBLANES, 2 * D_FF)
    s_states = (state_gdn_conv[0], state_gdn_S[0], state_ml_C[0], state_ml_n[0],
                state_ml_m[0].reshape(bs, 1, HEADS))
    ys, big_s, (ss, sc, sn, sm), up_s = _layer(
        xs, bs, 1, 4, SUBLANES, ts, HEADS, s_states, w,
        tm_in=bs * SUBLANES, tm_merge=256, tm_ffn=512, seq_tiles=1, ffn_state=st_ffn)
    y_sample = ys.reshape(bs, SUBLANES, D_MODEL)[:, :ts, :]
    xq = jnp.concatenate([state_gdn_conv[0], big_s[:, :QKV].reshape(bs, SUBLANES, QKV)[:, :ts, :]], axis=1)
    s_gdn_conv = xq[:, ts:, :]
    xf = jnp.concatenate([state_ffn_conv[0], up_s.reshape(bs, SUBLANES, 2 * D_FF)[:, :ts, :]], axis=1)
    s_ffn_conv = xf[:, ts:, :]

    return (y_prompt, y_sample,
            p_gdn_conv[None], ps[None], pc[None], pn[None], pm[None], p_ffn_conv[None],
            s_gdn_conv[None], ss[None], sc[None], sn[None], sm[None], s_ffn_conv[None])
```

```python
import functools
import math

import jax
import jax.numpy as jnp
from jax import lax
from jax.experimental import pallas as pl
from jax.experimental.pallas import tpu as pltpu

F32 = jnp.float32
BF16 = jnp.bfloat16

D_MODEL = 2048
HEADS = 8
DH = 128
QK = HEADS * DH
QKV = 3 * QK
D_FF = 5632
GDN_CONV = 4
FFN_CONV = 3
EPS = 1e-6
CHUNK = 64
N_GATES = 4 * HEADS
SUBLANES = 8
NEG = -1e30

BIG_COLS = 2 * QKV + 2 * QK + 2 * D_MODEL

VMEM_LIMIT = 56 * 1024 * 1024


def _params(*sem):
    return pltpu.CompilerParams(dimension_semantics=sem, vmem_limit_bytes=VMEM_LIMIT)


def _dot(a, b):
    return jnp.dot(a.astype(BF16), b.astype(BF16), preferred_element_type=F32)


def _dot_nt(a, b):
    return lax.dot_general(a.astype(BF16), b.astype(BF16), (((1,), (1,)), ((), ())),
                           preferred_element_type=F32)


def _dot_tn(a, b):
    return lax.dot_general(a.astype(BF16), b.astype(BF16), (((0,), (0,)), ((), ())),
                           preferred_element_type=F32)


def _softplus(x):
    return jnp.maximum(x, 0.0) + jnp.log1p(jnp.exp(-jnp.abs(x)))


def _rms(x, w):
    return x * lax.rsqrt(jnp.mean(x * x, axis=-1, keepdims=True) + EPS) * w


def _inproj_kernel(x_ref, nw_ref, wbig_ref, wsm_ref, gbias_ref, alog_ref, big_ref, gates_ref, h_scr):
    @pl.when(pl.program_id(1) == 0)
    def _():
        hb = _rms(x_ref[...], nw_ref[...]).astype(BF16)
        h_scr[...] = hb
        raw = jnp.dot(hb, wsm_ref[...], preferred_element_type=F32)
        lane = lax.broadcasted_iota(jnp.int32, raw.shape, 1)
        z = raw + gbias_ref[...]
        g = -jnp.exp(alog_ref[...]) * _softplus(z)
        beta = jax.nn.sigmoid(raw)
        lf = -_softplus(-z)
        gates_ref[...] = jnp.where(lane < HEADS, g,
                                   jnp.where(lane < 2 * HEADS, beta,
                                             jnp.where(lane < 3 * HEADS, z, lf)))

    big_ref[...] = jnp.dot(h_scr[...], wbig_ref[...], preferred_element_type=F32)


def _inproj(x, nw, wbig, wsm, gbias, alog, *, tm, tn):
    m = x.shape[0]
    return pl.pallas_call(
        _inproj_kernel,
        grid=(m // tm, BIG_COLS // tn),
        in_specs=[
            pl.BlockSpec((tm, D_MODEL), lambda i, j: (i, 0)),
            pl.BlockSpec((1, D_MODEL), lambda i, j: (0, 0)),
            pl.BlockSpec((D_MODEL, tn), lambda i, j: (0, j)),
            pl.BlockSpec((D_MODEL, N_GATES), lambda i, j: (0, 0)),
            pl.BlockSpec((1, N_GATES), lambda i, j: (0, 0)),
            pl.BlockSpec((1, N_GATES), lambda i, j: (0, 0)),
        ],
        out_specs=[
            pl.BlockSpec((tm, tn), lambda i, j: (i, j)),
            pl.BlockSpec((tm, N_GATES), lambda i, j: (i, 0)),
        ],
        out_shape=[jax.ShapeDtypeStruct((m, BIG_COLS), F32),
                   jax.ShapeDtypeStruct((m, N_GATES), F32)],
        scratch_shapes=[pltpu.VMEM((tm, D_MODEL), BF16)],
        compiler_params=_params("parallel", "arbitrary"),
        name="inproj",
    )(x, nw, wbig, wsm, gbias, alog)


def _group_masks(n, rows):
    r = n * rows
    shift = rows.bit_length() - 1
    ri = lax.broadcasted_iota(jnp.int32, (r, r), 0)
    ci = lax.broadcasted_iota(jnp.int32, (r, r), 1)
    same = lax.shift_right_logical(ri, shift) == lax.shift_right_logical(ci, shift)
    return ri == ci, same & (ci <= ri), same & (ci < ri)


def _row_block(nrows, rows, n):
    shift = rows.bit_length() - 1
    ri = lax.broadcasted_iota(jnp.int32, (nrows, 1), 0)
    return lax.shift_right_logical(ri, shift) & (n - 1)


def _row_of(col, eye):
    return jnp.sum(jnp.where(eye, col, 0.0), axis=0, keepdims=True)


def _cumsum_col(col, eye, incl):
    return jnp.sum(jnp.where(incl, _row_of(col, eye), 0.0), axis=1, keepdims=True)


def _per_row(vals, rows):
    return jnp.concatenate([jnp.broadcast_to(v, (rows, 1)) for v in vals], axis=0)


def _block_diag(x, rblk, n):
    return jnp.concatenate([jnp.where(rblk == c, x, 0.0) for c in range(n)], axis=1)


def _conv_silu(hist_ref, s, cw_ref, col0, rows):
    cols = slice(col0, col0 + DH)
    base = SUBLANES - GDN_CONV + 1
    acc = hist_ref[s, pl.ds(base, rows), cols] * cw_ref[0:1, cols]
    for j in range(1, GDN_CONV):
        acc = acc + hist_ref[s, pl.ds(base + j, rows), cols] * cw_ref[j:j + 1, cols]
    return acc * jax.nn.sigmoid(acc)


def _problems(nseq, gsz):
    probs = [(s, h) for s in range(nseq) for h in range(HEADS)]
    return [probs[i:i + gsz] for i in range(0, len(probs), gsz)]


def _gdn_kernel(qkv_ref, z_ref, gates_ref, cst_ref, s0_ref, cw_ref, nw_ref, o_ref, s_ref, hist,
                *, nseq, rows, valid, levels, gsz):
    c = pl.program_id(1)
    keep = GDN_CONV - 1

    @pl.when(c == 0)
    def _():
        hist[:, pl.ds(SUBLANES - keep, keep), :] = cst_ref[...]
        s_ref[...] = s0_ref[...]

    for s in range(nseq):
        hist[s, pl.ds(SUBLANES, rows), :] = qkv_ref[pl.ds(s * rows, rows), :]

    r = gsz * rows
    eye, incl, strict = _group_masks(gsz, rows)
    eye_f = eye.astype(F32)
    rvalid = (lax.broadcasted_iota(jnp.int32, (r, 1), 0) & (rows - 1)) < valid
    rblk = _row_block(r, rows, gsz)
    rblk2 = _row_block(2 * r, rows, gsz)
    gates = gates_ref[...]

    for group in _problems(nseq, gsz):
        def stack(fn):
            return jnp.concatenate([fn(s, h) for s, h in group], axis=0)

        q = stack(lambda s, h: _conv_silu(hist, s, cw_ref, h * DH, rows))
        k = stack(lambda s, h: _conv_silu(hist, s, cw_ref, QK + h * DH, rows))
        v = stack(lambda s, h: _conv_silu(hist, s, cw_ref, 2 * QK + h * DH, rows))
        q = q * lax.rsqrt(jnp.sum(q * q, axis=-1, keepdims=True) + EPS) * (DH ** -0.5)
        k = k * lax.rsqrt(jnp.sum(k * k, axis=-1, keepdims=True) + EPS)
        g = jnp.where(rvalid, stack(lambda s, h: gates[s * rows:(s + 1) * rows, h:h + 1]), 0.0)
        beta = jnp.where(rvalid, stack(lambda s, h: gates[s * rows:(s + 1) * rows, HEADS + h:HEADS + h + 1]), 0.0)

        gc = _cumsum_col(g, eye, incl)
        gr = _row_of(gc, eye)
        decay = jnp.where(incl, jnp.exp(jnp.where(incl, gc - gr, 0.0)), 0.0)
        kq = _dot_nt(jnp.concatenate([k, q], axis=0), k)
        a = jnp.where(strict, beta * kq[:r] * decay, 0.0)
        qk = kq[r:] * decay
        bk = -a
        t = eye_f + bk
        if levels >= 2:
            bk = _dot(bk, bk)
            for _ in range(2, levels):
                tb = _dot(jnp.concatenate([t, bk], axis=0), bk)
                t = t + tb[:r]
                bk = tb[r:]
            t = t + _dot(t, bk)
        eg = jnp.exp(gc)
        uw = _dot(t, jnp.concatenate([v * beta, k * (beta * eg)], axis=1))
        u, w = uw[:, :DH], uw[:, DH:]

        s_old = [s_ref[s, h] for s, h in group]
        ws = _dot(_block_diag(jnp.concatenate([w, q * eg], axis=0), rblk2, gsz),
                  jnp.concatenate(s_old, axis=0))
        v_new = u - ws[:r]
        o = ws[r:] + _dot(qk, v_new)
        g_last = [gc[(i + 1) * rows - 1:(i + 1) * rows, :] for i in range(gsz)]
        kd = k * jnp.exp(_per_row(g_last, rows) - gc)
        s_upd = _dot_tn(_block_diag(kd, rblk, gsz), v_new)
        for i, (s, h) in enumerate(group):
            s_ref[s, h] = s_old[i] * jnp.exp(g_last[i]) + s_upd[i * DH:(i + 1) * DH]

        zs = stack(lambda s, h: z_ref[pl.ds(s * rows, rows), h * DH:(h + 1) * DH])
        on = _rms(o, nw_ref[...]) * (zs * jax.nn.sigmoid(zs))
        for i, (s, h) in enumerate(group):
            o_ref[pl.ds(s * rows, rows), h * DH:(h + 1) * DH] = on[i * rows:(i + 1) * rows]

    for s in range(nseq):
        hist[s, pl.ds(SUBLANES - keep, keep), :] = hist[s, pl.ds(SUBLANES + rows - keep, keep), :]


def _gdn(big, gates, conv_state, s0, conv_w, norm_w, *, nb, nc, nseq, rows, valid, gsz):
    levels = max(1, math.ceil(math.log2(valid)))
    kern = functools.partial(_gdn_kernel, nseq=nseq, rows=rows, valid=valid, levels=levels, gsz=gsz)
    br = nseq * rows
    return pl.pallas_call(
        kern,
        grid=(nb // nseq, nc),
        in_specs=[
            pl.BlockSpec((br, QKV), lambda b, c: (b * nc + c, 0)),
            pl.BlockSpec((br, QK), lambda b, c: (b * nc + c, 2 * QKV // QK)),
            pl.BlockSpec((br, N_GATES), lambda b, c: (b * nc + c, 0)),
            pl.BlockSpec((nseq, GDN_CONV - 1, QKV), lambda b, c: (b, 0, 0)),
            pl.BlockSpec((nseq, HEADS, DH, DH), lambda b, c: (b, 0, 0, 0)),
            pl.BlockSpec((GDN_CONV, QKV), lambda b, c: (0, 0)),
            pl.BlockSpec((1, DH), lambda b, c: (0, 0)),
        ],
        out_specs=[
            pl.BlockSpec((br, QK), lambda b, c: (b * nc + c, 0)),
            pl.BlockSpec((nseq, HEADS, DH, DH), lambda b, c: (b, 0, 0, 0)),
        ],
        out_shape=[jax.ShapeDtypeStruct((nb * nc * rows, QK), F32),
                   jax.ShapeDtypeStruct((nb, HEADS, DH, DH), F32)],
        scratch_shapes=[pltpu.VMEM((nseq, SUBLANES + rows, QKV), F32)],
        compiler_params=_params("parallel", "arbitrary"),
        name="gdn",
    )(big, big, gates, conv_state, s0, conv_w, norm_w)


def _mlstm_kernel(qkv_ref, og_ref, gates_ref, c0_ref, n0_ref, m0_ref, h_ref, c_ref, n_ref, m_ref,
                  *, nseq, rows, valid, gsz):
    @pl.when(pl.program_id(1) == 0)
    def _():
        c_ref[...] = c0_ref[...]
        n_ref[...] = n0_ref[...]
        m_ref[...] = m0_ref[...]

    r = gsz * rows
    eye, incl, _ = _group_masks(gsz, rows)
    rvalid = (lax.broadcasted_iota(jnp.int32, (r, 1), 0) & (rows - 1)) < valid
    rblk = _row_block(r, rows, gsz)
    gates = gates_ref[...]
    m_all = m_ref[...]

    for group in _problems(nseq, gsz):
        def stack(fn):
            return jnp.concatenate([fn(s, h) for s, h in group], axis=0)

        q = stack(lambda s, h: qkv_ref[pl.ds(s * rows, rows), h * DH:(h + 1) * DH])
        k = stack(lambda s, h: qkv_ref[pl.ds(s * rows, rows), QK + h * DH:QK + (h + 1) * DH]) * (DH ** -0.5)
        v = stack(lambda s, h: qkv_ref[pl.ds(s * rows, rows), 2 * QK + h * DH:2 * QK + (h + 1) * DH])
        ig = jnp.where(rvalid, stack(lambda s, h: gates[s * rows:(s + 1) * rows, 2 * HEADS + h:2 * HEADS + h + 1]), NEG)
        lf = jnp.where(rvalid, stack(lambda s, h: gates[s * rows:(s + 1) * rows, 3 * HEADS + h:3 * HEADS + h + 1]), 0.0)

        bc = _cumsum_col(lf, eye, incl)
        br = _row_of(bc, eye)
        igr = _row_of(ig, eye)
        d_log = jnp.where(incl, bc - br + igr, NEG)
        d_max = jnp.max(d_log, axis=1, keepdims=True)
        b_last = [bc[(i + 1) * rows - 1:(i + 1) * rows, :] for i in range(gsz)]
        e_log = _per_row(b_last, rows) - bc + ig
        e_max = [jnp.max(e_log[i * rows:(i + 1) * rows], axis=0, keepdims=True) for i in range(gsz)]
        qk = _dot_nt(q, k)

        c_old = [c_ref[s, h] for s, h in group]
        n_old = [n_ref[s, h:h + 1, :] for s, h in group]
        m_old = [m_all[s, :, h:h + 1] for s, h in group]
        inter = bc + _per_row(m_old, rows)
        mt = jnp.maximum(inter, d_max)
        wi = jnp.exp(inter - mt)
        p = jnp.where(incl, jnp.exp(d_log - mt), 0.0) * qk
        qc = _dot_nt(_block_diag(q, rblk, gsz), jnp.concatenate(c_old, axis=1))
        num = wi * qc + _dot(p, v)
        n_rows = jnp.concatenate([jnp.broadcast_to(nv, (rows, DH)) for nv in n_old], axis=0)
        den = wi * jnp.sum(q * n_rows, axis=-1, keepdims=True) + jnp.sum(p, axis=-1, keepdims=True)
        hv = num / jnp.maximum(jnp.abs(den), jnp.exp(-mt))

        m_new = [jnp.maximum(b_last[i] + m_old[i], e_max[i]) for i in range(gsz)]
        fw = [jnp.exp(b_last[i] + m_old[i] - m_new[i]) for i in range(gsz)]
        sw = jnp.exp(e_log - _per_row(m_new, rows))
        c_upd = _dot_tn(sw * v, _block_diag(k, rblk, gsz))
        swk = sw * k
        for i, (s, h) in enumerate(group):
            c_ref[s, h] = fw[i] * c_old[i] + c_upd[:, i * DH:(i + 1) * DH]
            n_ref[s, h:h + 1, :] = fw[i] * n_old[i] + jnp.sum(swk[i * rows:(i + 1) * rows], axis=0, keepdims=True)
            m_ref[s, :, h:h + 1] = m_new[i]

        og = stack(lambda s, h: og_ref[pl.ds(s * rows, rows), h * DH:(h + 1) * DH])
        hg = hv * jax.nn.sigmoid(og)
        for i, (s, h) in enumerate(group):
            h_ref[pl.ds(s * rows, rows), h * DH:(h + 1) * DH] = hg[i * rows:(i + 1) * rows]


def _mlstm(big, gates, c0, n0, m0, *, nb, nc, nseq, rows, valid, gsz):
    kern = functools.partial(_mlstm_kernel, nseq=nseq, rows=rows, valid=valid, gsz=gsz)
    br = nseq * rows
    return pl.pallas_call(
        kern,
        grid=(nb // nseq, nc),
        in_specs=[
            pl.BlockSpec((br, QKV), lambda b, c: (b * nc + c, 1)),
            pl.BlockSpec((br, QK), lambda b, c: (b * nc + c, 2 * QKV // QK + 1)),
            pl.BlockSpec((br, N_GATES), lambda b, c: (b * nc + c, 0)),
            pl.BlockSpec((nseq, HEADS, DH, DH), lambda b, c: (b, 0, 0, 0)),
            pl.BlockSpec((nseq, HEADS, DH), lambda b, c: (b, 0, 0)),
            pl.BlockSpec((nseq, 1, HEADS), lambda b, c: (b, 0, 0)),
        ],
        out_specs=[
            pl.BlockSpec((br, QK), lambda b, c: (b * nc + c, 0)),
            pl.BlockSpec((nseq, HEADS, DH, DH), lambda b, c: (b, 0, 0, 0)),
            pl.BlockSpec((nseq, HEADS, DH), lambda b, c: (b, 0, 0)),
            pl.BlockSpec((nseq, 1, HEADS), lambda b, c: (b, 0, 0)),
        ],
        out_shape=[jax.ShapeDtypeStruct((nb * nc * rows, QK), F32),
                   jax.ShapeDtypeStruct((nb, HEADS, DH, DH), F32),
                   jax.ShapeDtypeStruct((nb, HEADS, DH), F32),
                   jax.ShapeDtypeStruct((nb, 1, HEADS), F32)],
        compiler_params=_params("parallel", "arbitrary"),
        name="mlstm",
    )(big, big, gates, c0, n0, m0)


def _merge_kernel(go_ref, mh_ref, ga_ref, gb_ref, x_ref, wa_ref, wb_ref, wo_ref, x1_ref, mix_scr, *, nt, tn):
    j = pl.program_id(1)

    @pl.when(j < nt)
    def _():
        ya = jnp.dot(go_ref[...].astype(BF16), wa_ref[...], preferred_element_type=F32)
        yb = jnp.dot(mh_ref[...].astype(BF16), wb_ref[...], preferred_element_type=F32)
        mixed = jax.nn.sigmoid(ga_ref[...]) * ya + jax.nn.sigmoid(gb_ref[...]) * yb
        mixed = mixed.astype(BF16)
        for t in range(nt):
            @pl.when(j == t)
            def _():
                mix_scr[:, t * tn:(t + 1) * tn] = mixed

    @pl.when(j >= nt)
    def _():
        x1_ref[...] = x_ref[...] + jnp.dot(mix_scr[...], wo_ref[...], preferred_element_type=F32)


def _merge(go, mh, big, x, wa, wb, wo, *, tm, tn):
    m = x.shape[0]
    nt = D_MODEL // tn
    ga_blk = (2 * QKV + 2 * QK) // tn
    mix_j = lambda j: jnp.minimum(j, nt - 1)
    out_j = lambda j: jnp.maximum(j - nt, 0)
    return pl.pallas_call(
        functools.partial(_merge_kernel, nt=nt, tn=tn),
        grid=(m // tm, 2 * nt),
        in_specs=[
            pl.BlockSpec((tm, QK), lambda i, j: (i, 0)),
            pl.BlockSpec((tm, QK), lambda i, j: (i, 0)),
            pl.BlockSpec((tm, tn), lambda i, j: (i, ga_blk + mix_j(j))),
            pl.BlockSpec((tm, tn), lambda i, j: (i, ga_blk + nt + mix_j(j))),
            pl.BlockSpec((tm, tn), lambda i, j: (i, out_j(j))),
            pl.BlockSpec((QK, tn), lambda i, j: (0, mix_j(j))),
            pl.BlockSpec((QK, tn), lambda i, j: (0, mix_j(j))),
            pl.BlockSpec((D_MODEL, tn), lambda i, j: (0, out_j(j))),
        ],
        out_specs=pl.BlockSpec((tm, tn), lambda i, j: (i, out_j(j))),
        out_shape=jax.ShapeDtypeStruct((m, D_MODEL), F32),
        scratch_shapes=[pltpu.VMEM((tm, D_MODEL), BF16)],
        compiler_params=_params("parallel", "arbitrary"),
        name="merge",
    )(go, mh, big, big, x, wa, wb, wo)


def _ffn_kernel(*refs, tm, tf, seq_tiles, with_state):
    if with_state:
        (x1_ref, nw_ref, wg_ref, wv_ref, cwg_ref, cwv_ref, wd_ref, fnw_ref, stg_ref, stv_ref,
         y_ref, upg_ref, upv_ref, h2_scr, hist_g, hist_v, acc_scr, st_g, st_v) = refs
    else:
        (x1_ref, halo_ref, nw_ref, wg_ref, wv_ref, cwg_ref, cwv_ref, wd_ref, fnw_ref,
         y_ref, upg_ref, upv_ref, h2_scr, hist_g, hist_v, acc_scr) = refs
    i = pl.program_id(0)
    f = pl.program_id(1)
    pad = SUBLANES

    @pl.when(f == 0)
    def _():
        if with_state:
            h2_scr[pl.ds(0, pad), :] = jnp.zeros((pad, D_MODEL), BF16)
        else:
            live = (i % seq_tiles != 0).astype(F32)
            h2_scr[pl.ds(0, pad), :] = (_rms(halo_ref[...], nw_ref[...]) * live).astype(BF16)
        h2_scr[pl.ds(pad, tm), :] = _rms(x1_ref[...], nw_ref[...]).astype(BF16)
        acc_scr[...] = jnp.zeros_like(acc_scr)

    h2 = h2_scr[...]
    hist_g[...] = jnp.dot(h2, wg_ref[...], preferred_element_type=F32)
    hist_v[...] = jnp.dot(h2, wv_ref[...], preferred_element_type=F32)

    if with_state:
        upg_ref[...] = hist_g[pl.ds(pad, tm), :]
        upv_ref[...] = hist_v[pl.ds(pad, tm), :]
        st_g[pl.ds(0, tm), :] = stg_ref[...]
        st_v[pl.ds(0, tm), :] = stv_ref[...]
        st_g[pl.ds(tm, pad), :] = jnp.zeros((pad, tf), F32)
        st_v[pl.ds(tm, pad), :] = jnp.zeros((pad, tf), F32)
        rmod = lax.broadcasted_iota(jnp.int32, (tm, 1), 0) % SUBLANES
    else:
        upg_ref[...] = hist_g[pl.ds(tm, pad), :]
        upv_ref[...] = hist_v[pl.ds(tm, pad), :]

    def conv(hist, cw_ref, st):
        prev2 = hist[pl.ds(pad - 2, tm), :]
        prev1 = hist[pl.ds(pad - 1, tm), :]
        if with_state:
            prev2 = jnp.where(rmod < 2, st[pl.ds(0, tm), :], prev2)
            prev1 = jnp.where(rmod < 1, st[pl.ds(1, tm), :], prev1)
        return (prev2 * cw_ref[0:1, :] + prev1 * cw_ref[1:2, :]) + hist[pl.ds(pad, tm), :] * cw_ref[2:3, :]

    ug = conv(hist_g, cwg_ref, st_g if with_state else None)
    uv = conv(hist_v, cwv_ref, st_v if with_state else None)
    act = (ug * jax.nn.sigmoid(ug) * uv).astype(BF16)
    acc_scr[...] += jnp.dot(act, wd_ref[...], preferred_element_type=F32)

    @pl.when(f == pl.num_programs(1) - 1)
    def _():
        y_ref[...] = _rms(x1_ref[...] + acc_scr[...], fnw_ref[...])


def _ffn(x1, nw, wup, cw, wd, fnw, state=None, *, tm, tf, seq_tiles):
    m = x1.shape[0]
    nf = D_FF // tf
    with_state = state is not None
    tr = tm if with_state else SUBLANES
    kern = functools.partial(_ffn_kernel, tm=tm, tf=tf, seq_tiles=seq_tiles, with_state=with_state)
    in_specs = [pl.BlockSpec((tm, D_MODEL), lambda i, f: (i, 0))]
    args = [x1]
    if not with_state:
        in_specs.append(pl.BlockSpec((SUBLANES, D_MODEL),
                                     lambda i, f: (jnp.maximum(i * (tm // SUBLANES) - 1, 0), 0)))
        args.append(x1)
    in_specs += [
        pl.BlockSpec((1, D_MODEL), lambda i, f: (0, 0)),
        pl.BlockSpec((D_MODEL, tf), lambda i, f: (0, f)),
        pl.BlockSpec((D_MODEL, tf), lambda i, f: (0, f + nf)),
        pl.BlockSpec((FFN_CONV, tf), lambda i, f: (0, f)),
        pl.BlockSpec((FFN_CONV, tf), lambda i, f: (0, f + nf)),
        pl.BlockSpec((tf, D_MODEL), lambda i, f: (f, 0)),
        pl.BlockSpec((1, D_MODEL), lambda i, f: (0, 0)),
    ]
    args += [nw, wup, wup, cw, cw, wd, fnw]
    scratch = [pltpu.VMEM((SUBLANES + tm, D_MODEL), BF16),
               pltpu.VMEM((SUBLANES + tm, tf), F32),
               pltpu.VMEM((SUBLANES + tm, tf), F32),
               pltpu.VMEM((tm, D_MODEL), F32)]
    if with_state:
        in_specs += [pl.BlockSpec((tm, tf), lambda i, f: (i, f)),
                     pl.BlockSpec((tm, tf), lambda i, f: (i, f + nf))]
        args += [state, state]
        scratch += [pltpu.VMEM((tm + SUBLANES, tf), F32), pltpu.VMEM((tm + SUBLANES, tf), F32)]
    n_tiles = m // tm
    return pl.pallas_call(
        kern,
        grid=(n_tiles, nf),
        in_specs=in_specs,
        out_specs=[
            pl.BlockSpec((tm, D_MODEL), lambda i, f: (i, 0)),
            pl.BlockSpec((tr, tf), lambda i, f: (i, f)),
            pl.BlockSpec((tr, tf), lambda i, f: (i, f)),
        ],
        out_shape=[jax.ShapeDtypeStruct((m, D_MODEL), F32),
                   jax.ShapeDtypeStruct((n_tiles * tr, D_FF), F32),
                   jax.ShapeDtypeStruct((n_tiles * tr, D_FF), F32)],
        scratch_shapes=scratch,
        compiler_params=_params("parallel", "arbitrary"),
        name="ffn",
    )(*args)


def _layer(x, nb, nc, nseq, rows, valid, gsz, states, w, *, tm_in, tm_merge, tm_ffn, seq_tiles, ffn_state):
    conv_state, s0, c0, n0, m0 = states
    big, gates = _inproj(x, w["norm_mix"], w["w_big"], w["w_small"], w["gate_bias"], w["a_log"],
                         tm=tm_in, tn=512)
    go, s_new = _gdn(big, gates, conv_state, s0, w["gdn_conv_w"], w["gdn_norm"],
                     nb=nb, nc=nc, nseq=nseq, rows=rows, valid=valid, gsz=gsz)
    mh, c_new, n_new, m_new = _mlstm(big, gates, c0, n0, m0,
                                     nb=nb, nc=nc, nseq=nseq, rows=rows, valid=valid, gsz=gsz)
    x1 = _merge(go, mh, big, x, w["w_a"], w["w_b"], w["w_out"], tm=tm_merge, tn=512)
    y, upg, upv = _ffn(x1, w["norm_ffn"], w["w_up"], w["ffn_conv_w"], w["w_down"], w["norm_final"],
                       ffn_state, tm=tm_ffn, tf=512, seq_tiles=seq_tiles)
    return y, big, (s_new, c_new, n_new, m_new.reshape(nb, HEADS)), (upg, upv)


def kernel(x_prompt, x_sample, state_gdn_conv, state_gdn_S, state_ml_C, state_ml_n, state_ml_m, state_ffn_conv,
           norm_mix_w, w_in, gdn_conv_w, gdn_A_log, gdn_dt_bias, gdn_norm_w, w_branch_a, ml_b_i, ml_b_f,
           w_branch_b, w_out, norm_ffn_w, w_up, ffn_conv_w, w_down, norm_final_w):
    assert w_in.shape[0] == 1, "single-layer step"
    bp, tp, _ = x_prompt.shape
    bs, ts, _ = x_sample.shape
    assert ts <= SUBLANES and tp % CHUNK == 0

    wi = w_in[0]
    o = 0
    parts = {}
    for name, size in (("gqkv", QKV), ("gz", QK), ("ga", HEADS), ("gb", HEADS), ("mqkv", QKV),
                       ("mi", HEADS), ("mf", HEADS), ("mo", QK), ("gA", D_MODEL), ("gB", D_MODEL)):
        parts[name] = wi[:, o:o + size]
        o += size
    zeros8 = jnp.zeros((HEADS,), F32)
    w = {
        "norm_mix": norm_mix_w[0][None, :],
        "w_big": jnp.concatenate([parts[n] for n in ("gqkv", "mqkv", "gz", "mo", "gA", "gB")], axis=1).astype(BF16),
        "w_small": jnp.concatenate([parts[n] for n in ("ga", "gb", "mi", "mf")], axis=1).astype(BF16),
        "gate_bias": jnp.concatenate([gdn_dt_bias[0], zeros8, ml_b_i[0], ml_b_f[0]])[None, :],
        "a_log": jnp.concatenate([gdn_A_log[0], zeros8, zeros8, zeros8])[None, :],
        "gdn_conv_w": gdn_conv_w[0],
        "gdn_norm": gdn_norm_w[0][None, :],
        "w_a": w_branch_a[0].astype(BF16),
        "w_b": w_branch_b[0].astype(BF16),
        "w_out": w_out[0].astype(BF16),
        "norm_ffn": norm_ffn_w[0][None, :],
        "w_up": w_up[0].astype(BF16),
        "ffn_conv_w": ffn_conv_w[0],
        "w_down": w_down[0].astype(BF16),
        "norm_final": norm_final_w[None, :],
    }

    xp = x_prompt.reshape(bp * tp, D_MODEL)
    p_states = (jnp.zeros((bp, GDN_CONV - 1, QKV), F32), jnp.zeros((bp, HEADS, DH, DH), F32),
                jnp.zeros((bp, HEADS, DH, DH), F32), jnp.zeros((bp, HEADS, DH), F32),
                jnp.zeros((bp, 1, HEADS), F32))
    tm_ffn = 512
    yp, big_p, (ps, pc, pn, pm), up_p = _layer(
        xp, bp, tp // CHUNK, 1, CHUNK, CHUNK, 4, p_states, w,
        tm_in=1024, tm_merge=1024, tm_ffn=tm_ffn, seq_tiles=tp // tm_ffn, ffn_state=None)
    y_prompt = yp.reshape(bp, tp, D_MODEL)
    p_gdn_conv = big_p.reshape(bp, tp, BIG_COLS)[:, tp - (GDN_CONV - 1):, :QKV]
    p_ffn_conv = jnp.concatenate(
        [u.reshape(bp, tp // tm_ffn, SUBLANES, D_FF)[:, -1, SUBLANES - (FFN_CONV - 1):, :] for u in up_p], axis=-1)

    padr = SUBLANES - ts
    xs = jnp.pad(x_sample, ((0, 0), (0, padr), (0, 0))).reshape(bs * SUBLANES, D_MODEL)
    st_ffn = jnp.pad(state_ffn_conv[0], ((0, 0), (0, SUBLANES - (FFN_CONV - 1)), (0, 0)))
    st_ffn = st_ffn.reshape(bs * SUBLANES, 2 * D_FF)
    s_states = (state_gdn_conv[0], state_gdn_S[0], state_ml_C[0], state_ml_n[0],
                state_ml_m[0].reshape(bs, 1, HEADS))
    ys, big_s, (ss, sc, sn, sm), up_s = _layer(
        xs, bs, 1, 4, SUBLANES, ts, HEADS, s_states, w,
        tm_in=bs * SUBLANES, tm_merge=bs * SUBLANES, tm_ffn=512, seq_tiles=1, ffn_state=st_ffn)
    y_sample = ys.reshape(bs, SUBLANES, D_MODEL)[:, :ts, :]
    xq = jnp.concatenate([state_gdn_conv[0], big_s.reshape(bs, SUBLANES, BIG_COLS)[:, :ts, :QKV]], axis=1)
    s_gdn_conv = xq[:, ts:, :]
    up_rows = jnp.concatenate([u.reshape(bs, SUBLANES, D_FF)[:, :ts, :] for u in up_s], axis=-1)
    s_ffn_conv = jnp.concatenate([state_ffn_conv[0], up_rows], axis=1)[:, ts:, :]

    return (y_prompt, y_sample,
            p_gdn_conv[None], ps[None], pc[None], pn[None], pm[None], p_ffn_conv[None],
            s_gdn_conv[None], ss[None], sc[None], sn[None], sm[None], s_ffn_conv[None])
```

```python
import functools
import math

import jax
import jax.numpy as jnp
from jax import lax
from jax.experimental import pallas as pl
from jax.experimental.pallas import tpu as pltpu

F32 = jnp.float32
BF16 = jnp.bfloat16

D_MODEL = 2048
HEADS = 8
DH = 128
QK = HEADS * DH
QKV = 3 * QK
D_FF = 5632
GDN_CONV = 4
FFN_CONV = 3
EPS = 1e-6
CHUNK = 64
N_GATES = 4 * HEADS
SUBLANES = 8
NEG = -1e30

BIG_COLS = 2 * QKV + 2 * QK + 2 * D_MODEL

VMEM_LIMIT = 56 * 1024 * 1024


def _params(*sem):
    return pltpu.CompilerParams(dimension_semantics=sem, vmem_limit_bytes=VMEM_LIMIT)


def _dot(a, b):
    return jnp.dot(a.astype(BF16), b.astype(BF16), preferred_element_type=F32)


def _dot_nt(a, b):
    return lax.dot_general(a.astype(BF16), b.astype(BF16), (((1,), (1,)), ((), ())),
                           preferred_element_type=F32)


def _dot_tn(a, b):
    return lax.dot_general(a.astype(BF16), b.astype(BF16), (((0,), (0,)), ((), ())),
                           preferred_element_type=F32)


def _softplus(x):
    return jnp.maximum(x, 0.0) + jnp.log1p(jnp.exp(-jnp.abs(x)))


def _rms(x, w):
    return x * lax.rsqrt(jnp.mean(x * x, axis=-1, keepdims=True) + EPS) * w


IN_SECTIONS = (("gqkv", QKV), ("gz", QK), ("ga", HEADS), ("gb", HEADS), ("mqkv", QKV),
               ("mi", HEADS), ("mf", HEADS), ("mo", QK), ("gA", D_MODEL), ("gB", D_MODEL))
BIG_ORDER = ("gqkv", "mqkv", "gz", "mo", "gA", "gB")
IN_COLS = sum(size for _, size in IN_SECTIONS)


def _in_start(name):
    off = 0
    for n, size in IN_SECTIONS:
        if n == name:
            return off
        off += size
    raise KeyError(name)


def _big_tiles(tn):
    starts = []
    for name in BIG_ORDER:
        size = dict(IN_SECTIONS)[name]
        assert size % tn == 0
        starts += [_in_start(name) + t * tn for t in range(size // tn)]
    return starts


def _tile_lookup(j, values):
    out = jnp.int32(values[-1])
    for t in range(len(values) - 2, -1, -1):
        out = jnp.where(j <= t, jnp.int32(values[t]), out)
    return out


LANES = 128
SMALL_ORDER = ("ga", "gb", "mi", "mf")


def _repack_kernel(*refs):
    wt_ref, g_refs, out_ref, small_ref = refs[0], refs[1:1 + len(SMALL_ORDER)], refs[-2], refs[-1]

    @pl.when(pl.program_id(0) == 0)
    def _():
        cols = [g_ref[...].T[:, _in_start(n) % SUBLANES:_in_start(n) % SUBLANES + HEADS]
                for n, g_ref in zip(SMALL_ORDER, g_refs)]
        small_ref[...] = jnp.concatenate(cols, axis=1).astype(BF16)

    out_ref[...] = wt_ref[...].T.astype(BF16)


def _repack(w_in_t, *, tn):
    starts = _big_tiles(tn)
    assert all(s % SUBLANES == 0 for s in starts)
    rows8 = [s // SUBLANES for s in starts]
    return pl.pallas_call(
        _repack_kernel,
        grid=(BIG_COLS // tn,),
        in_specs=[pl.BlockSpec((pl.Element(tn), pl.Element(D_MODEL)),
                               lambda j: (_tile_lookup(j, rows8) * SUBLANES, 0))]
        + [pl.BlockSpec((SUBLANES, D_MODEL), lambda j, b=_in_start(n) // SUBLANES: (b, 0)) for n in SMALL_ORDER],
        out_specs=[pl.BlockSpec((D_MODEL, tn), lambda j: (0, j)),
                   pl.BlockSpec((D_MODEL, N_GATES), lambda j: (0, 0))],
        out_shape=[jax.ShapeDtypeStruct((D_MODEL, BIG_COLS), BF16),
                   jax.ShapeDtypeStruct((D_MODEL, N_GATES), BF16)],
        compiler_params=_params("arbitrary"),
        name="repack",
    )(*([w_in_t] * (1 + len(SMALL_ORDER))))


def _inproj_kernel(x_ref, nw_ref, wbig_ref, wsm_ref, gbias_ref, alog_ref, big_ref, gates_ref, h_scr):
    @pl.when(pl.program_id(1) == 0)
    def _():
        hb = _rms(x_ref[...], nw_ref[...]).astype(BF16)
        h_scr[...] = hb
        raw = jnp.dot(hb, wsm_ref[...], preferred_element_type=F32)
        lane = lax.broadcasted_iota(jnp.int32, raw.shape, 1)
        z = raw + gbias_ref[...]
        g = -jnp.exp(alog_ref[...]) * _softplus(z)
        beta = jax.nn.sigmoid(raw)
        lf = -_softplus(-z)
        gates_ref[...] = jnp.where(lane < HEADS, g,
                                   jnp.where(lane < 2 * HEADS, beta,
                                             jnp.where(lane < 3 * HEADS, z, lf)))

    big_ref[...] = jnp.dot(h_scr[...], wbig_ref[...], preferred_element_type=F32)


def _inproj(x, nw, wbig, wsm, gbias, alog, *, tm, tn):
    m = x.shape[0]
    return pl.pallas_call(
        _inproj_kernel,
        grid=(m // tm, BIG_COLS // tn),
        in_specs=[
            pl.BlockSpec((tm, D_MODEL), lambda i, j: (i, 0)),
            pl.BlockSpec((1, D_MODEL), lambda i, j: (0, 0)),
            pl.BlockSpec((D_MODEL, tn), lambda i, j: (0, j)),
            pl.BlockSpec((D_MODEL, N_GATES), lambda i, j: (0, 0)),
            pl.BlockSpec((1, N_GATES), lambda i, j: (0, 0)),
            pl.BlockSpec((1, N_GATES), lambda i, j: (0, 0)),
        ],
        out_specs=[
            pl.BlockSpec((tm, tn), lambda i, j: (i, j)),
            pl.BlockSpec((tm, N_GATES), lambda i, j: (i, 0)),
        ],
        out_shape=[jax.ShapeDtypeStruct((m, BIG_COLS), F32),
                   jax.ShapeDtypeStruct((m, N_GATES), F32)],
        scratch_shapes=[pltpu.VMEM((tm, D_MODEL), BF16)],
        compiler_params=_params("parallel", "arbitrary"),
        name="inproj",
    )(x, nw, wbig, wsm, gbias, alog)


def _group_masks(n, rows):
    r = n * rows
    shift = rows.bit_length() - 1
    ri = lax.broadcasted_iota(jnp.int32, (r, r), 0)
    ci = lax.broadcasted_iota(jnp.int32, (r, r), 1)
    same = lax.shift_right_logical(ri, shift) == lax.shift_right_logical(ci, shift)
    return ri == ci, same & (ci <= ri), same & (ci < ri)


def _row_block(nrows, rows, n):
    shift = rows.bit_length() - 1
    ri = lax.broadcasted_iota(jnp.int32, (nrows, 1), 0)
    return lax.shift_right_logical(ri, shift) & (n - 1)


def _row_of(col, eye):
    return jnp.sum(jnp.where(eye, col, 0.0), axis=0, keepdims=True)


def _cumsum_col(col, eye, incl):
    return jnp.sum(jnp.where(incl, _row_of(col, eye), 0.0), axis=1, keepdims=True)


def _per_row(vals, rows):
    return jnp.concatenate([jnp.broadcast_to(v, (rows, 1)) for v in vals], axis=0)


def _block_diag(x, rblk, n):
    return jnp.concatenate([jnp.where(rblk == c, x, 0.0) for c in range(n)], axis=1)


def _conv_silu(hist_ref, s, cw_ref, col0, rows):
    cols = slice(col0, col0 + DH)
    base = SUBLANES - GDN_CONV + 1
    acc = hist_ref[s, pl.ds(base, rows), cols] * cw_ref[0:1, cols]
    for j in range(1, GDN_CONV):
        acc = acc + hist_ref[s, pl.ds(base + j, rows), cols] * cw_ref[j:j + 1, cols]
    return acc * jax.nn.sigmoid(acc)


def _seq_loader(ref, scr, nseq, valid, rows):
    if valid == rows:
        return lambda s, cols: ref[pl.ds(s * rows, rows), cols]
    for s in range(nseq):
        scr[s, pl.ds(0, valid), :] = ref[pl.ds(s * valid, valid), :]
        scr[s, pl.ds(valid, rows - valid), :] = jnp.zeros((rows - valid, scr.shape[-1]), F32)
    return lambda s, cols: scr[s, :, cols]


def _problems(nseq, gsz):
    probs = [(s, h) for s in range(nseq) for h in range(HEADS)]
    return [probs[i:i + gsz] for i in range(0, len(probs), gsz)]


def _gdn_kernel(qkv_ref, z_ref, gates_ref, cst_ref, s0_ref, cw_ref, nw_ref, o_ref, s_ref, cst_out_ref, hist, *pads,
                nseq, rows, valid, levels, gsz):
    c = pl.program_id(1)
    keep = GDN_CONV - 1

    @pl.when(c == 0)
    def _():
        hist[:, pl.ds(SUBLANES - keep, keep), :] = cst_ref[...]
        s_ref[...] = s0_ref[...]

    for s in range(nseq):
        hist[s, pl.ds(SUBLANES, valid), :] = qkv_ref[pl.ds(s * valid, valid), :]
        if valid < rows:
            hist[s, pl.ds(SUBLANES + valid, rows - valid), :] = jnp.zeros((rows - valid, QKV), F32)
    zpad, gpad = pads if pads else (None, None)
    load_z = _seq_loader(z_ref, zpad, nseq, valid, rows)
    load_g = _seq_loader(gates_ref, gpad, nseq, valid, rows)

    r = gsz * rows
    eye, incl, strict = _group_masks(gsz, rows)
    eye_f = eye.astype(F32)
    rvalid = (lax.broadcasted_iota(jnp.int32, (r, 1), 0) & (rows - 1)) < valid
    rblk = _row_block(r, rows, gsz)
    rblk2 = _row_block(2 * r, rows, gsz)

    for group in _problems(nseq, gsz):
        def stack(fn):
            return jnp.concatenate([fn(s, h) for s, h in group], axis=0)

        q = stack(lambda s, h: _conv_silu(hist, s, cw_ref, h * DH, rows))
        k = stack(lambda s, h: _conv_silu(hist, s, cw_ref, QK + h * DH, rows))
        v = stack(lambda s, h: _conv_silu(hist, s, cw_ref, 2 * QK + h * DH, rows))
        q = q * lax.rsqrt(jnp.sum(q * q, axis=-1, keepdims=True) + EPS) * (DH ** -0.5)
        k = k * lax.rsqrt(jnp.sum(k * k, axis=-1, keepdims=True) + EPS)
        g = jnp.where(rvalid, stack(lambda s, h: load_g(s, slice(h, h + 1))), 0.0)
        beta = jnp.where(rvalid, stack(lambda s, h: load_g(s, slice(HEADS + h, HEADS + h + 1))), 0.0)

        gc = _cumsum_col(g, eye, incl)
        gr = _row_of(gc, eye)
        decay = jnp.where(incl, jnp.exp(jnp.where(incl, gc - gr, 0.0)), 0.0)
        kq = _dot_nt(jnp.concatenate([k, q], axis=0), k)
        a = jnp.where(strict, beta * kq[:r] * decay, 0.0)
        qk = kq[r:] * decay
        bk = -a
        t = eye_f + bk
        if levels >= 2:
            bk = _dot(bk, bk)
            for _ in range(2, levels):
                tb = _dot(jnp.concatenate([t, bk], axis=0), bk)
                t = t + tb[:r]
                bk = tb[r:]
            t = t + _dot(t, bk)
        eg = jnp.exp(gc)
        uw = _dot(t, jnp.concatenate([v * beta, k * (beta * eg)], axis=1))
        u, w = uw[:, :DH], uw[:, DH:]

        s_old = [s_ref[s, h] for s, h in group]
        ws = _dot(_block_diag(jnp.concatenate([w, q * eg], axis=0), rblk2, gsz),
                  jnp.concatenate(s_old, axis=0))
        v_new = u - ws[:r]
        o = ws[r:] + _dot(qk, v_new)
        g_last = [gc[(i + 1) * rows - 1:(i + 1) * rows, :] for i in range(gsz)]
        kd = k * jnp.exp(_per_row(g_last, rows) - gc)
        s_upd = _dot_tn(_block_diag(kd, rblk, gsz), v_new)
        for i, (s, h) in enumerate(group):
            s_ref[s, h] = s_old[i] * jnp.exp(g_last[i]) + s_upd[i * DH:(i + 1) * DH]

        zs = stack(lambda s, h: load_z(s, slice(h * DH, (h + 1) * DH)))
        on = _rms(o, nw_ref[...]) * (zs * jax.nn.sigmoid(zs))
        for i, (s, h) in enumerate(group):
            o_ref[pl.ds(s * valid, valid), h * DH:(h + 1) * DH] = on[i * rows:i * rows + valid]

    for s in range(nseq):
        hist[s, pl.ds(SUBLANES - keep, keep), :] = hist[s, pl.ds(SUBLANES + valid - keep, keep), :]
    cst_out_ref[...] = hist[:, pl.ds(SUBLANES - keep, keep), :]


def _gdn(big, gates, conv_state, s0, conv_w, norm_w, *, nb, nc, nseq, rows, valid, gsz):
    levels = max(1, math.ceil(math.log2(valid)))
    kern = functools.partial(_gdn_kernel, nseq=nseq, rows=rows, valid=valid, levels=levels, gsz=gsz)
    br = nseq * valid
    scratch = [pltpu.VMEM((nseq, SUBLANES + rows, QKV), F32)]
    if valid < rows:
        scratch += [pltpu.VMEM((nseq, rows, QK), F32), pltpu.VMEM((nseq, rows, N_GATES), F32)]
    return pl.pallas_call(
        kern,
        grid=(nb // nseq, nc),
        in_specs=[
            pl.BlockSpec((br, QKV), lambda b, c: (b * nc + c, 0)),
            pl.BlockSpec((br, QK), lambda b, c: (b * nc + c, 2 * QKV // QK)),
            pl.BlockSpec((br, N_GATES), lambda b, c: (b * nc + c, 0)),
            pl.BlockSpec((nseq, GDN_CONV - 1, QKV), lambda b, c: (b, 0, 0)),
            pl.BlockSpec((nseq, HEADS, DH, DH), lambda b, c: (b, 0, 0, 0)),
            pl.BlockSpec((GDN_CONV, QKV), lambda b, c: (0, 0)),
            pl.BlockSpec((1, DH), lambda b, c: (0, 0)),
        ],
        out_specs=[
            pl.BlockSpec((br, QK), lambda b, c: (b * nc + c, 0)),
            pl.BlockSpec((nseq, HEADS, DH, DH), lambda b, c: (b, 0, 0, 0)),
            pl.BlockSpec((nseq, GDN_CONV - 1, QKV), lambda b, c: (b, 0, 0)),
        ],
        out_shape=[jax.ShapeDtypeStruct((nb * nc * valid, QK), F32),
                   jax.ShapeDtypeStruct((nb, HEADS, DH, DH), F32),
                   jax.ShapeDtypeStruct((nb, GDN_CONV - 1, QKV), F32)],
        scratch_shapes=scratch,
        compiler_params=_params("parallel", "arbitrary"),
        name="gdn",
    )(big, big, gates, conv_state, s0, conv_w, norm_w)


def _mlstm_kernel(qkv_ref, og_ref, gates_ref, c0_ref, n0_ref, m0_ref, h_ref, c_ref, n_ref, m_ref, *pads,
                  nseq, rows, valid, gsz):
    @pl.when(pl.program_id(1) == 0)
    def _():
        c_ref[...] = c0_ref[...]
        n_ref[...] = n0_ref[...]
        m_ref[...] = m0_ref[...]

    r = gsz * rows
    eye, incl, _ = _group_masks(gsz, rows)
    rvalid = (lax.broadcasted_iota(jnp.int32, (r, 1), 0) & (rows - 1)) < valid
    rblk = _row_block(r, rows, gsz)
    xpad, opad, gpad = pads if pads else (None, None, None)
    load_x = _seq_loader(qkv_ref, xpad, nseq, valid, rows)
    load_o = _seq_loader(og_ref, opad, nseq, valid, rows)
    load_g = _seq_loader(gates_ref, gpad, nseq, valid, rows)
    m_all = m_ref[...]

    for group in _problems(nseq, gsz):
        def stack(fn):
            return jnp.concatenate([fn(s, h) for s, h in group], axis=0)

        q = stack(lambda s, h: load_x(s, slice(h * DH, (h + 1) * DH)))
        k = stack(lambda s, h: load_x(s, slice(QK + h * DH, QK + (h + 1) * DH))) * (DH ** -0.5)
        v = stack(lambda s, h: load_x(s, slice(2 * QK + h * DH, 2 * QK + (h + 1) * DH)))
        ig = jnp.where(rvalid, stack(lambda s, h: load_g(s, slice(2 * HEADS + h, 2 * HEADS + h + 1))), NEG)
        lf = jnp.where(rvalid, stack(lambda s, h: load_g(s, slice(3 * HEADS + h, 3 * HEADS + h + 1))), 0.0)

        bc = _cumsum_col(lf, eye, incl)
        br = _row_of(bc, eye)
        igr = _row_of(ig, eye)
        d_log = jnp.where(incl, bc - br + igr, NEG)
        d_max = jnp.max(d_log, axis=1, keepdims=True)
        b_last = [bc[(i + 1) * rows - 1:(i + 1) * rows, :] for i in range(gsz)]
        e_log = _per_row(b_last, rows) - bc + ig
        e_max = [jnp.max(e_log[i * rows:(i + 1) * rows], axis=0, keepdims=True) for i in range(gsz)]
        qk = _dot_nt(q, k)

        c_old = [c_ref[s, h] for s, h in group]
        n_old = [n_ref[s, h:h + 1, :] for s, h in group]
        m_old = [m_all[s, :, h:h + 1] for s, h in group]
        inter = bc + _per_row(m_old, rows)
        mt = jnp.maximum(inter, d_max)
        wi = jnp.exp(inter - mt)
        p = jnp.where(incl, jnp.exp(d_log - mt), 0.0) * qk
        qc = _dot_nt(_block_diag(q, rblk, gsz), jnp.concatenate(c_old, axis=1))
        num = wi * qc + _dot(p, v)
        n_rows = jnp.concatenate([jnp.broadcast_to(nv, (rows, DH)) for nv in n_old], axis=0)
        den = wi * jnp.sum(q * n_rows, axis=-1, keepdims=True) + jnp.sum(p, axis=-1, keepdims=True)
        hv = num / jnp.maximum(jnp.abs(den), jnp.exp(-mt))

        m_new = [jnp.maximum(b_last[i] + m_old[i], e_max[i]) for i in range(gsz)]
        fw = [jnp.exp(b_last[i] + m_old[i] - m_new[i]) for i in range(gsz)]
        sw = jnp.exp(e_log - _per_row(m_new, rows))
        c_upd = _dot_tn(sw * v, _block_diag(k, rblk, gsz))
        swk = sw * k
        for i, (s, h) in enumerate(group):
            c_ref[s, h] = fw[i] * c_old[i] + c_upd[:, i * DH:(i + 1) * DH]
            n_ref[s, h:h + 1, :] = fw[i] * n_old[i] + jnp.sum(swk[i * rows:(i + 1) * rows], axis=0, keepdims=True)
            m_ref[s, :, h:h + 1] = m_new[i]

        og = stack(lambda s, h: load_o(s, slice(h * DH, (h + 1) * DH)))
        hg = hv * jax.nn.sigmoid(og)
        for i, (s, h) in enumerate(group):
            h_ref[pl.ds(s * valid, valid), h * DH:(h + 1) * DH] = hg[i * rows:i * rows + valid]


def _mlstm(big, gates, c0, n0, m0, *, nb, nc, nseq, rows, valid, gsz):
    kern = functools.partial(_mlstm_kernel, nseq=nseq, rows=rows, valid=valid, gsz=gsz)
    br = nseq * valid
    scratch = []
    if valid < rows:
        scratch = [pltpu.VMEM((nseq, rows, QKV), F32), pltpu.VMEM((nseq, rows, QK), F32),
                   pltpu.VMEM((nseq, rows, N_GATES), F32)]
    return pl.pallas_call(
        kern,
        grid=(nb // nseq, nc),
        in_specs=[
            pl.BlockSpec((br, QKV), lambda b, c: (b * nc + c, 1)),
            pl.BlockSpec((br, QK), lambda b, c: (b * nc + c, 2 * QKV // QK + 1)),
            pl.BlockSpec((br, N_GATES), lambda b, c: (b * nc + c, 0)),
            pl.BlockSpec((nseq, HEADS, DH, DH), lambda b, c: (b, 0, 0, 0)),
            pl.BlockSpec((nseq, HEADS, DH), lambda b, c: (b, 0, 0)),
            pl.BlockSpec((nseq, 1, HEADS), lambda b, c: (b, 0, 0)),
        ],
        out_specs=[
            pl.BlockSpec((br, QK), lambda b, c: (b * nc + c, 0)),
            pl.BlockSpec((nseq, HEADS, DH, DH), lambda b, c: (b, 0, 0, 0)),
            pl.BlockSpec((nseq, HEADS, DH), lambda b, c: (b, 0, 0)),
            pl.BlockSpec((nseq, 1, HEADS), lambda b, c: (b, 0, 0)),
        ],
        out_shape=[jax.ShapeDtypeStruct((nb * nc * valid, QK), F32),
                   jax.ShapeDtypeStruct((nb, HEADS, DH, DH), F32),
                   jax.ShapeDtypeStruct((nb, HEADS, DH), F32),
                   jax.ShapeDtypeStruct((nb, 1, HEADS), F32)],
        scratch_shapes=scratch,
        compiler_params=_params("parallel", "arbitrary"),
        name="mlstm",
    )(big, big, gates, c0, n0, m0)


def _merge_kernel(go_ref, mh_ref, ga_ref, gb_ref, x_ref, wa_ref, wb_ref, wo_ref, x1_ref, mix_scr, *, nt, tn):
    j = pl.program_id(1)

    @pl.when(j < nt)
    def _():
        ya = jnp.dot(go_ref[...].astype(BF16), wa_ref[...], preferred_element_type=F32)
        yb = jnp.dot(mh_ref[...].astype(BF16), wb_ref[...], preferred_element_type=F32)
        mixed = jax.nn.sigmoid(ga_ref[...]) * ya + jax.nn.sigmoid(gb_ref[...]) * yb
        mixed = mixed.astype(BF16)
        for t in range(nt):
            @pl.when(j == t)
            def _():
                mix_scr[:, t * tn:(t + 1) * tn] = mixed

    @pl.when(j >= nt)
    def _():
        x1_ref[...] = x_ref[...] + jnp.dot(mix_scr[...], wo_ref[...], preferred_element_type=F32)


def _merge(go, mh, big, x, wa, wb, wo, *, tm, tn):
    m = x.shape[0]
    nt = D_MODEL // tn
    ga_blk = (2 * QKV + 2 * QK) // tn
    mix_j = lambda j: jnp.minimum(j, nt - 1)
    out_j = lambda j: jnp.maximum(j - nt, 0)
    return pl.pallas_call(
        functools.partial(_merge_kernel, nt=nt, tn=tn),
        grid=(m // tm, 2 * nt),
        in_specs=[
            pl.BlockSpec((tm, QK), lambda i, j: (i, 0)),
            pl.BlockSpec((tm, QK), lambda i, j: (i, 0)),
            pl.BlockSpec((tm, tn), lambda i, j: (i, ga_blk + mix_j(j))),
            pl.BlockSpec((tm, tn), lambda i, j: (i, ga_blk + nt + mix_j(j))),
            pl.BlockSpec((tm, tn), lambda i, j: (i, out_j(j))),
            pl.BlockSpec((QK, tn), lambda i, j: (0, mix_j(j))),
            pl.BlockSpec((QK, tn), lambda i, j: (0, mix_j(j))),
            pl.BlockSpec((D_MODEL, tn), lambda i, j: (0, out_j(j))),
        ],
        out_specs=pl.BlockSpec((tm, tn), lambda i, j: (i, out_j(j))),
        out_shape=jax.ShapeDtypeStruct((m, D_MODEL), F32),
        scratch_shapes=[pltpu.VMEM((tm, D_MODEL), BF16)],
        compiler_params=_params("parallel", "arbitrary"),
        name="merge",
    )(go, mh, big, big, x, wa, wb, wo)


def _ffn_kernel(*refs, tm, tf, seq_tiles, with_state, srows):
    keep = FFN_CONV - 1
    if with_state:
        (x1_ref, nw_ref, wg_ref, wv_ref, cwg_ref, cwv_ref, wd_ref, fnw_ref) = refs[:8]
        stg_refs, stv_refs = refs[8:8 + keep], refs[8 + keep:8 + 2 * keep]
        y_ref = refs[8 + 2 * keep]
        newg_refs, newv_refs = refs[9 + 2 * keep:9 + 3 * keep], refs[9 + 3 * keep:9 + 4 * keep]
        h2_scr, hist_g, hist_v, acc_scr, st_g, st_v, cp_g, cp_v = refs[9 + 4 * keep:]
        nsq = tm // srows
        nck = tf // LANES
    else:
        (x1_ref, halo_ref, nw_ref, wg_ref, wv_ref, cwg_ref, cwv_ref, wd_ref, fnw_ref,
         y_ref, upg_ref, upv_ref, h2_scr, hist_g, hist_v, acc_scr) = refs
    i = pl.program_id(0)
    f = pl.program_id(1)
    pad = SUBLANES

    @pl.when(f == 0)
    def _():
        if with_state:
            h2_scr[pl.ds(0, pad), :] = jnp.zeros((pad, D_MODEL), BF16)
        else:
            live = (i % seq_tiles != 0).astype(F32)
            h2_scr[pl.ds(0, pad), :] = (_rms(halo_ref[...], nw_ref[...]) * live).astype(BF16)
        h2_scr[pl.ds(pad, tm), :] = _rms(x1_ref[...], nw_ref[...]).astype(BF16)
        acc_scr[...] = jnp.zeros_like(acc_scr)

    h2 = h2_scr[...]
    hist_g[...] = jnp.dot(h2, wg_ref[...], preferred_element_type=F32)
    hist_v[...] = jnp.dot(h2, wv_ref[...], preferred_element_type=F32)

    if with_state:
        for st, cp, st_refs, hist, new_refs in ((st_g, cp_g, stg_refs, hist_g, newg_refs),
                                                (st_v, cp_v, stv_refs, hist_v, newv_refs)):
            st[...] = jnp.zeros_like(st)
            for c in range(nck):
                lanes = slice(c * LANES, (c + 1) * LANES)
                cp[c] = hist[:, lanes]
                for r in range(keep):
                    st[c, pl.ds(r, nsq, stride=srows), :] = st_refs[r][:, lanes]
                    new_refs[r][:, lanes] = cp[c, pl.ds(pad + srows - keep + r, nsq, stride=srows), :]
        rmod = lax.broadcasted_iota(jnp.int32, (tm, 1), 0) % srows
        slab = lambda st, off: jnp.concatenate([st[c, pl.ds(off, tm), :] for c in range(nck)], axis=1)
    else:
        upg_ref[...] = hist_g[pl.ds(tm, pad), :]
        upv_ref[...] = hist_v[pl.ds(tm, pad), :]

    def conv(hist, cw_ref, st):
        prev2 = hist[pl.ds(pad - 2, tm), :]
        prev1 = hist[pl.ds(pad - 1, tm), :]
        if with_state:
            prev2 = jnp.where(rmod < 2, slab(st, 0), prev2)
            prev1 = jnp.where(rmod < 1, slab(st, 1), prev1)
        return (prev2 * cw_ref[0:1, :] + prev1 * cw_ref[1:2, :]) + hist[pl.ds(pad, tm), :] * cw_ref[2:3, :]

    ug = conv(hist_g, cwg_ref, st_g if with_state else None)
    uv = conv(hist_v, cwv_ref, st_v if with_state else None)
    act = (ug * jax.nn.sigmoid(ug) * uv).astype(BF16)
    acc_scr[...] += jnp.dot(act, wd_ref[...], preferred_element_type=F32)

    @pl.when(f == pl.num_programs(1) - 1)
    def _():
        y_ref[...] = _rms(x1_ref[...] + acc_scr[...], fnw_ref[...])


def _ffn(x1, nw, wup, cw, wd, fnw, state=None, *, tm, tf, seq_tiles, srows=None):
    m = x1.shape[0]
    nf = D_FF // tf
    keep = FFN_CONV - 1
    with_state = state is not None
    kern = functools.partial(_ffn_kernel, tm=tm, tf=tf, seq_tiles=seq_tiles, with_state=with_state, srows=srows)
    in_specs = [pl.BlockSpec((tm, D_MODEL), lambda i, f: (i, 0))]
    args = [x1]
    if not with_state:
        in_specs.append(pl.BlockSpec((SUBLANES, D_MODEL),
                                     lambda i, f: (jnp.maximum(i * (tm // SUBLANES) - 1, 0), 0)))
        args.append(x1)
    in_specs += [
        pl.BlockSpec((1, D_MODEL), lambda i, f: (0, 0)),
        pl.BlockSpec((D_MODEL, tf), lambda i, f: (0, f)),
        pl.BlockSpec((D_MODEL, tf), lambda i, f: (0, f + nf)),
        pl.BlockSpec((FFN_CONV, tf), lambda i, f: (0, f)),
        pl.BlockSpec((FFN_CONV, tf), lambda i, f: (0, f + nf)),
        pl.BlockSpec((tf, D_MODEL), lambda i, f: (f, 0)),
        pl.BlockSpec((1, D_MODEL), lambda i, f: (0, 0)),
    ]
    args += [nw, wup, wup, cw, cw, wd, fnw]
    scratch = [pltpu.VMEM((SUBLANES + tm, D_MODEL), BF16),
               pltpu.VMEM((SUBLANES + tm, tf), F32),
               pltpu.VMEM((SUBLANES + tm, tf), F32),
               pltpu.VMEM((tm, D_MODEL), F32)]
    n_tiles = m // tm
    out_specs = [pl.BlockSpec((tm, D_MODEL), lambda i, f: (i, 0))]
    out_shape = [jax.ShapeDtypeStruct((m, D_MODEL), F32)]
    if with_state:
        nsq = tm // srows
        in_specs += [pl.BlockSpec((nsq, tf), lambda i, f: (i, f))] * keep
        in_specs += [pl.BlockSpec((nsq, tf), lambda i, f: (i, f + nf))] * keep
        args += list(state) * 2
        scratch += [pltpu.VMEM((tf // LANES, tm + SUBLANES, LANES), F32)] * 4
        out_specs += [pl.BlockSpec((nsq, tf), lambda i, f: (i, f))] * (2 * keep)
        out_shape += [jax.ShapeDtypeStruct((m // srows, D_FF), F32)] * (2 * keep)
    else:
        out_specs += [pl.BlockSpec((SUBLANES, tf), lambda i, f: (i, f))] * 2
        out_shape += [jax.ShapeDtypeStruct((n_tiles * SUBLANES, D_FF), F32)] * 2
    outs = pl.pallas_call(
        kern,
        grid=(n_tiles, nf),
        in_specs=in_specs,
        out_specs=out_specs,
        out_shape=out_shape,
        scratch_shapes=scratch,
        compiler_params=_params("parallel", "arbitrary"),
        name="ffn",
    )(*args)
    return outs[0], outs[1:]


def _layer(x, nb, nc, nseq, rows, valid, gsz, states, w, *, tm_in, tm_merge, tm_ffn, seq_tiles, ffn_state):
    conv_state, s0, c0, n0, m0 = states
    big, gates = _inproj(x, w["norm_mix"], w["w_big"], w["w_small"], w["gate_bias"], w["a_log"],
                         tm=tm_in, tn=512)
    go, s_new, conv_new = _gdn(big, gates, conv_state, s0, w["gdn_conv_w"], w["gdn_norm"],
                               nb=nb, nc=nc, nseq=nseq, rows=rows, valid=valid, gsz=gsz)
    mh, c_new, n_new, m_new = _mlstm(big, gates, c0, n0, m0,
                                     nb=nb, nc=nc, nseq=nseq, rows=rows, valid=valid, gsz=gsz)
    x1 = _merge(go, mh, big, x, w["w_a"], w["w_b"], w["w_out"], tm=tm_merge, tn=512)
    y, ffn_rows = _ffn(x1, w["norm_ffn"], w["w_up"], w["ffn_conv_w"], w["w_down"], w["norm_final"],
                       ffn_state, tm=tm_ffn, tf=512, seq_tiles=seq_tiles, srows=valid)
    return y, (conv_new, s_new, c_new, n_new, m_new.reshape(nb, HEADS)), ffn_rows


def kernel(x_prompt, x_sample, state_gdn_conv, state_gdn_S, state_ml_C, state_ml_n, state_ml_m, state_ffn_conv,
           norm_mix_w, w_in, gdn_conv_w, gdn_A_log, gdn_dt_bias, gdn_norm_w, w_branch_a, ml_b_i, ml_b_f,
           w_branch_b, w_out, norm_ffn_w, w_up, ffn_conv_w, w_down, norm_final_w):
    assert w_in.shape[0] == 1, "single-layer step"
    bp, tp, _ = x_prompt.shape
    bs, ts, _ = x_sample.shape
    assert FFN_CONV - 1 <= ts <= SUBLANES and GDN_CONV - 1 <= ts and tp % CHUNK == 0
    assert w_in.shape[1:] == (D_MODEL, IN_COLS)
    keep = FFN_CONV - 1

    w_big, w_small = _repack(w_in[0].T, tn=512)
    zeros8 = jnp.zeros((HEADS,), F32)
    w = {
        "norm_mix": norm_mix_w[0][None, :],
        "w_big": w_big,
        "w_small": w_small,
        "gate_bias": jnp.concatenate([gdn_dt_bias[0], zeros8, ml_b_i[0], ml_b_f[0]])[None, :],
        "a_log": jnp.concatenate([gdn_A_log[0], zeros8, zeros8, zeros8])[None, :],
        "gdn_conv_w": gdn_conv_w[0],
        "gdn_norm": gdn_norm_w[0][None, :],
        "w_a": w_branch_a[0].astype(BF16),
        "w_b": w_branch_b[0].astype(BF16),
        "w_out": w_out[0].astype(BF16),
        "norm_ffn": norm_ffn_w[0][None, :],
        "w_up": w_up[0].astype(BF16),
        "ffn_conv_w": ffn_conv_w[0],
        "w_down": w_down[0].astype(BF16),
        "norm_final": norm_final_w[None, :],
    }

    xp = x_prompt.reshape(bp * tp, D_MODEL)
    p_states = (jnp.zeros((bp, GDN_CONV - 1, QKV), F32), jnp.zeros((bp, HEADS, DH, DH), F32),
                jnp.zeros((bp, HEADS, DH, DH), F32), jnp.zeros((bp, HEADS, DH), F32),
                jnp.zeros((bp, 1, HEADS), F32))
    tm_ffn = 512
    yp, p_new, up_p = _layer(
        xp, bp, tp // CHUNK, 1, CHUNK, CHUNK, 4, p_states, w,
        tm_in=1024, tm_merge=1024, tm_ffn=tm_ffn, seq_tiles=tp // tm_ffn, ffn_state=None)
    p_ffn_conv = jnp.concatenate(
        [u.reshape(bp, tp // tm_ffn, SUBLANES, D_FF)[:, -1, SUBLANES - keep:, :] for u in up_p], axis=-1)

    xs = x_sample.reshape(bs * ts, D_MODEL)
    s_states = (state_gdn_conv[0], state_gdn_S[0], state_ml_C[0], state_ml_n[0],
                state_ml_m[0].reshape(bs, 1, HEADS))
    ys, s_new, new_rows = _layer(
        xs, bs, 1, 4, SUBLANES, ts, HEADS, s_states, w,
        tm_in=bs * ts, tm_merge=bs * ts, tm_ffn=bs * ts, seq_tiles=1,
        ffn_state=[state_ffn_conv[0, :, r, :] for r in range(keep)])
    s_ffn_conv = jnp.stack([jnp.concatenate([new_rows[r], new_rows[keep + r]], axis=-1) for r in range(keep)], axis=1)

    lead = lambda t: tuple(a[None] for a in t)
    return (yp.reshape(bp, tp, D_MODEL), ys.reshape(bs, ts, D_MODEL),
            *lead(p_new), p_ffn_conv[None], *lead(s_new), s_ffn_conv[None])
```

```python
import functools
import math

import jax
import jax.numpy as jnp
from jax import lax
from jax.experimental import pallas as pl
from jax.experimental.pallas import tpu as pltpu

F32 = jnp.float32
BF16 = jnp.bfloat16

D_MODEL = 2048
HEADS = 8
DH = 128
QK = HEADS * DH
QKV = 3 * QK
D_FF = 5632
GDN_CONV = 4
FFN_CONV = 3
EPS = 1e-6
CHUNK = 64
N_GATES = 4 * HEADS
SUBLANES = 8
NEG = -1e30

BIG_COLS = 2 * QKV + 2 * QK + 2 * D_MODEL

VMEM_LIMIT = 56 * 1024 * 1024


def _params(*sem):
    return pltpu.CompilerParams(dimension_semantics=sem, vmem_limit_bytes=VMEM_LIMIT)


def _dot(a, b):
    return jnp.dot(a.astype(BF16), b.astype(BF16), preferred_element_type=F32)


def _dot_nt(a, b):
    return lax.dot_general(a.astype(BF16), b.astype(BF16), (((1,), (1,)), ((), ())),
                           preferred_element_type=F32)


def _dot_tn(a, b):
    return lax.dot_general(a.astype(BF16), b.astype(BF16), (((0,), (0,)), ((), ())),
                           preferred_element_type=F32)


def _softplus(x):
    return jnp.maximum(x, 0.0) + jnp.log1p(jnp.exp(-jnp.abs(x)))


def _rms(x, w):
    return x * lax.rsqrt(jnp.mean(x * x, axis=-1, keepdims=True) + EPS) * w


IN_SECTIONS = (("gqkv", QKV), ("gz", QK), ("ga", HEADS), ("gb", HEADS), ("mqkv", QKV),
               ("mi", HEADS), ("mf", HEADS), ("mo", QK), ("gA", D_MODEL), ("gB", D_MODEL))
BIG_ORDER = ("gqkv", "mqkv", "gz", "mo", "gA", "gB")
IN_COLS = sum(size for _, size in IN_SECTIONS)


def _in_start(name):
    off = 0
    for n, size in IN_SECTIONS:
        if n == name:
            return off
        off += size
    raise KeyError(name)


def _big_tiles(tn):
    starts = []
    for name in BIG_ORDER:
        size = dict(IN_SECTIONS)[name]
        assert size % tn == 0
        starts += [_in_start(name) + t * tn for t in range(size // tn)]
    return starts


def _tile_lookup(j, values):
    out = jnp.int32(values[-1])
    for t in range(len(values) - 2, -1, -1):
        out = jnp.where(j <= t, jnp.int32(values[t]), out)
    return out


LANES = 128
SMALL_ORDER = ("ga", "gb", "mi", "mf")


def _repack_kernel(*refs):
    wt_ref, g_refs, out_ref, small_ref = refs[0], refs[1:1 + len(SMALL_ORDER)], refs[-2], refs[-1]

    @pl.when(pl.program_id(0) == 0)
    def _():
        cols = [g_ref[...].T[:, _in_start(n) % SUBLANES:_in_start(n) % SUBLANES + HEADS]
                for n, g_ref in zip(SMALL_ORDER, g_refs)]
        small_ref[...] = jnp.concatenate(cols, axis=1).astype(BF16)

    out_ref[...] = wt_ref[...].T.astype(BF16)


def _repack(w_in_t, *, tn):
    starts = _big_tiles(tn)
    assert all(s % SUBLANES == 0 for s in starts)
    rows8 = [s // SUBLANES for s in starts]
    return pl.pallas_call(
        _repack_kernel,
        grid=(BIG_COLS // tn,),
        in_specs=[pl.BlockSpec((pl.Element(tn), pl.Element(D_MODEL)),
                               lambda j: (_tile_lookup(j, rows8) * SUBLANES, 0))]
        + [pl.BlockSpec((SUBLANES, D_MODEL), lambda j, b=_in_start(n) // SUBLANES: (b, 0)) for n in SMALL_ORDER],
        out_specs=[pl.BlockSpec((D_MODEL, tn), lambda j: (0, j)),
                   pl.BlockSpec((D_MODEL, N_GATES), lambda j: (0, 0))],
        out_shape=[jax.ShapeDtypeStruct((D_MODEL, BIG_COLS), BF16),
                   jax.ShapeDtypeStruct((D_MODEL, N_GATES), BF16)],
        compiler_params=_params("arbitrary"),
        name="repack",
    )(*([w_in_t] * (1 + len(SMALL_ORDER))))


def _inproj_kernel(x_ref, nw_ref, wbig_ref, wsm_ref, gbias_ref, alog_ref, big_ref, gates_ref, h_scr):
    @pl.when(pl.program_id(1) == 0)
    def _():
        hb = _rms(x_ref[...], nw_ref[...]).astype(BF16)
        h_scr[...] = hb
        raw = jnp.dot(hb, wsm_ref[...], preferred_element_type=F32)
        lane = lax.broadcasted_iota(jnp.int32, raw.shape, 1)
        z = raw + gbias_ref[...]
        g = -jnp.exp(alog_ref[...]) * _softplus(z)
        beta = jax.nn.sigmoid(raw)
        lf = -_softplus(-z)
        gates_ref[...] = jnp.where(lane < HEADS, g,
                                   jnp.where(lane < 2 * HEADS, beta,
                                             jnp.where(lane < 3 * HEADS, z, lf)))

    big_ref[...] = jnp.dot(h_scr[...], wbig_ref[...], preferred_element_type=F32)


def _inproj(x, nw, wbig, wsm, gbias, alog, *, tm, tn):
    m = x.shape[0]
    return pl.pallas_call(
        _inproj_kernel,
        grid=(m // tm, BIG_COLS // tn),
        in_specs=[
            pl.BlockSpec((tm, D_MODEL), lambda i, j: (i, 0)),
            pl.BlockSpec((1, D_MODEL), lambda i, j: (0, 0)),
            pl.BlockSpec((D_MODEL, tn), lambda i, j: (0, j)),
            pl.BlockSpec((D_MODEL, N_GATES), lambda i, j: (0, 0)),
            pl.BlockSpec((1, N_GATES), lambda i, j: (0, 0)),
            pl.BlockSpec((1, N_GATES), lambda i, j: (0, 0)),
        ],
        out_specs=[
            pl.BlockSpec((tm, tn), lambda i, j: (i, j)),
            pl.BlockSpec((tm, N_GATES), lambda i, j: (i, 0)),
        ],
        out_shape=[jax.ShapeDtypeStruct((m, BIG_COLS), F32),
                   jax.ShapeDtypeStruct((m, N_GATES), F32)],
        scratch_shapes=[pltpu.VMEM((tm, D_MODEL), BF16)],
        compiler_params=_params("parallel", "arbitrary"),
        name="inproj",
    )(x, nw, wbig, wsm, gbias, alog)


def _group_masks(n, rows):
    r = n * rows
    shift = rows.bit_length() - 1
    ri = lax.broadcasted_iota(jnp.int32, (r, r), 0)
    ci = lax.broadcasted_iota(jnp.int32, (r, r), 1)
    same = lax.shift_right_logical(ri, shift) == lax.shift_right_logical(ci, shift)
    return ri == ci, same & (ci <= ri), same & (ci < ri)


def _row_block(nrows, rows, n):
    shift = rows.bit_length() - 1
    ri = lax.broadcasted_iota(jnp.int32, (nrows, 1), 0)
    return lax.shift_right_logical(ri, shift) & (n - 1)


def _row_of(col, eye):
    return jnp.sum(jnp.where(eye, col, 0.0), axis=0, keepdims=True)


def _cumsum_col(col, eye, incl):
    return jnp.sum(jnp.where(incl, _row_of(col, eye), 0.0), axis=1, keepdims=True)


def _per_row(vals, rows):
    return jnp.concatenate([jnp.broadcast_to(v, (rows, 1)) for v in vals], axis=0)


def _block_diag(x, rblk, n):
    return jnp.concatenate([jnp.where(rblk == c, x, 0.0) for c in range(n)], axis=1)


def _conv_silu(hist_ref, s, cw_ref, col0, rows):
    cols = slice(col0, col0 + DH)
    base = SUBLANES - GDN_CONV + 1
    acc = hist_ref[s, pl.ds(base, rows), cols] * cw_ref[0:1, cols]
    for j in range(1, GDN_CONV):
        acc = acc + hist_ref[s, pl.ds(base + j, rows), cols] * cw_ref[j:j + 1, cols]
    return acc * jax.nn.sigmoid(acc)


def _seq_loader(ref, scr, nseq, valid, rows):
    if valid == rows:
        return lambda s, cols: ref[pl.ds(s * rows, rows), cols]
    for s in range(nseq):
        scr[s, pl.ds(0, valid), :] = ref[pl.ds(s * valid, valid), :]
        scr[s, pl.ds(valid, rows - valid), :] = jnp.zeros((rows - valid, scr.shape[-1]), F32)
    return lambda s, cols: scr[s, :, cols]


def _problems(nseq, gsz):
    probs = [(s, h) for s in range(nseq) for h in range(HEADS)]
    return [probs[i:i + gsz] for i in range(0, len(probs), gsz)]


def _gdn_kernel(qkv_ref, z_ref, gates_ref, cst_ref, s0_ref, cw_ref, nw_ref, o_ref, s_ref, cst_out_ref, hist, *pads,
                nseq, rows, valid, levels, gsz):
    c = pl.program_id(1)
    keep = GDN_CONV - 1

    @pl.when(c == 0)
    def _():
        hist[:, pl.ds(SUBLANES - keep, keep), :] = cst_ref[...]
        s_ref[...] = s0_ref[...]

    for s in range(nseq):
        hist[s, pl.ds(SUBLANES, valid), :] = qkv_ref[pl.ds(s * valid, valid), :]
        if valid < rows:
            hist[s, pl.ds(SUBLANES + valid, rows - valid), :] = jnp.zeros((rows - valid, QKV), F32)
    zpad, gpad = pads if pads else (None, None)
    load_z = _seq_loader(z_ref, zpad, nseq, valid, rows)
    load_g = _seq_loader(gates_ref, gpad, nseq, valid, rows)

    r = gsz * rows
    eye, incl, strict = _group_masks(gsz, rows)
    eye_f = eye.astype(F32)
    rvalid = (lax.broadcasted_iota(jnp.int32, (r, 1), 0) & (rows - 1)) < valid
    rblk = _row_block(r, rows, gsz)

    for group in _problems(nseq, gsz):
        def stack(fn):
            return jnp.concatenate([fn(s, h) for s, h in group], axis=0)

        q = stack(lambda s, h: _conv_silu(hist, s, cw_ref, h * DH, rows))
        k = stack(lambda s, h: _conv_silu(hist, s, cw_ref, QK + h * DH, rows))
        v = stack(lambda s, h: _conv_silu(hist, s, cw_ref, 2 * QK + h * DH, rows))
        q = q * lax.rsqrt(jnp.sum(q * q, axis=-1, keepdims=True) + EPS) * (DH ** -0.5)
        k = k * lax.rsqrt(jnp.sum(k * k, axis=-1, keepdims=True) + EPS)
        g = jnp.where(rvalid, stack(lambda s, h: load_g(s, slice(h, h + 1))), 0.0)
        beta = jnp.where(rvalid, stack(lambda s, h: load_g(s, slice(HEADS + h, HEADS + h + 1))), 0.0)

        gc = _cumsum_col(g, eye, incl)
        gr = _row_of(gc, eye)
        decay = jnp.where(incl, jnp.exp(jnp.where(incl, gc - gr, 0.0)), 0.0)
        a = jnp.where(strict, beta * _dot_nt(k, k) * decay, 0.0)
        qk = _dot_nt(q, k) * decay
        bk = -a
        t = eye_f + bk
        if levels >= 2:
            bk = _dot(bk, bk)
            for _ in range(2, levels):
                t, bk = t + _dot(t, bk), _dot(bk, bk)
            t = t + _dot(t, bk)
        eg = jnp.exp(gc)
        uw = _dot(t, jnp.concatenate([v * beta, k * (beta * eg)], axis=1))
        u, w = uw[:, :DH], uw[:, DH:]

        s_old = [s_ref[s, h] for s, h in group]
        s_stack = jnp.concatenate(s_old, axis=0)
        v_new = u - _dot(_block_diag(w, rblk, gsz), s_stack)
        o = _dot(_block_diag(q * eg, rblk, gsz), s_stack) + _dot(qk, v_new)
        g_last = [gc[(i + 1) * rows - 1:(i + 1) * rows, :] for i in range(gsz)]
        kd = k * jnp.exp(_per_row(g_last, rows) - gc)
        if rows >= CHUNK:
            s_upd = [_dot_tn(kd[i * rows:(i + 1) * rows], v_new[i * rows:(i + 1) * rows]) for i in range(gsz)]
        else:
            s_all = _dot_tn(_block_diag(kd, rblk, gsz), v_new)
            s_upd = [s_all[i * DH:(i + 1) * DH] for i in range(gsz)]
        for i, (s, h) in enumerate(group):
            s_ref[s, h] = s_old[i] * jnp.exp(g_last[i]) + s_upd[i]

        zs = stack(lambda s, h: load_z(s, slice(h * DH, (h + 1) * DH)))
        on = _rms(o, nw_ref[...]) * (zs * jax.nn.sigmoid(zs))
        for i, (s, h) in enumerate(group):
            o_ref[pl.ds(s * valid, valid), h * DH:(h + 1) * DH] = on[i * rows:i * rows + valid]

    for s in range(nseq):
        hist[s, pl.ds(SUBLANES - keep, keep), :] = hist[s, pl.ds(SUBLANES + valid - keep, keep), :]
    cst_out_ref[...] = hist[:, pl.ds(SUBLANES - keep, keep), :]


def _gdn(big, gates, conv_state, s0, conv_w, norm_w, *, nb, nc, nseq, rows, valid, gsz):
    levels = max(1, math.ceil(math.log2(valid)))
    kern = functools.partial(_gdn_kernel, nseq=nseq, rows=rows, valid=valid, levels=levels, gsz=gsz)
    br = nseq * valid
    scratch = [pltpu.VMEM((nseq, SUBLANES + rows, QKV), F32)]
    if valid < rows:
        scratch += [pltpu.VMEM((nseq, rows, QK), F32), pltpu.VMEM((nseq, rows, N_GATES), F32)]
    return pl.pallas_call(
        kern,
        grid=(nb // nseq, nc),
        in_specs=[
            pl.BlockSpec((br, QKV), lambda b, c: (b * nc + c, 0)),
            pl.BlockSpec((br, QK), lambda b, c: (b * nc + c, 2 * QKV // QK)),
            pl.BlockSpec((br, N_GATES), lambda b, c: (b * nc + c, 0)),
            pl.BlockSpec((nseq, GDN_CONV - 1, QKV), lambda b, c: (b, 0, 0)),
            pl.BlockSpec((nseq, HEADS, DH, DH), lambda b, c: (b, 0, 0, 0)),
            pl.BlockSpec((GDN_CONV, QKV), lambda b, c: (0, 0)),
            pl.BlockSpec((1, DH), lambda b, c: (0, 0)),
        ],
        out_specs=[
            pl.BlockSpec((br, QK), lambda b, c: (b * nc + c, 0)),
            pl.BlockSpec((nseq, HEADS, DH, DH), lambda b, c: (b, 0, 0, 0)),
            pl.BlockSpec((nseq, GDN_CONV - 1, QKV), lambda b, c: (b, 0, 0)),
        ],
        out_shape=[jax.ShapeDtypeStruct((nb * nc * valid, QK), F32),
                   jax.ShapeDtypeStruct((nb, HEADS, DH, DH), F32),
                   jax.ShapeDtypeStruct((nb, GDN_CONV - 1, QKV), F32)],
        scratch_shapes=scratch,
        compiler_params=_params("parallel", "arbitrary"),
        name="gdn",
    )(big, big, gates, conv_state, s0, conv_w, norm_w)


def _mlstm_kernel(qkv_ref, og_ref, gates_ref, c0_ref, n0_ref, m0_ref, h_ref, c_ref, n_ref, m_ref, *pads,
                  nseq, rows, valid, gsz):
    @pl.when(pl.program_id(1) == 0)
    def _():
        c_ref[...] = c0_ref[...]
        n_ref[...] = n0_ref[...]
        m_ref[...] = m0_ref[...]

    r = gsz * rows
    eye, incl, _ = _group_masks(gsz, rows)
    rvalid = (lax.broadcasted_iota(jnp.int32, (r, 1), 0) & (rows - 1)) < valid
    rblk = _row_block(r, rows, gsz)
    xpad, opad, gpad = pads if pads else (None, None, None)
    load_x = _seq_loader(qkv_ref, xpad, nseq, valid, rows)
    load_o = _seq_loader(og_ref, opad, nseq, valid, rows)
    load_g = _seq_loader(gates_ref, gpad, nseq, valid, rows)
    m_all = m_ref[...]

    for group in _problems(nseq, gsz):
        def stack(fn):
            return jnp.concatenate([fn(s, h) for s, h in group], axis=0)

        q = stack(lambda s, h: load_x(s, slice(h * DH, (h + 1) * DH)))
        k = stack(lambda s, h: load_x(s, slice(QK + h * DH, QK + (h + 1) * DH))) * (DH ** -0.5)
        v = stack(lambda s, h: load_x(s, slice(2 * QK + h * DH, 2 * QK + (h + 1) * DH)))
        ig = jnp.where(rvalid, stack(lambda s, h: load_g(s, slice(2 * HEADS + h, 2 * HEADS + h + 1))), NEG)
        lf = jnp.where(rvalid, stack(lambda s, h: load_g(s, slice(3 * HEADS + h, 3 * HEADS + h + 1))), 0.0)

        bc = _cumsum_col(lf, eye, incl)
        br = _row_of(bc, eye)
        igr = _row_of(ig, eye)
        d_log = jnp.where(incl, bc - br + igr, NEG)
        d_max = jnp.max(d_log, axis=1, keepdims=True)
        b_last = [bc[(i + 1) * rows - 1:(i + 1) * rows, :] for i in range(gsz)]
        e_log = _per_row(b_last, rows) - bc + ig
        e_max = [jnp.max(e_log[i * rows:(i + 1) * rows], axis=0, keepdims=True) for i in range(gsz)]
        qk = _dot_nt(q, k)

        c_old = [c_ref[s, h] for s, h in group]
        n_old = [n_ref[s, h:h + 1, :] for s, h in group]
        m_old = [m_all[s, :, h:h + 1] for s, h in group]
        inter = bc + _per_row(m_old, rows)
        mt = jnp.maximum(inter, d_max)
        wi = jnp.exp(inter - mt)
        p = jnp.where(incl, jnp.exp(d_log - mt), 0.0) * qk
        qc = _dot_nt(_block_diag(q, rblk, gsz), jnp.concatenate(c_old, axis=1))
        num = wi * qc + _dot(p, v)
        n_rows = jnp.concatenate([jnp.broadcast_to(nv, (rows, DH)) for nv in n_old], axis=0)
        den = wi * jnp.sum(q * n_rows, axis=-1, keepdims=True) + jnp.sum(p, axis=-1, keepdims=True)
        hv = num / jnp.maximum(jnp.abs(den), jnp.exp(-mt))

        m_new = [jnp.maximum(b_last[i] + m_old[i], e_max[i]) for i in range(gsz)]
        fw = [jnp.exp(b_last[i] + m_old[i] - m_new[i]) for i in range(gsz)]
        sw = jnp.exp(e_log - _per_row(m_new, rows))
        c_upd = _dot_tn(sw * v, _block_diag(k, rblk, gsz))
        swk = sw * k
        for i, (s, h) in enumerate(group):
            c_ref[s, h] = fw[i] * c_old[i] + c_upd[:, i * DH:(i + 1) * DH]
            n_ref[s, h:h + 1, :] = fw[i] * n_old[i] + jnp.sum(swk[i * rows:(i + 1) * rows], axis=0, keepdims=True)
            m_ref[s, :, h:h + 1] = m_new[i]

        og = stack(lambda s, h: load_o(s, slice(h * DH, (h + 1) * DH)))
        hg = hv * jax.nn.sigmoid(og)
        for i, (s, h) in enumerate(group):
            h_ref[pl.ds(s * valid, valid), h * DH:(h + 1) * DH] = hg[i * rows:i * rows + valid]


def _mlstm(big, gates, c0, n0, m0, *, nb, nc, nseq, rows, valid, gsz):
    kern = functools.partial(_mlstm_kernel, nseq=nseq, rows=rows, valid=valid, gsz=gsz)
    br = nseq * valid
    scratch = []
    if valid < rows:
        scratch = [pltpu.VMEM((nseq, rows, QKV), F32), pltpu.VMEM((nseq, rows, QK), F32),
                   pltpu.VMEM((nseq, rows, N_GATES), F32)]
    return pl.pallas_call(
        kern,
        grid=(nb // nseq, nc),
        in_specs=[
            pl.BlockSpec((br, QKV), lambda b, c: (b * nc + c, 1)),
            pl.BlockSpec((br, QK), lambda b, c: (b * nc + c, 2 * QKV // QK + 1)),
            pl.BlockSpec((br, N_GATES), lambda b, c: (b * nc + c, 0)),
            pl.BlockSpec((nseq, HEADS, DH, DH), lambda b, c: (b, 0, 0, 0)),
            pl.BlockSpec((nseq, HEADS, DH), lambda b, c: (b, 0, 0)),
            pl.BlockSpec((nseq, 1, HEADS), lambda b, c: (b, 0, 0)),
        ],
        out_specs=[
            pl.BlockSpec((br, QK), lambda b, c: (b * nc + c, 0)),
            pl.BlockSpec((nseq, HEADS, DH, DH), lambda b, c: (b, 0, 0, 0)),
            pl.BlockSpec((nseq, HEADS, DH), lambda b, c: (b, 0, 0)),
            pl.BlockSpec((nseq, 1, HEADS), lambda b, c: (b, 0, 0)),
        ],
        out_shape=[jax.ShapeDtypeStruct((nb * nc * valid, QK), F32),
                   jax.ShapeDtypeStruct((nb, HEADS, DH, DH), F32),
                   jax.ShapeDtypeStruct((nb, HEADS, DH), F32),
                   jax.ShapeDtypeStruct((nb, 1, HEADS), F32)],
        scratch_shapes=scratch,
        compiler_params=_params("parallel", "arbitrary"),
        name="mlstm",
    )(big, big, gates, c0, n0, m0)


def _merge_kernel(go_ref, mh_ref, ga_ref, gb_ref, x_ref, wa_ref, wb_ref, wo_ref, x1_ref, mix_scr, *, nt, tn):
    j = pl.program_id(1)

    @pl.when(j < nt)
    def _():
        ya = jnp.dot(go_ref[...].astype(BF16), wa_ref[...], preferred_element_type=F32)
        yb = jnp.dot(mh_ref[...].astype(BF16), wb_ref[...], preferred_element_type=F32)
        mixed = jax.nn.sigmoid(ga_ref[...]) * ya + jax.nn.sigmoid(gb_ref[...]) * yb
        mixed = mixed.astype(BF16)
        for t in range(nt):
            @pl.when(j == t)
            def _():
                mix_scr[:, t * tn:(t + 1) * tn] = mixed

    @pl.when(j >= nt)
    def _():
        x1_ref[...] = x_ref[...] + jnp.dot(mix_scr[...], wo_ref[...], preferred_element_type=F32)


def _merge(go, mh, big, x, wa, wb, wo, *, tm, tn):
    m = x.shape[0]
    nt = D_MODEL // tn
    ga_blk = (2 * QKV + 2 * QK) // tn
    mix_j = lambda j: jnp.minimum(j, nt - 1)
    out_j = lambda j: jnp.maximum(j - nt, 0)
    return pl.pallas_call(
        functools.partial(_merge_kernel, nt=nt, tn=tn),
        grid=(m // tm, 2 * nt),
        in_specs=[
            pl.BlockSpec((tm, QK), lambda i, j: (i, 0)),
            pl.BlockSpec((tm, QK), lambda i, j: (i, 0)),
            pl.BlockSpec((tm, tn), lambda i, j: (i, ga_blk + mix_j(j))),
            pl.BlockSpec((tm, tn), lambda i, j: (i, ga_blk + nt + mix_j(j))),
            pl.BlockSpec((tm, tn), lambda i, j: (i, out_j(j))),
            pl.BlockSpec((QK, tn), lambda i, j: (0, mix_j(j))),
            pl.BlockSpec((QK, tn), lambda i, j: (0, mix_j(j))),
            pl.BlockSpec((D_MODEL, tn), lambda i, j: (0, out_j(j))),
        ],
        out_specs=pl.BlockSpec((tm, tn), lambda i, j: (i, out_j(j))),
        out_shape=jax.ShapeDtypeStruct((m, D_MODEL), F32),
        scratch_shapes=[pltpu.VMEM((tm, D_MODEL), BF16)],
        compiler_params=_params("parallel", "arbitrary"),
        name="merge",
    )(go, mh, big, big, x, wa, wb, wo)


def _ffn_kernel(*refs, tm, tf, seq_tiles, with_state, srows):
    keep = FFN_CONV - 1
    if with_state:
        (x1_ref, nw_ref, wg_ref, wv_ref, cwg_ref, cwv_ref, wd_ref, fnw_ref) = refs[:8]
        stg_refs, stv_refs = refs[8:8 + keep], refs[8 + keep:8 + 2 * keep]
        y_ref = refs[8 + 2 * keep]
        newg_refs, newv_refs = refs[9 + 2 * keep:9 + 3 * keep], refs[9 + 3 * keep:9 + 4 * keep]
        h2_scr, hist_g, hist_v, acc_scr, st_g, st_v, cp_g, cp_v = refs[9 + 4 * keep:]
        nsq = tm // srows
        nck = tf // LANES
    else:
        (x1_ref, halo_ref, nw_ref, wg_ref, wv_ref, cwg_ref, cwv_ref, wd_ref, fnw_ref,
         y_ref, upg_ref, upv_ref, h2_scr, hist_g, hist_v, acc_scr) = refs
    i = pl.program_id(0)
    f = pl.program_id(1)
    pad = SUBLANES

    @pl.when(f == 0)
    def _():
        if with_state:
            h2_scr[pl.ds(0, pad), :] = jnp.zeros((pad, D_MODEL), BF16)
        else:
            live = (i % seq_tiles != 0).astype(F32)
            h2_scr[pl.ds(0, pad), :] = (_rms(halo_ref[...], nw_ref[...]) * live).astype(BF16)
        h2_scr[pl.ds(pad, tm), :] = _rms(x1_ref[...], nw_ref[...]).astype(BF16)
        acc_scr[...] = jnp.zeros_like(acc_scr)

    h2 = h2_scr[...]
    hist_g[...] = jnp.dot(h2, wg_ref[...], preferred_element_type=F32)
    hist_v[...] = jnp.dot(h2, wv_ref[...], preferred_element_type=F32)

    if with_state:
        for st, cp, st_refs, hist, new_refs in ((st_g, cp_g, stg_refs, hist_g, newg_refs),
                                                (st_v, cp_v, stv_refs, hist_v, newv_refs)):
            st[...] = jnp.zeros_like(st)
            for c in range(nck):
                lanes = slice(c * LANES, (c + 1) * LANES)
                cp[c] = hist[:, lanes]
                for r in range(keep):
                    st[c, pl.ds(r, nsq, stride=srows), :] = st_refs[r][:, lanes]
                    new_refs[r][:, lanes] = cp[c, pl.ds(pad + srows - keep + r, nsq, stride=srows), :]
        rmod = lax.broadcasted_iota(jnp.int32, (tm, 1), 0) % srows
        slab = lambda st, off: jnp.concatenate([st[c, pl.ds(off, tm), :] for c in range(nck)], axis=1)
    else:
        upg_ref[...] = hist_g[pl.ds(tm, pad), :]
        upv_ref[...] = hist_v[pl.ds(tm, pad), :]

    def conv(hist, cw_ref, st):
        prev2 = hist[pl.ds(pad - 2, tm), :]
        prev1 = hist[pl.ds(pad - 1, tm), :]
        if with_state:
            prev2 = jnp.where(rmod < 2, slab(st, 0), prev2)
            prev1 = jnp.where(rmod < 1, slab(st, 1), prev1)
        return (prev2 * cw_ref[0:1, :] + prev1 * cw_ref[1:2, :]) + hist[pl.ds(pad, tm), :] * cw_ref[2:3, :]

    ug = conv(hist_g, cwg_ref, st_g if with_state else None)
    uv = conv(hist_v, cwv_ref, st_v if with_state else None)
    act = (ug * jax.nn.sigmoid(ug) * uv).astype(BF16)
    acc_scr[...] += jnp.dot(act, wd_ref[...], preferred_element_type=F32)

    @pl.when(f == pl.num_programs(1) - 1)
    def _():
        y_ref[...] = _rms(x1_ref[...] + acc_scr[...], fnw_ref[...])


def _ffn(x1, nw, wup, cw, wd, fnw, state=None, *, tm, tf, seq_tiles, srows=None):
    m = x1.shape[0]
    nf = D_FF // tf
    keep = FFN_CONV - 1
    with_state = state is not None
    kern = functools.partial(_ffn_kernel, tm=tm, tf=tf, seq_tiles=seq_tiles, with_state=with_state, srows=srows)
    in_specs = [pl.BlockSpec((tm, D_MODEL), lambda i, f: (i, 0))]
    args = [x1]
    if not with_state:
        in_specs.append(pl.BlockSpec((SUBLANES, D_MODEL),
                                     lambda i, f: (jnp.maximum(i * (tm // SUBLANES) - 1, 0), 0)))
        args.append(x1)
    in_specs += [
        pl.BlockSpec((1, D_MODEL), lambda i, f: (0, 0)),
        pl.BlockSpec((D_MODEL, tf), lambda i, f: (0, f)),
        pl.BlockSpec((D_MODEL, tf), lambda i, f: (0, f + nf)),
        pl.BlockSpec((FFN_CONV, tf), lambda i, f: (0, f)),
        pl.BlockSpec((FFN_CONV, tf), lambda i, f: (0, f + nf)),
        pl.BlockSpec((tf, D_MODEL), lambda i, f: (f, 0)),
        pl.BlockSpec((1, D_MODEL), lambda i, f: (0, 0)),
    ]
    args += [nw, wup, wup, cw, cw, wd, fnw]
    scratch = [pltpu.VMEM((SUBLANES + tm, D_MODEL), BF16),
               pltpu.VMEM((SUBLANES + tm, tf), F32),
               pltpu.VMEM((SUBLANES + tm, tf), F32),
               pltpu.VMEM((tm, D_MODEL), F32)]
    n_tiles = m // tm
    out_specs = [pl.BlockSpec((tm, D_MODEL), lambda i, f: (i, 0))]
    out_shape = [jax.ShapeDtypeStruct((m, D_MODEL), F32)]
    if with_state:
        nsq = tm // srows
        in_specs += [pl.BlockSpec((nsq, tf), lambda i, f: (i, f))] * keep
        in_specs += [pl.BlockSpec((nsq, tf), lambda i, f: (i, f + nf))] * keep
        args += list(state) * 2
        scratch += [pltpu.VMEM((tf // LANES, tm + SUBLANES, LANES), F32)] * 4
        out_specs += [pl.BlockSpec((nsq, tf), lambda i, f: (i, f))] * (2 * keep)
        out_shape += [jax.ShapeDtypeStruct((m // srows, D_FF), F32)] * (2 * keep)
    else:
        out_specs += [pl.BlockSpec((SUBLANES, tf), lambda i, f: (i, f))] * 2
        out_shape += [jax.ShapeDtypeStruct((n_tiles * SUBLANES, D_FF), F32)] * 2
    outs = pl.pallas_call(
        kern,
        grid=(n_tiles, nf),
        in_specs=in_specs,
        out_specs=out_specs,
        out_shape=out_shape,
        scratch_shapes=scratch,
        compiler_params=_params("parallel", "arbitrary"),
        name="ffn",
    )(*args)
    return outs[0], outs[1:]


def _layer(x, nb, nc, nseq, rows, valid, gsz, states, w, *, tm_in, tm_merge, tm_ffn, seq_tiles, ffn_state):
    conv_state, s0, c0, n0, m0 = states
    big, gates = _inproj(x, w["norm_mix"], w["w_big"], w["w_small"], w["gate_bias"], w["a_log"],
                         tm=tm_in, tn=1024)
    go, s_new, conv_new = _gdn(big, gates, conv_state, s0, w["gdn_conv_w"], w["gdn_norm"],
                               nb=nb, nc=nc, nseq=nseq, rows=rows, valid=valid, gsz=gsz)
    mh, c_new, n_new, m_new = _mlstm(big, gates, c0, n0, m0,
                                     nb=nb, nc=nc, nseq=nseq, rows=rows, valid=valid, gsz=gsz)
    x1 = _merge(go, mh, big, x, w["w_a"], w["w_b"], w["w_out"], tm=tm_merge, tn=512)
    y, ffn_rows = _ffn(x1, w["norm_ffn"], w["w_up"], w["ffn_conv_w"], w["w_down"], w["norm_final"],
                       ffn_state, tm=tm_ffn, tf=512, seq_tiles=seq_tiles, srows=valid)
    return y, (conv_new, s_new, c_new, n_new, m_new.reshape(nb, HEADS)), ffn_rows


def kernel(x_prompt, x_sample, state_gdn_conv, state_gdn_S, state_ml_C, state_ml_n, state_ml_m, state_ffn_conv,
           norm_mix_w, w_in, gdn_conv_w, gdn_A_log, gdn_dt_bias, gdn_norm_w, w_branch_a, ml_b_i, ml_b_f,
           w_branch_b, w_out, norm_ffn_w, w_up, ffn_conv_w, w_down, norm_final_w):
    assert w_in.shape[0] == 1, "single-layer step"
    bp, tp, _ = x_prompt.shape
    bs, ts, _ = x_sample.shape
    assert FFN_CONV - 1 <= ts <= SUBLANES and GDN_CONV - 1 <= ts and tp % CHUNK == 0
    assert w_in.shape[1:] == (D_MODEL, IN_COLS)
    keep = FFN_CONV - 1

    w_big, w_small = _repack(w_in[0].T, tn=512)
    zeros8 = jnp.zeros((HEADS,), F32)
    w = {
        "norm_mix": norm_mix_w[0][None, :],
        "w_big": w_big,
        "w_small": w_small,
        "gate_bias": jnp.concatenate([gdn_dt_bias[0], zeros8, ml_b_i[0], ml_b_f[0]])[None, :],
        "a_log": jnp.concatenate([gdn_A_log[0], zeros8, zeros8, zeros8])[None, :],
        "gdn_conv_w": gdn_conv_w[0],
        "gdn_norm": gdn_norm_w[0][None, :],
        "w_a": w_branch_a[0].astype(BF16),
        "w_b": w_branch_b[0].astype(BF16),
        "w_out": w_out[0].astype(BF16),
        "norm_ffn": norm_ffn_w[0][None, :],
        "w_up": w_up[0].astype(BF16),
        "ffn_conv_w": ffn_conv_w[0],
        "w_down": w_down[0].astype(BF16),
        "norm_final": norm_final_w[None, :],
    }

    xp = x_prompt.reshape(bp * tp, D_MODEL)
    p_states = (jnp.zeros((bp, GDN_CONV - 1, QKV), F32), jnp.zeros((bp, HEADS, DH, DH), F32),
                jnp.zeros((bp, HEADS, DH, DH), F32), jnp.zeros((bp, HEADS, DH), F32),
                jnp.zeros((bp, 1, HEADS), F32))
    tm_ffn = 512
    yp, p_new, up_p = _layer(
        xp, bp, tp // CHUNK, 1, CHUNK, CHUNK, 4, p_states, w,
        tm_in=1024, tm_merge=1024, tm_ffn=tm_ffn, seq_tiles=tp // tm_ffn, ffn_state=None)
    p_ffn_conv = jnp.concatenate(
        [u.reshape(bp, tp // tm_ffn, SUBLANES, D_FF)[:, -1, SUBLANES - keep:, :] for u in up_p], axis=-1)

    xs = x_sample.reshape(bs * ts, D_MODEL)
    s_states = (state_gdn_conv[0], state_gdn_S[0], state_ml_C[0], state_ml_n[0],
                state_ml_m[0].reshape(bs, 1, HEADS))
    ys, s_new, new_rows = _layer(
        xs, bs, 1, 4, SUBLANES, ts, 2 * HEADS, s_states, w,
        tm_in=bs * ts, tm_merge=bs * ts, tm_ffn=bs * ts, seq_tiles=1,
        ffn_state=[state_ffn_conv[0, :, r, :] for r in range(keep)])
    s_ffn_conv = jnp.stack([jnp.concatenate([new_rows[r], new_rows[keep + r]], axis=-1) for r in range(keep)], axis=1)

    lead = lambda t: tuple(a[None] for a in t)
    return (yp.reshape(bp, tp, D_MODEL), ys.reshape(bs, ts, D_MODEL),
            *lead(p_new), p_ffn_conv[None], *lead(s_new), s_ffn_conv[None])
```

```python
import functools
import math

import jax
import jax.numpy as jnp
from jax import lax
from jax.experimental import pallas as pl
from jax.experimental.pallas import tpu as pltpu

F32 = jnp.float32
BF16 = jnp.bfloat16

D_MODEL = 2048
HEADS = 8
DH = 128
QK = HEADS * DH
QKV = 3 * QK
D_FF = 5632
GDN_CONV = 4
FFN_CONV = 3
EPS = 1e-6
CHUNK = 64
N_GATES = 4 * HEADS
SUBLANES = 8
NEG = -1e30

BIG_COLS = 2 * QKV + 2 * QK + 2 * D_MODEL

VMEM_LIMIT = 56 * 1024 * 1024


def _params(*sem):
    return pltpu.CompilerParams(dimension_semantics=sem, vmem_limit_bytes=VMEM_LIMIT)


def _dot(a, b):
    return jnp.dot(a.astype(BF16), b.astype(BF16), preferred_element_type=F32)


def _dot_nt(a, b):
    return lax.dot_general(a.astype(BF16), b.astype(BF16), (((1,), (1,)), ((), ())),
                           preferred_element_type=F32)


def _dot_tn(a, b):
    return lax.dot_general(a.astype(BF16), b.astype(BF16), (((0,), (0,)), ((), ())),
                           preferred_element_type=F32)


def _softplus(x):
    return jnp.maximum(x, 0.0) + jnp.log1p(jnp.exp(-jnp.abs(x)))


def _rms(x, w):
    return x * lax.rsqrt(jnp.mean(x * x, axis=-1, keepdims=True) + EPS) * w


IN_SECTIONS = (("gqkv", QKV), ("gz", QK), ("ga", HEADS), ("gb", HEADS), ("mqkv", QKV),
               ("mi", HEADS), ("mf", HEADS), ("mo", QK), ("gA", D_MODEL), ("gB", D_MODEL))
BIG_ORDER = ("gqkv", "mqkv", "gz", "mo", "gA", "gB")
IN_COLS = sum(size for _, size in IN_SECTIONS)


def _in_start(name):
    off = 0
    for n, size in IN_SECTIONS:
        if n == name:
            return off
        off += size
    raise KeyError(name)


def _big_tiles(tn):
    starts = []
    for name in BIG_ORDER:
        size = dict(IN_SECTIONS)[name]
        assert size % tn == 0
        starts += [_in_start(name) + t * tn for t in range(size // tn)]
    return starts


def _tile_lookup(j, values):
    out = jnp.int32(values[-1])
    for t in range(len(values) - 2, -1, -1):
        out = jnp.where(j <= t, jnp.int32(values[t]), out)
    return out


LANES = 128
SMALL_ORDER = ("ga", "gb", "mi", "mf")


def _repack_kernel(*refs):
    wt_ref, g_refs, out_ref, small_ref = refs[0], refs[1:1 + len(SMALL_ORDER)], refs[-2], refs[-1]

    @pl.when(pl.program_id(0) == 0)
    def _():
        cols = [g_ref[...].T[:, _in_start(n) % SUBLANES:_in_start(n) % SUBLANES + HEADS]
                for n, g_ref in zip(SMALL_ORDER, g_refs)]
        small_ref[...] = jnp.concatenate(cols, axis=1).astype(BF16)

    out_ref[...] = wt_ref[...].T.astype(BF16)


def _repack(w_in_t, *, tn):
    starts = _big_tiles(tn)
    assert all(s % SUBLANES == 0 for s in starts)
    rows8 = [s // SUBLANES for s in starts]
    return pl.pallas_call(
        _repack_kernel,
        grid=(BIG_COLS // tn,),
        in_specs=[pl.BlockSpec((pl.Element(tn), pl.Element(D_MODEL)),
                               lambda j: (_tile_lookup(j, rows8) * SUBLANES, 0))]
        + [pl.BlockSpec((SUBLANES, D_MODEL), lambda j, b=_in_start(n) // SUBLANES: (b, 0)) for n in SMALL_ORDER],
        out_specs=[pl.BlockSpec((D_MODEL, tn), lambda j: (0, j)),
                   pl.BlockSpec((D_MODEL, N_GATES), lambda j: (0, 0))],
        out_shape=[jax.ShapeDtypeStruct((D_MODEL, BIG_COLS), BF16),
                   jax.ShapeDtypeStruct((D_MODEL, N_GATES), BF16)],
        compiler_params=_params("arbitrary"),
        name="repack",
    )(*([w_in_t] * (1 + len(SMALL_ORDER))))


def _inproj_kernel(x_ref, nw_ref, wbig_ref, wsm_ref, gbias_ref, alog_ref, big_ref, gates_ref, h_scr):
    @pl.when(pl.program_id(1) == 0)
    def _():
        hb = _rms(x_ref[...], nw_ref[...]).astype(BF16)
        h_scr[...] = hb
        raw = jnp.dot(hb, wsm_ref[...], preferred_element_type=F32)
        lane = lax.broadcasted_iota(jnp.int32, raw.shape, 1)
        z = raw + gbias_ref[...]
        g = -jnp.exp(alog_ref[...]) * _softplus(z)
        beta = jax.nn.sigmoid(raw)
        lf = -_softplus(-z)
        gates_ref[...] = jnp.where(lane < HEADS, g,
                                   jnp.where(lane < 2 * HEADS, beta,
                                             jnp.where(lane < 3 * HEADS, z, lf)))

    big_ref[...] = jnp.dot(h_scr[...], wbig_ref[...], preferred_element_type=F32)


def _inproj(x, nw, wbig, wsm, gbias, alog, *, tm, tn):
    m = x.shape[0]
    return pl.pallas_call(
        _inproj_kernel,
        grid=(m // tm, BIG_COLS // tn),
        in_specs=[
            pl.BlockSpec((tm, D_MODEL), lambda i, j: (i, 0)),
            pl.BlockSpec((1, D_MODEL), lambda i, j: (0, 0)),
            pl.BlockSpec((D_MODEL, tn), lambda i, j: (0, j)),
            pl.BlockSpec((D_MODEL, N_GATES), lambda i, j: (0, 0)),
            pl.BlockSpec((1, N_GATES), lambda i, j: (0, 0)),
            pl.BlockSpec((1, N_GATES), lambda i, j: (0, 0)),
        ],
        out_specs=[
            pl.BlockSpec((tm, tn), lambda i, j: (i, j)),
            pl.BlockSpec((tm, N_GATES), lambda i, j: (i, 0)),
        ],
        out_shape=[jax.ShapeDtypeStruct((m, BIG_COLS), F32),
                   jax.ShapeDtypeStruct((m, N_GATES), F32)],
        scratch_shapes=[pltpu.VMEM((tm, D_MODEL), BF16)],
        compiler_params=_params("parallel", "arbitrary"),
        name="inproj",
    )(x, nw, wbig, wsm, gbias, alog)


def _group_masks(n, rows):
    r = n * rows
    shift = rows.bit_length() - 1
    ri = lax.broadcasted_iota(jnp.int32, (r, r), 0)
    ci = lax.broadcasted_iota(jnp.int32, (r, r), 1)
    same = lax.shift_right_logical(ri, shift) == lax.shift_right_logical(ci, shift)
    return ri == ci, same & (ci <= ri), same & (ci < ri)


def _row_block(nrows, rows, n):
    shift = rows.bit_length() - 1
    ri = lax.broadcasted_iota(jnp.int32, (nrows, 1), 0)
    return lax.shift_right_logical(ri, shift) & (n - 1)


def _row_of(col, eye):
    return jnp.sum(jnp.where(eye, col, 0.0), axis=0, keepdims=True)


def _cumsum_col(col, eye, incl):
    return jnp.sum(jnp.where(incl, _row_of(col, eye), 0.0), axis=1, keepdims=True)


def _per_row(vals, rows):
    return jnp.concatenate([jnp.broadcast_to(v, (rows, 1)) for v in vals], axis=0)


def _block_diag(x, rblk, n):
    return jnp.concatenate([jnp.where(rblk == c, x, 0.0) for c in range(n)], axis=1)


def _conv_silu(hist_ref, s, cw_ref, col0, rows):
    cols = slice(col0, col0 + DH)
    base = SUBLANES - GDN_CONV + 1
    acc = hist_ref[s, pl.ds(base, rows), cols] * cw_ref[0:1, cols]
    for j in range(1, GDN_CONV):
        acc = acc + hist_ref[s, pl.ds(base + j, rows), cols] * cw_ref[j:j + 1, cols]
    return acc * jax.nn.sigmoid(acc)


def _seq_loader(ref, scr, nseq, valid, rows, base=0):
    if valid == rows:
        return lambda s, cols: ref[pl.ds(base + s * rows, rows), cols]
    assert base == 0
    for s in range(nseq):
        scr[s, pl.ds(0, valid), :] = ref[pl.ds(s * valid, valid), :]
        scr[s, pl.ds(valid, rows - valid), :] = jnp.zeros((rows - valid, scr.shape[-1]), F32)
    return lambda s, cols: scr[s, :, cols]


def _problems(nseq, gsz):
    probs = [(s, h) for s in range(nseq) for h in range(HEADS)]
    return [probs[i:i + gsz] for i in range(0, len(probs), gsz)]


def _gdn_kernel(qkv_ref, z_ref, gates_ref, cst_ref, s0_ref, cw_ref, nw_ref, o_ref, s_ref, cst_out_ref, hist, *pads,
                nseq, rows, valid, levels, gsz, cps):
    keep = GDN_CONV - 1

    @pl.when(pl.program_id(1) == 0)
    def _():
        hist[:, pl.ds(SUBLANES - keep, keep), :] = cst_ref[...]
        s_ref[...] = s0_ref[...]

    r = gsz * rows
    eye, incl, strict = _group_masks(gsz, rows)
    rvalid = (lax.broadcasted_iota(jnp.int32, (r, 1), 0) & (rows - 1)) < valid
    masks = (eye, incl, strict, eye.astype(F32), rvalid, _row_block(r, rows, gsz))
    for ck in range(cps):
        _gdn_chunk(ck * nseq * valid, qkv_ref, z_ref, gates_ref, cw_ref, nw_ref, o_ref, s_ref, hist, pads, masks,
                   nseq=nseq, rows=rows, valid=valid, levels=levels, gsz=gsz)
    cst_out_ref[...] = hist[:, pl.ds(SUBLANES - keep, keep), :]


def _gdn_chunk(base, qkv_ref, z_ref, gates_ref, cw_ref, nw_ref, o_ref, s_ref, hist, pads, masks,
               *, nseq, rows, valid, levels, gsz):
    eye, incl, strict, eye_f, rvalid, rblk = masks
    keep = GDN_CONV - 1
    r = gsz * rows
    for s in range(nseq):
        hist[s, pl.ds(SUBLANES, valid), :] = qkv_ref[pl.ds(base + s * valid, valid), :]
        if valid < rows:
            hist[s, pl.ds(SUBLANES + valid, rows - valid), :] = jnp.zeros((rows - valid, QKV), F32)
    zpad, gpad = pads if pads else (None, None)
    load_z = _seq_loader(z_ref, zpad, nseq, valid, rows, base)
    load_g = _seq_loader(gates_ref, gpad, nseq, valid, rows, base)

    for group in _problems(nseq, gsz):
        def stack(fn):
            return jnp.concatenate([fn(s, h) for s, h in group], axis=0)

        q = stack(lambda s, h: _conv_silu(hist, s, cw_ref, h * DH, rows))
        k = stack(lambda s, h: _conv_silu(hist, s, cw_ref, QK + h * DH, rows))
        v = stack(lambda s, h: _conv_silu(hist, s, cw_ref, 2 * QK + h * DH, rows))
        q = q * lax.rsqrt(jnp.sum(q * q, axis=-1, keepdims=True) + EPS) * (DH ** -0.5)
        k = k * lax.rsqrt(jnp.sum(k * k, axis=-1, keepdims=True) + EPS)
        g = jnp.where(rvalid, stack(lambda s, h: load_g(s, slice(h, h + 1))), 0.0)
        beta = jnp.where(rvalid, stack(lambda s, h: load_g(s, slice(HEADS + h, HEADS + h + 1))), 0.0)

        gc = _cumsum_col(g, eye, incl)
        gr = _row_of(gc, eye)
        decay = jnp.where(incl, jnp.exp(jnp.where(incl, gc - gr, 0.0)), 0.0)
        a = jnp.where(strict, beta * _dot_nt(k, k) * decay, 0.0)
        qk = _dot_nt(q, k) * decay
        bk = -a
        t = eye_f + bk
        if levels >= 2:
            bk = _dot(bk, bk)
            for _ in range(2, levels):
                t, bk = t + _dot(t, bk), _dot(bk, bk)
            t = t + _dot(t, bk)
        eg = jnp.exp(gc)
        uw = _dot(t, jnp.concatenate([v * beta, k * (beta * eg)], axis=1))
        u, w = uw[:, :DH], uw[:, DH:]

        s_old = [s_ref[s, h] for s, h in group]
        s_stack = jnp.concatenate(s_old, axis=0)
        v_new = u - _dot(_block_diag(w, rblk, gsz), s_stack)
        o = _dot(_block_diag(q * eg, rblk, gsz), s_stack) + _dot(qk, v_new)
        g_last = [gc[(i + 1) * rows - 1:(i + 1) * rows, :] for i in range(gsz)]
        kd = k * jnp.exp(_per_row(g_last, rows) - gc)
        if rows >= CHUNK:
            s_upd = [_dot_tn(kd[i * rows:(i + 1) * rows], v_new[i * rows:(i + 1) * rows]) for i in range(gsz)]
        else:
            s_all = _dot_tn(_block_diag(kd, rblk, gsz), v_new)
            s_upd = [s_all[i * DH:(i + 1) * DH] for i in range(gsz)]
        for i, (s, h) in enumerate(group):
            s_ref[s, h] = s_old[i] * jnp.exp(g_last[i]) + s_upd[i]

        zs = stack(lambda s, h: load_z(s, slice(h * DH, (h + 1) * DH)))
        on = _rms(o, nw_ref[...]) * (zs * jax.nn.sigmoid(zs))
        for i, (s, h) in enumerate(group):
            o_ref[pl.ds(base + s * valid, valid), h * DH:(h + 1) * DH] = on[i * rows:i * rows + valid]

    for s in range(nseq):
        hist[s, pl.ds(SUBLANES - keep, keep), :] = hist[s, pl.ds(SUBLANES + valid - keep, keep), :]


def _gdn(big, gates, conv_state, s0, conv_w, norm_w, *, nb, nc, nseq, rows, valid, gsz, cps):
    assert nc % cps == 0 and (cps == 1 or nseq == 1)
    nc = nc // cps
    levels = max(1, math.ceil(math.log2(valid)))
    kern = functools.partial(_gdn_kernel, nseq=nseq, rows=rows, valid=valid, levels=levels, gsz=gsz, cps=cps)
    br = nseq * valid * cps
    scratch = [pltpu.VMEM((nseq, SUBLANES + rows, QKV), F32)]
    if valid < rows:
        scratch += [pltpu.VMEM((nseq, rows, QK), F32), pltpu.VMEM((nseq, rows, N_GATES), F32)]
    return pl.pallas_call(
        kern,
        grid=(nb // nseq, nc),
        in_specs=[
            pl.BlockSpec((br, QKV), lambda b, c: (b * nc + c, 0)),
            pl.BlockSpec((br, QK), lambda b, c: (b * nc + c, 2 * QKV // QK)),
            pl.BlockSpec((br, N_GATES), lambda b, c: (b * nc + c, 0)),
            pl.BlockSpec((nseq, GDN_CONV - 1, QKV), lambda b, c: (b, 0, 0)),
            pl.BlockSpec((nseq, HEADS, DH, DH), lambda b, c: (b, 0, 0, 0)),
            pl.BlockSpec((GDN_CONV, QKV), lambda b, c: (0, 0)),
            pl.BlockSpec((1, DH), lambda b, c: (0, 0)),
        ],
        out_specs=[
            pl.BlockSpec((br, QK), lambda b, c: (b * nc + c, 0)),
            pl.BlockSpec((nseq, HEADS, DH, DH), lambda b, c: (b, 0, 0, 0)),
            pl.BlockSpec((nseq, GDN_CONV - 1, QKV), lambda b, c: (b, 0, 0)),
        ],
        out_shape=[jax.ShapeDtypeStruct((nb // nseq * nc * br, QK), F32),
                   jax.ShapeDtypeStruct((nb, HEADS, DH, DH), F32),
                   jax.ShapeDtypeStruct((nb, GDN_CONV - 1, QKV), F32)],
        scratch_shapes=scratch,
        compiler_params=_params("parallel", "arbitrary"),
        name="gdn",
    )(big, big, gates, conv_state, s0, conv_w, norm_w)


def _mlstm_kernel(qkv_ref, og_ref, gates_ref, c0_ref, n0_ref, m0_ref, h_ref, c_ref, n_ref, m_ref, *pads,
                  nseq, rows, valid, gsz):
    @pl.when(pl.program_id(1) == 0)
    def _():
        c_ref[...] = c0_ref[...]
        n_ref[...] = n0_ref[...]
        m_ref[...] = m0_ref[...]

    r = gsz * rows
    eye, incl, _ = _group_masks(gsz, rows)
    rvalid = (lax.broadcasted_iota(jnp.int32, (r, 1), 0) & (rows - 1)) < valid
    rblk = _row_block(r, rows, gsz)
    xpad, opad, gpad = pads if pads else (None, None, None)
    load_x = _seq_loader(qkv_ref, xpad, nseq, valid, rows)
    load_o = _seq_loader(og_ref, opad, nseq, valid, rows)
    load_g = _seq_loader(gates_ref, gpad, nseq, valid, rows)
    m_all = m_ref[...]

    for group in _problems(nseq, gsz):
        def stack(fn):
            return jnp.concatenate([fn(s, h) for s, h in group], axis=0)

        q = stack(lambda s, h: load_x(s, slice(h * DH, (h + 1) * DH)))
        k = stack(lambda s, h: load_x(s, slice(QK + h * DH, QK + (h + 1) * DH))) * (DH ** -0.5)
        v = stack(lambda s, h: load_x(s, slice(2 * QK + h * DH, 2 * QK + (h + 1) * DH)))
        ig = jnp.where(rvalid, stack(lambda s, h: load_g(s, slice(2 * HEADS + h, 2 * HEADS + h + 1))), NEG)
        lf = jnp.where(rvalid, stack(lambda s, h: load_g(s, slice(3 * HEADS + h, 3 * HEADS + h + 1))), 0.0)

        bc = _cumsum_col(lf, eye, incl)
        br = _row_of(bc, eye)
        igr = _row_of(ig, eye)
        d_log = jnp.where(incl, bc - br + igr, NEG)
        d_max = jnp.max(d_log, axis=1, keepdims=True)
        b_last = [bc[(i + 1) * rows - 1:(i + 1) * rows, :] for i in range(gsz)]
        e_log = _per_row(b_last, rows) - bc + ig
        e_max = [jnp.max(e_log[i * rows:(i + 1) * rows], axis=0, keepdims=True) for i in range(gsz)]
        qk = _dot_nt(q, k)

        c_old = [c_ref[s, h] for s, h in group]
        n_old = [n_ref[s, h:h + 1, :] for s, h in group]
        m_old = [m_all[s, :, h:h + 1] for s, h in group]
        inter = bc + _per_row(m_old, rows)
        mt = jnp.maximum(inter, d_max)
        wi = jnp.exp(inter - mt)
        p = jnp.where(incl, jnp.exp(d_log - mt), 0.0) * qk
        qc = _dot_nt(_block_diag(q, rblk, gsz), jnp.concatenate(c_old, axis=1))
        num = wi * qc + _dot(p, v)
        n_rows = jnp.concatenate([jnp.broadcast_to(nv, (rows, DH)) for nv in n_old], axis=0)
        den = wi * jnp.sum(q * n_rows, axis=-1, keepdims=True) + jnp.sum(p, axis=-1, keepdims=True)
        hv = num / jnp.maximum(jnp.abs(den), jnp.exp(-mt))

        m_new = [jnp.maximum(b_last[i] + m_old[i], e_max[i]) for i in range(gsz)]
        fw = [jnp.exp(b_last[i] + m_old[i] - m_new[i]) for i in range(gsz)]
        sw = jnp.exp(e_log - _per_row(m_new, rows))
        c_upd = _dot_tn(sw * v, _block_diag(k, rblk, gsz))
        swk = sw * k
        for i, (s, h) in enumerate(group):
            c_ref[s, h] = fw[i] * c_old[i] + c_upd[:, i * DH:(i + 1) * DH]
            n_ref[s, h:h + 1, :] = fw[i] * n_old[i] + jnp.sum(swk[i * rows:(i + 1) * rows], axis=0, keepdims=True)
            m_ref[s, :, h:h + 1] = m_new[i]

        og = stack(lambda s, h: load_o(s, slice(h * DH, (h + 1) * DH)))
        hg = hv * jax.nn.sigmoid(og)
        for i, (s, h) in enumerate(group):
            h_ref[pl.ds(s * valid, valid), h * DH:(h + 1) * DH] = hg[i * rows:i * rows + valid]


def _mlstm(big, gates, c0, n0, m0, *, nb, nc, nseq, rows, valid, gsz):
    kern = functools.partial(_mlstm_kernel, nseq=nseq, rows=rows, valid=valid, gsz=gsz)
    br = nseq * valid
    scratch = []
    if valid < rows:
        scratch = [pltpu.VMEM((nseq, rows, QKV), F32), pltpu.VMEM((nseq, rows, QK), F32),
                   pltpu.VMEM((nseq, rows, N_GATES), F32)]
    return pl.pallas_call(
        kern,
        grid=(nb // nseq, nc),
        in_specs=[
            pl.BlockSpec((br, QKV), lambda b, c: (b * nc + c, 1)),
            pl.BlockSpec((br, QK), lambda b, c: (b * nc + c, 2 * QKV // QK + 1)),
            pl.BlockSpec((br, N_GATES), lambda b, c: (b * nc + c, 0)),
            pl.BlockSpec((nseq, HEADS, DH, DH), lambda b, c: (b, 0, 0, 0)),
            pl.BlockSpec((nseq, HEADS, DH), lambda b, c: (b, 0, 0)),
            pl.BlockSpec((nseq, 1, HEADS), lambda b, c: (b, 0, 0)),
        ],
        out_specs=[
            pl.BlockSpec((br, QK), lambda b, c: (b * nc + c, 0)),
            pl.BlockSpec((nseq, HEADS, DH, DH), lambda b, c: (b, 0, 0, 0)),
            pl.BlockSpec((nseq, HEADS, DH), lambda b, c: (b, 0, 0)),
            pl.BlockSpec((nseq, 1, HEADS), lambda b, c: (b, 0, 0)),
        ],
        out_shape=[jax.ShapeDtypeStruct((nb * nc * valid, QK), F32),
                   jax.ShapeDtypeStruct((nb, HEADS, DH, DH), F32),
                   jax.ShapeDtypeStruct((nb, HEADS, DH), F32),
                   jax.ShapeDtypeStruct((nb, 1, HEADS), F32)],
        scratch_shapes=scratch,
        compiler_params=_params("parallel", "arbitrary"),
        name="mlstm",
    )(big, big, gates, c0, n0, m0)


def _merge_kernel(go_ref, mh_ref, ga_ref, gb_ref, x_ref, wa_ref, wb_ref, wo_ref, x1_ref, mix_scr, *, nt, tn):
    j = pl.program_id(1)

    @pl.when(j < nt)
    def _():
        ya = jnp.dot(go_ref[...].astype(BF16), wa_ref[...], preferred_element_type=F32)
        yb = jnp.dot(mh_ref[...].astype(BF16), wb_ref[...], preferred_element_type=F32)
        mixed = jax.nn.sigmoid(ga_ref[...]) * ya + jax.nn.sigmoid(gb_ref[...]) * yb
        mixed = mixed.astype(BF16)
        for t in range(nt):
            @pl.when(j == t)
            def _():
                mix_scr[:, t * tn:(t + 1) * tn] = mixed

    @pl.when(j >= nt)
    def _():
        x1_ref[...] = x_ref[...] + jnp.dot(mix_scr[...], wo_ref[...], preferred_element_type=F32)


def _merge(go, mh, big, x, wa, wb, wo, *, tm, tn):
    m = x.shape[0]
    nt = D_MODEL // tn
    ga_blk = (2 * QKV + 2 * QK) // tn
    mix_j = lambda j: jnp.minimum(j, nt - 1)
    out_j = lambda j: jnp.maximum(j - nt, 0)
    return pl.pallas_call(
        functools.partial(_merge_kernel, nt=nt, tn=tn),
        grid=(m // tm, 2 * nt),
        in_specs=[
            pl.BlockSpec((tm, QK), lambda i, j: (i, 0)),
            pl.BlockSpec((tm, QK), lambda i, j: (i, 0)),
            pl.BlockSpec((tm, tn), lambda i, j: (i, ga_blk + mix_j(j))),
            pl.BlockSpec((tm, tn), lambda i, j: (i, ga_blk + nt + mix_j(j))),
            pl.BlockSpec((tm, tn), lambda i, j: (i, out_j(j))),
            pl.BlockSpec((QK, tn), lambda i, j: (0, mix_j(j))),
            pl.BlockSpec((QK, tn), lambda i, j: (0, mix_j(j))),
            pl.BlockSpec((D_MODEL, tn), lambda i, j: (0, out_j(j))),
        ],
        out_specs=pl.BlockSpec((tm, tn), lambda i, j: (i, out_j(j))),
        out_shape=jax.ShapeDtypeStruct((m, D_MODEL), F32),
        scratch_shapes=[pltpu.VMEM((tm, D_MODEL), BF16)],
        compiler_params=_params("parallel", "arbitrary"),
        name="merge",
    )(go, mh, big, big, x, wa, wb, wo)


def _ffn_kernel(*refs, tm, tf, seq_tiles, with_state, srows):
    keep = FFN_CONV - 1
    if with_state:
        (x1_ref, nw_ref, wg_ref, wv_ref, cwg_ref, cwv_ref, wd_ref, fnw_ref) = refs[:8]
        stg_refs, stv_refs = refs[8:8 + keep], refs[8 + keep:8 + 2 * keep]
        y_ref = refs[8 + 2 * keep]
        newg_refs, newv_refs = refs[9 + 2 * keep:9 + 3 * keep], refs[9 + 3 * keep:9 + 4 * keep]
        h2_scr, hist_g, hist_v, acc_scr, st_g, st_v, cp_g, cp_v = refs[9 + 4 * keep:]
        nsq = tm // srows
        nck = tf // LANES
    else:
        (x1_ref, halo_ref, nw_ref, wg_ref, wv_ref, cwg_ref, cwv_ref, wd_ref, fnw_ref,
         y_ref, upg_ref, upv_ref, h2_scr, hist_g, hist_v, acc_scr) = refs
    i = pl.program_id(0)
    f = pl.program_id(1)
    pad = SUBLANES

    @pl.when(f == 0)
    def _():
        if with_state:
            h2_scr[pl.ds(0, pad), :] = jnp.zeros((pad, D_MODEL), BF16)
        else:
            live = (i % seq_tiles != 0).astype(F32)
            h2_scr[pl.ds(0, pad), :] = (_rms(halo_ref[...], nw_ref[...]) * live).astype(BF16)
        h2_scr[pl.ds(pad, tm), :] = _rms(x1_ref[...], nw_ref[...]).astype(BF16)
        acc_scr[...] = jnp.zeros_like(acc_scr)

    h2 = h2_scr[...]
    hist_g[...] = jnp.dot(h2, wg_ref[...], preferred_element_type=F32)
    hist_v[...] = jnp.dot(h2, wv_ref[...], preferred_element_type=F32)

    if with_state:
        for st, cp, st_refs, hist, new_refs in ((st_g, cp_g, stg_refs, hist_g, newg_refs),
                                                (st_v, cp_v, stv_refs, hist_v, newv_refs)):
            st[...] = jnp.zeros_like(st)
            for c in range(nck):
                lanes = slice(c * LANES, (c + 1) * LANES)
                cp[c] = hist[:, lanes]
                for r in range(keep):
                    st[c, pl.ds(r, nsq, stride=srows), :] = st_refs[r][:, lanes]
                    new_refs[r][:, lanes] = cp[c, pl.ds(pad + srows - keep + r, nsq, stride=srows), :]
        rmod = lax.broadcasted_iota(jnp.int32, (tm, 1), 0) % srows
        slab = lambda st, off: jnp.concatenate([st[c, pl.ds(off, tm), :] for c in range(nck)], axis=1)
    else:
        upg_ref[...] = hist_g[pl.ds(tm, pad), :]
        upv_ref[...] = hist_v[pl.ds(tm, pad), :]

    def conv(hist, cw_ref, st):
        prev2 = hist[pl.ds(pad - 2, tm), :]
        prev1 = hist[pl.ds(pad - 1, tm), :]
        if with_state:
            prev2 = jnp.where(rmod < 2, slab(st, 0), prev2)
            prev1 = jnp.where(rmod < 1, slab(st, 1), prev1)
        return (prev2 * cw_ref[0:1, :] + prev1 * cw_ref[1:2, :]) + hist[pl.ds(pad, tm), :] * cw_ref[2:3, :]

    ug = conv(hist_g, cwg_ref, st_g if with_state else None)
    uv = conv(hist_v, cwv_ref, st_v if with_state else None)
    act = (ug * jax.nn.sigmoid(ug) * uv).astype(BF16)
    acc_scr[...] += jnp.dot(act, wd_ref[...], preferred_element_type=F32)

    @pl.when(f == pl.num_programs(1) - 1)
    def _():
        y_ref[...] = _rms(x1_ref[...] + acc_scr[...], fnw_ref[...])


def _ffn(x1, nw, wup, cw, wd, fnw, state=None, *, tm, tf, seq_tiles, srows=None):
    m = x1.shape[0]
    nf = D_FF // tf
    keep = FFN_CONV - 1
    with_state = state is not None
    kern = functools.partial(_ffn_kernel, tm=tm, tf=tf, seq_tiles=seq_tiles, with_state=with_state, srows=srows)
    in_specs = [pl.BlockSpec((tm, D_MODEL), lambda i, f: (i, 0))]
    args = [x1]
    if not with_state:
        in_specs.append(pl.BlockSpec((SUBLANES, D_MODEL),
                                     lambda i, f: (jnp.maximum(i * (tm // SUBLANES) - 1, 0), 0)))
        args.append(x1)
    in_specs += [
        pl.BlockSpec((1, D_MODEL), lambda i, f: (0, 0)),
        pl.BlockSpec((D_MODEL, tf), lambda i, f: (0, f)),
        pl.BlockSpec((D_MODEL, tf), lambda i, f: (0, f + nf)),
        pl.BlockSpec((FFN_CONV, tf), lambda i, f: (0, f)),
        pl.BlockSpec((FFN_CONV, tf), lambda i, f: (0, f + nf)),
        pl.BlockSpec((tf, D_MODEL), lambda i, f: (f, 0)),
        pl.BlockSpec((1, D_MODEL), lambda i, f: (0, 0)),
    ]
    args += [nw, wup, wup, cw, cw, wd, fnw]
    scratch = [pltpu.VMEM((SUBLANES + tm, D_MODEL), BF16),
               pltpu.VMEM((SUBLANES + tm, tf), F32),
               pltpu.VMEM((SUBLANES + tm, tf), F32),
               pltpu.VMEM((tm, D_MODEL), F32)]
    n_tiles = m // tm
    out_specs = [pl.BlockSpec((tm, D_MODEL), lambda i, f: (i, 0))]
    out_shape = [jax.ShapeDtypeStruct((m, D_MODEL), F32)]
    if with_state:
        nsq = tm // srows
        in_specs += [pl.BlockSpec((nsq, tf), lambda i, f: (i, f))] * keep
        in_specs += [pl.BlockSpec((nsq, tf), lambda i, f: (i, f + nf))] * keep
        args += list(state) * 2
        scratch += [pltpu.VMEM((tf // LANES, tm + SUBLANES, LANES), F32)] * 4
        out_specs += [pl.BlockSpec((nsq, tf), lambda i, f: (i, f))] * (2 * keep)
        out_shape += [jax.ShapeDtypeStruct((m // srows, D_FF), F32)] * (2 * keep)
    else:
        out_specs += [pl.BlockSpec((SUBLANES, tf), lambda i, f: (i, f))] * 2
        out_shape += [jax.ShapeDtypeStruct((n_tiles * SUBLANES, D_FF), F32)] * 2
    outs = pl.pallas_call(
        kern,
        grid=(n_tiles, nf),
        in_specs=in_specs,
        out_specs=out_specs,
        out_shape=out_shape,
        scratch_shapes=scratch,
        compiler_params=_params("parallel", "arbitrary"),
        name="ffn",
    )(*args)
    return outs[0], outs[1:]


def _layer(x, nb, nc, nseq, rows, valid, gdn_group, ml_group, states, w, *, tm_in, tm_merge, tm_ffn, seq_tiles,
           ffn_state):
    conv_state, s0, c0, n0, m0 = states
    big, gates = _inproj(x, w["norm_mix"], w["w_big"], w["w_small"], w["gate_bias"], w["a_log"],
                         tm=tm_in, tn=1024)
    go, s_new, conv_new = _gdn(big, gates, conv_state, s0, w["gdn_conv_w"], w["gdn_norm"],
                               nb=nb, nc=nc, nseq=nseq, rows=rows, valid=valid, gsz=gdn_group[0], cps=gdn_group[1])
    mh, c_new, n_new, m_new = _mlstm(big, gates, c0, n0, m0,
                                     nb=nb, nc=nc, nseq=nseq, rows=rows, valid=valid, gsz=ml_group)
    x1 = _merge(go, mh, big, x, w["w_a"], w["w_b"], w["w_out"], tm=tm_merge, tn=512)
    y, ffn_rows = _ffn(x1, w["norm_ffn"], w["w_up"], w["ffn_conv_w"], w["w_down"], w["norm_final"],
                       ffn_state, tm=tm_ffn, tf=512, seq_tiles=seq_tiles, srows=valid)
    return y, (conv_new, s_new, c_new, n_new, m_new.reshape(nb, HEADS)), ffn_rows


def kernel(x_prompt, x_sample, state_gdn_conv, state_gdn_S, state_ml_C, state_ml_n, state_ml_m, state_ffn_conv,
           norm_mix_w, w_in, gdn_conv_w, gdn_A_log, gdn_dt_bias, gdn_norm_w, w_branch_a, ml_b_i, ml_b_f,
           w_branch_b, w_out, norm_ffn_w, w_up, ffn_conv_w, w_down, norm_final_w):
    assert w_in.shape[0] == 1, "single-layer step"
    bp, tp, _ = x_prompt.shape
    bs, ts, _ = x_sample.shape
    assert FFN_CONV - 1 <= ts <= SUBLANES and GDN_CONV - 1 <= ts and tp % CHUNK == 0
    assert w_in.shape[1:] == (D_MODEL, IN_COLS)
    keep = FFN_CONV - 1

    w_big, w_small = _repack(w_in[0].T, tn=512)
    zeros8 = jnp.zeros((HEADS,), F32)
    w = {
        "norm_mix": norm_mix_w[0][None, :],
        "w_big": w_big,
        "w_small": w_small,
        "gate_bias": jnp.concatenate([gdn_dt_bias[0], zeros8, ml_b_i[0], ml_b_f[0]])[None, :],
        "a_log": jnp.concatenate([gdn_A_log[0], zeros8, zeros8, zeros8])[None, :],
        "gdn_conv_w": gdn_conv_w[0],
        "gdn_norm": gdn_norm_w[0][None, :],
        "w_a": w_branch_a[0].astype(BF16),
        "w_b": w_branch_b[0].astype(BF16),
        "w_out": w_out[0].astype(BF16),
        "norm_ffn": norm_ffn_w[0][None, :],
        "w_up": w_up[0].astype(BF16),
        "ffn_conv_w": ffn_conv_w[0],
        "w_down": w_down[0].astype(BF16),
        "norm_final": norm_final_w[None, :],
    }

    xp = x_prompt.reshape(bp * tp, D_MODEL)
    p_states = (jnp.zeros((bp, GDN_CONV - 1, QKV), F32), jnp.zeros((bp, HEADS, DH, DH), F32),
                jnp.zeros((bp, HEADS, DH, DH), F32), jnp.zeros((bp, HEADS, DH), F32),
                jnp.zeros((bp, 1, HEADS), F32))
    tm_ffn = 512
    yp, p_new, up_p = _layer(
        xp, bp, tp // CHUNK, 1, CHUNK, CHUNK, (4, 4), HEADS, p_states, w,
        tm_in=1024, tm_merge=1024, tm_ffn=tm_ffn, seq_tiles=tp // tm_ffn, ffn_state=None)
    p_ffn_conv = jnp.concatenate(
        [u.reshape(bp, tp // tm_ffn, SUBLANES, D_FF)[:, -1, SUBLANES - keep:, :] for u in up_p], axis=-1)

    xs = x_sample.reshape(bs * ts, D_MODEL)
    s_states = (state_gdn_conv[0], state_gdn_S[0], state_ml_C[0], state_ml_n[0],
                state_ml_m[0].reshape(bs, 1, HEADS))
    ys, s_new, new_rows = _layer(
        xs, bs, 1, 4, SUBLANES, ts, (2 * HEADS, 1), 2 * HEADS, s_states, w,
        tm_in=bs * ts, tm_merge=bs * ts, tm_ffn=bs * ts, seq_tiles=1,
        ffn_state=[state_ffn_conv[0, :, r, :] for r in range(keep)])
    s_ffn_conv = jnp.stack([jnp.concatenate([new_rows[r], new_rows[keep + r]], axis=-1) for r in range(keep)], axis=1)

    lead = lambda t: tuple(a[None] for a in t)
    return (yp.reshape(bp, tp, D_MODEL), ys.reshape(bs, ts, D_MODEL),
            *lead(p_new), p_ffn_conv[None], *lead(s_new), s_ffn_conv[None])
```

```python
import functools
import math

import jax
import jax.numpy as jnp
from jax import lax
from jax.experimental import pallas as pl
from jax.experimental.pallas import tpu as pltpu

F32 = jnp.float32
BF16 = jnp.bfloat16

D_MODEL = 2048
HEADS = 8
DH = 128
QK = HEADS * DH
QKV = 3 * QK
D_FF = 5632
GDN_CONV = 4
FFN_CONV = 3
EPS = 1e-6
CHUNK = 64
N_GATES = 4 * HEADS
SUBLANES = 8
NEG = -1e30

BIG_COLS = 2 * QKV + 2 * QK + 2 * D_MODEL

VMEM_LIMIT = 56 * 1024 * 1024


def _params(*sem):
    return pltpu.CompilerParams(dimension_semantics=sem, vmem_limit_bytes=VMEM_LIMIT)


def _dot(a, b):
    return jnp.dot(a.astype(BF16), b.astype(BF16), preferred_element_type=F32)


def _dot_nt(a, b):
    return lax.dot_general(a.astype(BF16), b.astype(BF16), (((1,), (1,)), ((), ())),
                           preferred_element_type=F32)


def _dot_tn(a, b):
    return lax.dot_general(a.astype(BF16), b.astype(BF16), (((0,), (0,)), ((), ())),
                           preferred_element_type=F32)


def _softplus(x):
    return jnp.maximum(x, 0.0) + jnp.log1p(jnp.exp(-jnp.abs(x)))


def _rms(x, w):
    return x * lax.rsqrt(jnp.mean(x * x, axis=-1, keepdims=True) + EPS) * w


IN_SECTIONS = (("gqkv", QKV), ("gz", QK), ("ga", HEADS), ("gb", HEADS), ("mqkv", QKV),
               ("mi", HEADS), ("mf", HEADS), ("mo", QK), ("gA", D_MODEL), ("gB", D_MODEL))
BIG_ORDER = ("gqkv", "mqkv", "gz", "mo", "gA", "gB")
IN_COLS = sum(size for _, size in IN_SECTIONS)


def _in_start(name):
    off = 0
    for n, size in IN_SECTIONS:
        if n == name:
            return off
        off += size
    raise KeyError(name)


def _big_tiles(tn):
    starts = []
    for name in BIG_ORDER:
        size = dict(IN_SECTIONS)[name]
        assert size % tn == 0
        starts += [_in_start(name) + t * tn for t in range(size // tn)]
    return starts


def _tile_lookup(j, values):
    out = jnp.int32(values[-1])
    for t in range(len(values) - 2, -1, -1):
        out = jnp.where(j <= t, jnp.int32(values[t]), out)
    return out


LANES = 128
SMALL_ORDER = ("ga", "gb", "mi", "mf")


def _repack_kernel(*refs):
    wt_ref, g_refs, out_ref, small_ref = refs[0], refs[1:1 + len(SMALL_ORDER)], refs[-2], refs[-1]

    @pl.when(pl.program_id(0) == 0)
    def _():
        cols = [g_ref[...].T[:, _in_start(n) % SUBLANES:_in_start(n) % SUBLANES + HEADS]
                for n, g_ref in zip(SMALL_ORDER, g_refs)]
        small_ref[...] = jnp.concatenate(cols, axis=1).astype(BF16)

    out_ref[...] = wt_ref[...].T.astype(BF16)


def _repack(w_in_t, *, tn):
    starts = _big_tiles(tn)
    assert all(s % SUBLANES == 0 for s in starts)
    rows8 = [s // SUBLANES for s in starts]
    return pl.pallas_call(
        _repack_kernel,
        grid=(BIG_COLS // tn,),
        in_specs=[pl.BlockSpec((pl.Element(tn), pl.Element(D_MODEL)),
                               lambda j: (_tile_lookup(j, rows8) * SUBLANES, 0))]
        + [pl.BlockSpec((SUBLANES, D_MODEL), lambda j, b=_in_start(n) // SUBLANES: (b, 0)) for n in SMALL_ORDER],
        out_specs=[pl.BlockSpec((D_MODEL, tn), lambda j: (0, j)),
                   pl.BlockSpec((D_MODEL, N_GATES), lambda j: (0, 0))],
        out_shape=[jax.ShapeDtypeStruct((D_MODEL, BIG_COLS), BF16),
                   jax.ShapeDtypeStruct((D_MODEL, N_GATES), BF16)],
        compiler_params=_params("arbitrary"),
        name="repack",
    )(*([w_in_t] * (1 + len(SMALL_ORDER))))


def _inproj_kernel(x_ref, nw_ref, wbig_ref, wsm_ref, gbias_ref, alog_ref, big_ref, gates_ref, h_scr):
    @pl.when(pl.program_id(1) == 0)
    def _():
        hb = _rms(x_ref[...], nw_ref[...]).astype(BF16)
        h_scr[...] = hb
        raw = jnp.dot(hb, wsm_ref[...], preferred_element_type=F32)
        lane = lax.broadcasted_iota(jnp.int32, raw.shape, 1)
        z = raw + gbias_ref[...]
        g = -jnp.exp(alog_ref[...]) * _softplus(z)
        beta = jax.nn.sigmoid(raw)
        lf = -_softplus(-z)
        gates_ref[...] = jnp.where(lane < HEADS, g,
                                   jnp.where(lane < 2 * HEADS, beta,
                                             jnp.where(lane < 3 * HEADS, z, lf)))

    big_ref[...] = jnp.dot(h_scr[...], wbig_ref[...], preferred_element_type=F32)


def _inproj(x, nw, wbig, wsm, gbias, alog, *, tm, tn):
    m = x.shape[0]
    return pl.pallas_call(
        _inproj_kernel,
        grid=(m // tm, BIG_COLS // tn),
        in_specs=[
            pl.BlockSpec((tm, D_MODEL), lambda i, j: (i, 0)),
            pl.BlockSpec((1, D_MODEL), lambda i, j: (0, 0)),
            pl.BlockSpec((D_MODEL, tn), lambda i, j: (0, j)),
            pl.BlockSpec((D_MODEL, N_GATES), lambda i, j: (0, 0)),
            pl.BlockSpec((1, N_GATES), lambda i, j: (0, 0)),
            pl.BlockSpec((1, N_GATES), lambda i, j: (0, 0)),
        ],
        out_specs=[
            pl.BlockSpec((tm, tn), lambda i, j: (i, j)),
            pl.BlockSpec((tm, N_GATES), lambda i, j: (i, 0)),
        ],
        out_shape=[jax.ShapeDtypeStruct((m, BIG_COLS), F32),
                   jax.ShapeDtypeStruct((m, N_GATES), F32)],
        scratch_shapes=[pltpu.VMEM((tm, D_MODEL), BF16)],
        compiler_params=_params("parallel", "arbitrary"),
        name="inproj",
    )(x, nw, wbig, wsm, gbias, alog)


def _group_masks(n, rows):
    r = n * rows
    shift = rows.bit_length() - 1
    ri = lax.broadcasted_iota(jnp.int32, (r, r), 0)
    ci = lax.broadcasted_iota(jnp.int32, (r, r), 1)
    same = lax.shift_right_logical(ri, shift) == lax.shift_right_logical(ci, shift)
    return ri == ci, same & (ci <= ri), same & (ci < ri)


def _row_block(nrows, rows, n):
    shift = rows.bit_length() - 1
    ri = lax.broadcasted_iota(jnp.int32, (nrows, 1), 0)
    return lax.shift_right_logical(ri, shift) & (n - 1)


def _row_of(col, eye):
    return jnp.sum(jnp.where(eye, col, 0.0), axis=0, keepdims=True)


def _cumsum_col(col, eye, incl):
    return jnp.sum(jnp.where(incl, _row_of(col, eye), 0.0), axis=1, keepdims=True)


def _per_row(vals, rows):
    return jnp.concatenate([jnp.broadcast_to(v, (rows, 1)) for v in vals], axis=0)


def _block_diag(x, rblk, n):
    return jnp.concatenate([jnp.where(rblk == c, x, 0.0) for c in range(n)], axis=1)


def _conv_silu(hist_ref, s, cw_ref, col0, rows):
    cols = slice(col0, col0 + DH)
    base = SUBLANES - GDN_CONV + 1
    acc = hist_ref[s, pl.ds(base, rows), cols] * cw_ref[0:1, cols]
    for j in range(1, GDN_CONV):
        acc = acc + hist_ref[s, pl.ds(base + j, rows), cols] * cw_ref[j:j + 1, cols]
    return acc * jax.nn.sigmoid(acc)


def _seq_loader(ref, scr, nseq, valid, rows, base=0):
    if valid == rows:
        return lambda s, cols: ref[pl.ds(base + s * rows, rows), cols]
    assert base == 0
    for s in range(nseq):
        scr[s, pl.ds(0, valid), :] = ref[pl.ds(s * valid, valid), :]
        scr[s, pl.ds(valid, rows - valid), :] = jnp.zeros((rows - valid, scr.shape[-1]), F32)
    return lambda s, cols: scr[s, :, cols]


def _problems(nseq, gsz):
    probs = [(s, h) for s in range(nseq) for h in range(HEADS)]
    return [probs[i:i + gsz] for i in range(0, len(probs), gsz)]


def _gdn_kernel(*refs, nseq, rows, valid, levels, gsz, cps, ncast):
    qkv_ref, z_ref, gates_ref, cst_ref, s0_ref, cw_ref, nw_ref = refs[:7]
    cast_in = refs[7:7 + ncast]
    o_ref, s_ref, cst_out_ref = refs[7 + ncast:10 + ncast]
    cast_out = refs[10 + ncast:10 + 2 * ncast]
    hist, pads = refs[10 + 2 * ncast], refs[11 + 2 * ncast:]
    keep = GDN_CONV - 1

    for src, dst in zip(cast_in, cast_out):
        dst[...] = src[...].astype(BF16)

    @pl.when(pl.program_id(1) == 0)
    def _():
        hist[:, pl.ds(SUBLANES - keep, keep), :] = cst_ref[...]
        s_ref[...] = s0_ref[...]

    r = gsz * rows
    eye, incl, strict = _group_masks(gsz, rows)
    rvalid = (lax.broadcasted_iota(jnp.int32, (r, 1), 0) & (rows - 1)) < valid
    masks = (eye, incl, strict, eye.astype(F32), rvalid, _row_block(r, rows, gsz))
    for ck in range(cps):
        _gdn_chunk(ck * nseq * valid, qkv_ref, z_ref, gates_ref, cw_ref, nw_ref, o_ref, s_ref, hist, pads, masks,
                   nseq=nseq, rows=rows, valid=valid, levels=levels, gsz=gsz)
    cst_out_ref[...] = hist[:, pl.ds(SUBLANES - keep, keep), :]


def _gdn_chunk(base, qkv_ref, z_ref, gates_ref, cw_ref, nw_ref, o_ref, s_ref, hist, pads, masks,
               *, nseq, rows, valid, levels, gsz):
    eye, incl, strict, eye_f, rvalid, rblk = masks
    keep = GDN_CONV - 1
    r = gsz * rows
    for s in range(nseq):
        hist[s, pl.ds(SUBLANES, valid), :] = qkv_ref[pl.ds(base + s * valid, valid), :]
        if valid < rows:
            hist[s, pl.ds(SUBLANES + valid, rows - valid), :] = jnp.zeros((rows - valid, QKV), F32)
    zpad, gpad = pads if pads else (None, None)
    load_z = _seq_loader(z_ref, zpad, nseq, valid, rows, base)
    load_g = _seq_loader(gates_ref, gpad, nseq, valid, rows, base)

    for group in _problems(nseq, gsz):
        def stack(fn):
            return jnp.concatenate([fn(s, h) for s, h in group], axis=0)

        q = stack(lambda s, h: _conv_silu(hist, s, cw_ref, h * DH, rows))
        k = stack(lambda s, h: _conv_silu(hist, s, cw_ref, QK + h * DH, rows))
        v = stack(lambda s, h: _conv_silu(hist, s, cw_ref, 2 * QK + h * DH, rows))
        q = q * lax.rsqrt(jnp.sum(q * q, axis=-1, keepdims=True) + EPS) * (DH ** -0.5)
        k = k * lax.rsqrt(jnp.sum(k * k, axis=-1, keepdims=True) + EPS)
        g = jnp.where(rvalid, stack(lambda s, h: load_g(s, slice(h, h + 1))), 0.0)
        beta = jnp.where(rvalid, stack(lambda s, h: load_g(s, slice(HEADS + h, HEADS + h + 1))), 0.0)

        gc = _cumsum_col(g, eye, incl)
        gr = _row_of(gc, eye)
        decay = jnp.where(incl, jnp.exp(jnp.where(incl, gc - gr, 0.0)), 0.0)
        a = jnp.where(strict, beta * _dot_nt(k, k) * decay, 0.0)
        qk = _dot_nt(q, k) * decay
        bk = -a
        t = eye_f + bk
        if levels >= 2:
            bk = _dot(bk, bk)
            for _ in range(2, levels):
                t, bk = t + _dot(t, bk), _dot(bk, bk)
            t = t + _dot(t, bk)
        eg = jnp.exp(gc)
        uw = _dot(t, jnp.concatenate([v * beta, k * (beta * eg)], axis=1))
        u, w = uw[:, :DH], uw[:, DH:]

        s_old = [s_ref[s, h] for s, h in group]
        s_stack = jnp.concatenate(s_old, axis=0)
        v_new = u - _dot(_block_diag(w, rblk, gsz), s_stack)
        o = _dot(_block_diag(q * eg, rblk, gsz), s_stack) + _dot(qk, v_new)
        g_last = [gc[(i + 1) * rows - 1:(i + 1) * rows, :] for i in range(gsz)]
        kd = k * jnp.exp(_per_row(g_last, rows) - gc)
        if rows >= CHUNK:
            s_upd = [_dot_tn(kd[i * rows:(i + 1) * rows], v_new[i * rows:(i + 1) * rows]) for i in range(gsz)]
        else:
            s_all = _dot_tn(_block_diag(kd, rblk, gsz), v_new)
            s_upd = [s_all[i * DH:(i + 1) * DH] for i in range(gsz)]
        for i, (s, h) in enumerate(group):
            s_ref[s, h] = s_old[i] * jnp.exp(g_last[i]) + s_upd[i]

        zs = stack(lambda s, h: load_z(s, slice(h * DH, (h + 1) * DH)))
        on = _rms(o, nw_ref[...]) * (zs * jax.nn.sigmoid(zs))
        for i, (s, h) in enumerate(group):
            o_ref[pl.ds(base + s * valid, valid), h * DH:(h + 1) * DH] = on[i * rows:i * rows + valid]

    for s in range(nseq):
        hist[s, pl.ds(SUBLANES - keep, keep), :] = hist[s, pl.ds(SUBLANES + valid - keep, keep), :]


def _gdn(big, gates, conv_state, s0, conv_w, norm_w, casts=(), *, nb, nc, nseq, rows, valid, gsz, cps):
    assert nc % cps == 0 and (cps == 1 or nseq == 1)
    nc = nc // cps
    levels = max(1, math.ceil(math.log2(valid)))
    kern = functools.partial(_gdn_kernel, nseq=nseq, rows=rows, valid=valid, levels=levels, gsz=gsz, cps=cps,
                             ncast=len(casts))
    br = nseq * valid * cps
    scratch = [pltpu.VMEM((nseq, SUBLANES + rows, QKV), F32)]
    if valid < rows:
        scratch += [pltpu.VMEM((nseq, rows, QK), F32), pltpu.VMEM((nseq, rows, N_GATES), F32)]
    steps = nb // nseq * nc
    bf16_rows = 2 * SUBLANES
    assert all(a.shape[0] % (steps * bf16_rows) == 0 for a in casts)
    cast_specs = [pl.BlockSpec((a.shape[0] // steps, a.shape[1]), lambda b, c: (b * nc + c, 0)) for a in casts]
    return pl.pallas_call(
        kern,
        grid=(nb // nseq, nc),
        in_specs=[
            pl.BlockSpec((br, QKV), lambda b, c: (b * nc + c, 0)),
            pl.BlockSpec((br, QK), lambda b, c: (b * nc + c, 2 * QKV // QK)),
            pl.BlockSpec((br, N_GATES), lambda b, c: (b * nc + c, 0)),
            pl.BlockSpec((nseq, GDN_CONV - 1, QKV), lambda b, c: (b, 0, 0)),
            pl.BlockSpec((nseq, HEADS, DH, DH), lambda b, c: (b, 0, 0, 0)),
            pl.BlockSpec((GDN_CONV, QKV), lambda b, c: (0, 0)),
            pl.BlockSpec((1, DH), lambda b, c: (0, 0)),
            *cast_specs,
        ],
        out_specs=[
            pl.BlockSpec((br, QK), lambda b, c: (b * nc + c, 0)),
            pl.BlockSpec((nseq, HEADS, DH, DH), lambda b, c: (b, 0, 0, 0)),
            pl.BlockSpec((nseq, GDN_CONV - 1, QKV), lambda b, c: (b, 0, 0)),
            *cast_specs,
        ],
        out_shape=[jax.ShapeDtypeStruct((nb // nseq * nc * br, QK), F32),
                   jax.ShapeDtypeStruct((nb, HEADS, DH, DH), F32),
                   jax.ShapeDtypeStruct((nb, GDN_CONV - 1, QKV), F32),
                   *[jax.ShapeDtypeStruct(a.shape, BF16) for a in casts]],
        scratch_shapes=scratch,
        compiler_params=_params("parallel", "arbitrary"),
        name="gdn",
    )(big, big, gates, conv_state, s0, conv_w, norm_w, *casts)


def _mlstm_kernel(qkv_ref, og_ref, gates_ref, c0_ref, n0_ref, m0_ref, h_ref, c_ref, n_ref, m_ref, *pads,
                  nseq, rows, valid, gsz):
    @pl.when(pl.program_id(1) == 0)
    def _():
        c_ref[...] = c0_ref[...]
        n_ref[...] = n0_ref[...]
        m_ref[...] = m0_ref[...]

    r = gsz * rows
    eye, incl, _ = _group_masks(gsz, rows)
    rvalid = (lax.broadcasted_iota(jnp.int32, (r, 1), 0) & (rows - 1)) < valid
    rblk = _row_block(r, rows, gsz)
    xpad, opad, gpad = pads if pads else (None, None, None)
    load_x = _seq_loader(qkv_ref, xpad, nseq, valid, rows)
    load_o = _seq_loader(og_ref, opad, nseq, valid, rows)
    load_g = _seq_loader(gates_ref, gpad, nseq, valid, rows)
    m_all = m_ref[...]

    for group in _problems(nseq, gsz):
        def stack(fn):
            return jnp.concatenate([fn(s, h) for s, h in group], axis=0)

        q = stack(lambda s, h: load_x(s, slice(h * DH, (h + 1) * DH)))
        k = stack(lambda s, h: load_x(s, slice(QK + h * DH, QK + (h + 1) * DH))) * (DH ** -0.5)
        v = stack(lambda s, h: load_x(s, slice(2 * QK + h * DH, 2 * QK + (h + 1) * DH)))
        ig = jnp.where(rvalid, stack(lambda s, h: load_g(s, slice(2 * HEADS + h, 2 * HEADS + h + 1))), NEG)
        lf = jnp.where(rvalid, stack(lambda s, h: load_g(s, slice(3 * HEADS + h, 3 * HEADS + h + 1))), 0.0)

        bc = _cumsum_col(lf, eye, incl)
        br = _row_of(bc, eye)
        igr = _row_of(ig, eye)
        d_log = jnp.where(incl, bc - br + igr, NEG)
        d_max = jnp.max(d_log, axis=1, keepdims=True)
        b_last = [bc[(i + 1) * rows - 1:(i + 1) * rows, :] for i in range(gsz)]
        e_log = _per_row(b_last, rows) - bc + ig
        e_max = [jnp.max(e_log[i * rows:(i + 1) * rows], axis=0, keepdims=True) for i in range(gsz)]
        qk = _dot_nt(q, k)

        c_old = [c_ref[s, h] for s, h in group]
        n_old = [n_ref[s, h:h + 1, :] for s, h in group]
        m_old = [m_all[s, :, h:h + 1] for s, h in group]
        inter = bc + _per_row(m_old, rows)
        mt = jnp.maximum(inter, d_max)
        wi = jnp.exp(inter - mt)
        p = jnp.where(incl, jnp.exp(d_log - mt), 0.0) * qk
        qc = _dot_nt(_block_diag(q, rblk, gsz), jnp.concatenate(c_old, axis=1))
        num = wi * qc + _dot(p, v)
        n_rows = jnp.concatenate([jnp.broadcast_to(nv, (rows, DH)) for nv in n_old], axis=0)
        den = wi * jnp.sum(q * n_rows, axis=-1, keepdims=True) + jnp.sum(p, axis=-1, keepdims=True)
        hv = num / jnp.maximum(jnp.abs(den), jnp.exp(-mt))

        m_new = [jnp.maximum(b_last[i] + m_old[i], e_max[i]) for i in range(gsz)]
        fw = [jnp.exp(b_last[i] + m_old[i] - m_new[i]) for i in range(gsz)]
        sw = jnp.exp(e_log - _per_row(m_new, rows))
        c_upd = _dot_tn(sw * v, _block_diag(k, rblk, gsz))
        swk = sw * k
        for i, (s, h) in enumerate(group):
            c_ref[s, h] = fw[i] * c_old[i] + c_upd[:, i * DH:(i + 1) * DH]
            n_ref[s, h:h + 1, :] = fw[i] * n_old[i] + jnp.sum(swk[i * rows:(i + 1) * rows], axis=0, keepdims=True)
            m_ref[s, :, h:h + 1] = m_new[i]

        og = stack(lambda s, h: load_o(s, slice(h * DH, (h + 1) * DH)))
        hg = hv * jax.nn.sigmoid(og)
        for i, (s, h) in enumerate(group):
            h_ref[pl.ds(s * valid, valid), h * DH:(h + 1) * DH] = hg[i * rows:i * rows + valid]


def _mlstm(big, gates, c0, n0, m0, *, nb, nc, nseq, rows, valid, gsz):
    kern = functools.partial(_mlstm_kernel, nseq=nseq, rows=rows, valid=valid, gsz=gsz)
    br = nseq * valid
    scratch = []
    if valid < rows:
        scratch = [pltpu.VMEM((nseq, rows, QKV), F32), pltpu.VMEM((nseq, rows, QK), F32),
                   pltpu.VMEM((nseq, rows, N_GATES), F32)]
    return pl.pallas_call(
        kern,
        grid=(nb // nseq, nc),
        in_specs=[
            pl.BlockSpec((br, QKV), lambda b, c: (b * nc + c, 1)),
            pl.BlockSpec((br, QK), lambda b, c: (b * nc + c, 2 * QKV // QK + 1)),
            pl.BlockSpec((br, N_GATES), lambda b, c: (b * nc + c, 0)),
            pl.BlockSpec((nseq, HEADS, DH, DH), lambda b, c: (b, 0, 0, 0)),
            pl.BlockSpec((nseq, HEADS, DH), lambda b, c: (b, 0, 0)),
            pl.BlockSpec((nseq, 1, HEADS), lambda b, c: (b, 0, 0)),
        ],
        out_specs=[
            pl.BlockSpec((br, QK), lambda b, c: (b * nc + c, 0)),
            pl.BlockSpec((nseq, HEADS, DH, DH), lambda b, c: (b, 0, 0, 0)),
            pl.BlockSpec((nseq, HEADS, DH), lambda b, c: (b, 0, 0)),
            pl.BlockSpec((nseq, 1, HEADS), lambda b, c: (b, 0, 0)),
        ],
        out_shape=[jax.ShapeDtypeStruct((nb * nc * valid, QK), F32),
                   jax.ShapeDtypeStruct((nb, HEADS, DH, DH), F32),
                   jax.ShapeDtypeStruct((nb, HEADS, DH), F32),
                   jax.ShapeDtypeStruct((nb, 1, HEADS), F32)],
        scratch_shapes=scratch,
        compiler_params=_params("parallel", "arbitrary"),
        name="mlstm",
    )(big, big, gates, c0, n0, m0)


def _merge_kernel(go_ref, mh_ref, ga_ref, gb_ref, x_ref, wa_ref, wb_ref, wo_ref, x1_ref, mix_scr, *, nt, tn):
    j = pl.program_id(1)

    @pl.when(j < nt)
    def _():
        ya = jnp.dot(go_ref[...].astype(BF16), wa_ref[...], preferred_element_type=F32)
        yb = jnp.dot(mh_ref[...].astype(BF16), wb_ref[...], preferred_element_type=F32)
        mixed = jax.nn.sigmoid(ga_ref[...]) * ya + jax.nn.sigmoid(gb_ref[...]) * yb
        mixed = mixed.astype(BF16)
        for t in range(nt):
            @pl.when(j == t)
            def _():
                mix_scr[:, t * tn:(t + 1) * tn] = mixed

    @pl.when(j >= nt)
    def _():
        x1_ref[...] = x_ref[...] + jnp.dot(mix_scr[...], wo_ref[...], preferred_element_type=F32)


def _merge(go, mh, big, x, wa, wb, wo, *, tm, tn):
    m = x.shape[0]
    nt = D_MODEL // tn
    ga_blk = (2 * QKV + 2 * QK) // tn
    mix_j = lambda j: jnp.minimum(j, nt - 1)
    out_j = lambda j: jnp.maximum(j - nt, 0)
    return pl.pallas_call(
        functools.partial(_merge_kernel, nt=nt, tn=tn),
        grid=(m // tm, 2 * nt),
        in_specs=[
            pl.BlockSpec((tm, QK), lambda i, j: (i, 0)),
            pl.BlockSpec((tm, QK), lambda i, j: (i, 0)),
            pl.BlockSpec((tm, tn), lambda i, j: (i, ga_blk + mix_j(j))),
            pl.BlockSpec((tm, tn), lambda i, j: (i, ga_blk + nt + mix_j(j))),
            pl.BlockSpec((tm, tn), lambda i, j: (i, out_j(j))),
            pl.BlockSpec((QK, tn), lambda i, j: (0, mix_j(j))),
            pl.BlockSpec((QK, tn), lambda i, j: (0, mix_j(j))),
            pl.BlockSpec((D_MODEL, tn), lambda i, j: (0, out_j(j))),
        ],
        out_specs=pl.BlockSpec((tm, tn), lambda i, j: (i, out_j(j))),
        out_shape=jax.ShapeDtypeStruct((m, D_MODEL), F32),
        scratch_shapes=[pltpu.VMEM((tm, D_MODEL), BF16)],
        compiler_params=_params("parallel", "arbitrary"),
        name="merge",
    )(go, mh, big, big, x, wa, wb, wo)


def _ffn_kernel(*refs, tm, tf, seq_tiles, with_state, srows):
    keep = FFN_CONV - 1
    if with_state:
        (x1_ref, nw_ref, wg_ref, wv_ref, cwg_ref, cwv_ref, wd_ref, fnw_ref) = refs[:8]
        stg_refs, stv_refs = refs[8:8 + keep], refs[8 + keep:8 + 2 * keep]
        y_ref = refs[8 + 2 * keep]
        newg_refs, newv_refs = refs[9 + 2 * keep:9 + 3 * keep], refs[9 + 3 * keep:9 + 4 * keep]
        h2_scr, hist_g, hist_v, acc_scr, st_g, st_v, cp_g, cp_v = refs[9 + 4 * keep:]
        nsq = tm // srows
        nck = tf // LANES
    else:
        (x1_ref, halo_ref, nw_ref, wg_ref, wv_ref, cwg_ref, cwv_ref, wd_ref, fnw_ref,
         y_ref, upg_ref, upv_ref, h2_scr, hist_g, hist_v, acc_scr) = refs
    i = pl.program_id(0)
    f = pl.program_id(1)
    pad = SUBLANES

    @pl.when(f == 0)
    def _():
        if with_state:
            h2_scr[pl.ds(0, pad), :] = jnp.zeros((pad, D_MODEL), BF16)
        else:
            live = (i % seq_tiles != 0).astype(F32)
            h2_scr[pl.ds(0, pad), :] = (_rms(halo_ref[...], nw_ref[...]) * live).astype(BF16)
        h2_scr[pl.ds(pad, tm), :] = _rms(x1_ref[...], nw_ref[...]).astype(BF16)
        acc_scr[...] = jnp.zeros_like(acc_scr)

    h2 = h2_scr[...]
    hist_g[...] = jnp.dot(h2, wg_ref[...], preferred_element_type=F32)
    hist_v[...] = jnp.dot(h2, wv_ref[...], preferred_element_type=F32)

    if with_state:
        for st, cp, st_refs, hist, new_refs in ((st_g, cp_g, stg_refs, hist_g, newg_refs),
                                                (st_v, cp_v, stv_refs, hist_v, newv_refs)):
            st[...] = jnp.zeros_like(st)
            for c in range(nck):
                lanes = slice(c * LANES, (c + 1) * LANES)
                cp[c] = hist[:, lanes]
                for r in range(keep):
                    st[c, pl.ds(r, nsq, stride=srows), :] = st_refs[r][:, lanes]
                    new_refs[r][:, lanes] = cp[c, pl.ds(pad + srows - keep + r, nsq, stride=srows), :]
        rmod = lax.broadcasted_iota(jnp.int32, (tm, 1), 0) % srows
        slab = lambda st, off: jnp.concatenate([st[c, pl.ds(off, tm), :] for c in range(nck)], axis=1)
    else:
        upg_ref[...] = hist_g[pl.ds(tm, pad), :]
        upv_ref[...] = hist_v[pl.ds(tm, pad), :]

    def conv(hist, cw_ref, st):
        prev2 = hist[pl.ds(pad - 2, tm), :]
        prev1 = hist[pl.ds(pad - 1, tm), :]
        if with_state:
            prev2 = jnp.where(rmod < 2, slab(st, 0), prev2)
            prev1 = jnp.where(rmod < 1, slab(st, 1), prev1)
        return (prev2 * cw_ref[0:1, :] + prev1 * cw_ref[1:2, :]) + hist[pl.ds(pad, tm), :] * cw_ref[2:3, :]

    ug = conv(hist_g, cwg_ref, st_g if with_state else None)
    uv = conv(hist_v, cwv_ref, st_v if with_state else None)
    act = (ug * jax.nn.sigmoid(ug) * uv).astype(BF16)
    acc_scr[...] += jnp.dot(act, wd_ref[...], preferred_element_type=F32)

    @pl.when(f == pl.num_programs(1) - 1)
    def _():
        y_ref[...] = _rms(x1_ref[...] + acc_scr[...], fnw_ref[...])


def _ffn(x1, nw, wup, cw, wd, fnw, state=None, *, tm, tf, seq_tiles, srows=None):
    m = x1.shape[0]
    nf = D_FF // tf
    keep = FFN_CONV - 1
    with_state = state is not None
    kern = functools.partial(_ffn_kernel, tm=tm, tf=tf, seq_tiles=seq_tiles, with_state=with_state, srows=srows)
    in_specs = [pl.BlockSpec((tm, D_MODEL), lambda i, f: (i, 0))]
    args = [x1]
    if not with_state:
        in_specs.append(pl.BlockSpec((SUBLANES, D_MODEL),
                                     lambda i, f: (jnp.maximum(i * (tm // SUBLANES) - 1, 0), 0)))
        args.append(x1)
    in_specs += [
        pl.BlockSpec((1, D_MODEL), lambda i, f: (0, 0)),
        pl.BlockSpec((D_MODEL, tf), lambda i, f: (0, f)),
        pl.BlockSpec((D_MODEL, tf), lambda i, f: (0, f + nf)),
        pl.BlockSpec((FFN_CONV, tf), lambda i, f: (0, f)),
        pl.BlockSpec((FFN_CONV, tf), lambda i, f: (0, f + nf)),
        pl.BlockSpec((tf, D_MODEL), lambda i, f: (f, 0)),
        pl.BlockSpec((1, D_MODEL), lambda i, f: (0, 0)),
    ]
    args += [nw, wup, wup, cw, cw, wd, fnw]
    scratch = [pltpu.VMEM((SUBLANES + tm, D_MODEL), BF16),
               pltpu.VMEM((SUBLANES + tm, tf), F32),
               pltpu.VMEM((SUBLANES + tm, tf), F32),
               pltpu.VMEM((tm, D_MODEL), F32)]
    n_tiles = m // tm
    out_specs = [pl.BlockSpec((tm, D_MODEL), lambda i, f: (i, 0))]
    out_shape = [jax.ShapeDtypeStruct((m, D_MODEL), F32)]
    if with_state:
        nsq = tm // srows
        in_specs += [pl.BlockSpec((nsq, tf), lambda i, f: (i, f))] * keep
        in_specs += [pl.BlockSpec((nsq, tf), lambda i, f: (i, f + nf))] * keep
        args += list(state) * 2
        scratch += [pltpu.VMEM((tf // LANES, tm + SUBLANES, LANES), F32)] * 4
        out_specs += [pl.BlockSpec((nsq, tf), lambda i, f: (i, f))] * (2 * keep)
        out_shape += [jax.ShapeDtypeStruct((m // srows, D_FF), F32)] * (2 * keep)
    else:
        out_specs += [pl.BlockSpec((SUBLANES, tf), lambda i, f: (i, f))] * 2
        out_shape += [jax.ShapeDtypeStruct((n_tiles * SUBLANES, D_FF), F32)] * 2
    outs = pl.pallas_call(
        kern,
        grid=(n_tiles, nf),
        in_specs=in_specs,
        out_specs=out_specs,
        out_shape=out_shape,
        scratch_shapes=scratch,
        compiler_params=_params("parallel", "arbitrary"),
        name="ffn",
    )(*args)
    return outs[0], outs[1:]


def _layer(x, nb, nc, nseq, rows, valid, gdn_group, ml_group, states, w, *, tm_in, tm_merge, tm_ffn, seq_tiles,
           ffn_state):
    conv_state, s0, c0, n0, m0 = states
    big, gates = _inproj(x, w["norm_mix"], w["w_big"], w["w_small"], w["gate_bias"], w["a_log"],
                         tm=tm_in, tn=1024)
    names = [n for n in ("w_a", "w_b", "w_out", "w_up", "w_down") if w[n].dtype != BF16]
    go, s_new, conv_new, *cast = _gdn(big, gates, conv_state, s0, w["gdn_conv_w"], w["gdn_norm"],
                                      [w[n] for n in names], nb=nb, nc=nc, nseq=nseq, rows=rows, valid=valid,
                                      gsz=gdn_group[0], cps=gdn_group[1])
    w = {**w, **dict(zip(names, cast))}
    mh, c_new, n_new, m_new = _mlstm(big, gates, c0, n0, m0,
                                     nb=nb, nc=nc, nseq=nseq, rows=rows, valid=valid, gsz=ml_group)
    x1 = _merge(go, mh, big, x, w["w_a"], w["w_b"], w["w_out"], tm=tm_merge, tn=512)
    y, ffn_rows = _ffn(x1, w["norm_ffn"], w["w_up"], w["ffn_conv_w"], w["w_down"], w["norm_final"],
                       ffn_state, tm=tm_ffn, tf=512, seq_tiles=seq_tiles, srows=valid)
    return y, (conv_new, s_new, c_new, n_new, m_new.reshape(nb, HEADS)), ffn_rows, w


def kernel(x_prompt, x_sample, state_gdn_conv, state_gdn_S, state_ml_C, state_ml_n, state_ml_m, state_ffn_conv,
           norm_mix_w, w_in, gdn_conv_w, gdn_A_log, gdn_dt_bias, gdn_norm_w, w_branch_a, ml_b_i, ml_b_f,
           w_branch_b, w_out, norm_ffn_w, w_up, ffn_conv_w, w_down, norm_final_w):
    assert w_in.shape[0] == 1, "single-layer step"
    bp, tp, _ = x_prompt.shape
    bs, ts, _ = x_sample.shape
    assert FFN_CONV - 1 <= ts <= SUBLANES and GDN_CONV - 1 <= ts and tp % CHUNK == 0
    assert w_in.shape[1:] == (D_MODEL, IN_COLS)
    keep = FFN_CONV - 1

    w_big, w_small = _repack(w_in[0].T, tn=512)
    zeros8 = jnp.zeros((HEADS,), F32)
    w = {
        "norm_mix": norm_mix_w[0][None, :],
        "w_big": w_big,
        "w_small": w_small,
        "gate_bias": jnp.concatenate([gdn_dt_bias[0], zeros8, ml_b_i[0], ml_b_f[0]])[None, :],
        "a_log": jnp.concatenate([gdn_A_log[0], zeros8, zeros8, zeros8])[None, :],
        "gdn_conv_w": gdn_conv_w[0],
        "gdn_norm": gdn_norm_w[0][None, :],
        "w_a": w_branch_a[0],
        "w_b": w_branch_b[0],
        "w_out": w_out[0],
        "norm_ffn": norm_ffn_w[0][None, :],
        "w_up": w_up[0],
        "ffn_conv_w": ffn_conv_w[0],
        "w_down": w_down[0],
        "norm_final": norm_final_w[None, :],
    }

    xp = x_prompt.reshape(bp * tp, D_MODEL)
    p_states = (jnp.zeros((bp, GDN_CONV - 1, QKV), F32), jnp.zeros((bp, HEADS, DH, DH), F32),
                jnp.zeros((bp, HEADS, DH, DH), F32), jnp.zeros((bp, HEADS, DH), F32),
                jnp.zeros((bp, 1, HEADS), F32))
    tm_ffn = 512
    yp, p_new, up_p, w = _layer(
        xp, bp, tp // CHUNK, 1, CHUNK, CHUNK, (4, 4), HEADS, p_states, w,
        tm_in=1024, tm_merge=1024, tm_ffn=tm_ffn, seq_tiles=tp // tm_ffn, ffn_state=None)
    p_ffn_conv = jnp.concatenate(
        [u.reshape(bp, tp // tm_ffn, SUBLANES, D_FF)[:, -1, SUBLANES - keep:, :] for u in up_p], axis=-1)

    xs = x_sample.reshape(bs * ts, D_MODEL)
    s_states = (state_gdn_conv[0], state_gdn_S[0], state_ml_C[0], state_ml_n[0],
                state_ml_m[0].reshape(bs, 1, HEADS))
    ys, s_new, new_rows, _ = _layer(
        xs, bs, 1, 4, SUBLANES, ts, (2 * HEADS, 1), 2 * HEADS, s_states, w,
        tm_in=bs * ts, tm_merge=bs * ts, tm_ffn=bs * ts, seq_tiles=1,
        ffn_state=[state_ffn_conv[0, :, r, :] for r in range(keep)])
    s_ffn_conv = jnp.stack([jnp.concatenate([new_rows[r], new_rows[keep + r]], axis=-1) for r in range(keep)], axis=1)

    lead = lambda t: tuple(a[None] for a in t)
    return (yp.reshape(bp, tp, D_MODEL), ys.reshape(bs, ts, D_MODEL),
            *lead(p_new), p_ffn_conv[None], *lead(s_new), s_ffn_conv[None])
```

```python
import functools
import math

import jax
import jax.numpy as jnp
from jax import lax
from jax.experimental import pallas as pl
from jax.experimental.pallas import tpu as pltpu

F32 = jnp.float32
BF16 = jnp.bfloat16

D_MODEL = 2048
HEADS = 8
DH = 128
QK = HEADS * DH
QKV = 3 * QK
D_FF = 5632
GDN_CONV = 4
FFN_CONV = 3
EPS = 1e-6
CHUNK = 64
N_GATES = 4 * HEADS
SUBLANES = 8
NEG = -1e30

BIG_COLS = 2 * QKV + 2 * QK + 2 * D_MODEL

VMEM_LIMIT = 56 * 1024 * 1024


def _params(*sem):
    return pltpu.CompilerParams(dimension_semantics=sem, vmem_limit_bytes=VMEM_LIMIT)


def _dot(a, b):
    return jnp.dot(a.astype(BF16), b.astype(BF16), preferred_element_type=F32)


def _dot_nt(a, b):
    return lax.dot_general(a.astype(BF16), b.astype(BF16), (((1,), (1,)), ((), ())),
                           preferred_element_type=F32)


def _dot_tn(a, b):
    return lax.dot_general(a.astype(BF16), b.astype(BF16), (((0,), (0,)), ((), ())),
                           preferred_element_type=F32)


def _softplus(x):
    return jnp.maximum(x, 0.0) + jnp.log1p(jnp.exp(-jnp.abs(x)))


def _rms(x, w):
    return x * lax.rsqrt(jnp.mean(x * x, axis=-1, keepdims=True) + EPS) * w


IN_SECTIONS = (("gqkv", QKV), ("gz", QK), ("ga", HEADS), ("gb", HEADS), ("mqkv", QKV),
               ("mi", HEADS), ("mf", HEADS), ("mo", QK), ("gA", D_MODEL), ("gB", D_MODEL))
BIG_ORDER = ("gqkv", "mqkv", "gz", "mo", "gA", "gB")
IN_COLS = sum(size for _, size in IN_SECTIONS)


def _in_start(name):
    off = 0
    for n, size in IN_SECTIONS:
        if n == name:
            return off
        off += size
    raise KeyError(name)


def _big_tiles(tn):
    starts = []
    for name in BIG_ORDER:
        size = dict(IN_SECTIONS)[name]
        assert size % tn == 0
        starts += [_in_start(name) + t * tn for t in range(size // tn)]
    return starts


def _tile_lookup(j, values):
    out = jnp.int32(values[-1])
    for t in range(len(values) - 2, -1, -1):
        out = jnp.where(j <= t, jnp.int32(values[t]), out)
    return out


LANES = 128
SMALL_ORDER = ("ga", "gb", "mi", "mf")


def _repack_kernel(*refs):
    wt_ref, g_refs, out_ref, small_ref = refs[0], refs[1:1 + len(SMALL_ORDER)], refs[-2], refs[-1]

    @pl.when(pl.program_id(0) == 0)
    def _():
        cols = [g_ref[...].T[:, _in_start(n) % SUBLANES:_in_start(n) % SUBLANES + HEADS]
                for n, g_ref in zip(SMALL_ORDER, g_refs)]
        small_ref[...] = jnp.concatenate(cols, axis=1).astype(BF16)

    out_ref[...] = wt_ref[...].T.astype(BF16)


def _repack(w_in_t, *, tn):
    starts = _big_tiles(tn)
    assert all(s % SUBLANES == 0 for s in starts)
    rows8 = [s // SUBLANES for s in starts]
    return pl.pallas_call(
        _repack_kernel,
        grid=(BIG_COLS // tn,),
        in_specs=[pl.BlockSpec((pl.Element(tn), pl.Element(D_MODEL)),
                               lambda j: (_tile_lookup(j, rows8) * SUBLANES, 0))]
        + [pl.BlockSpec((SUBLANES, D_MODEL), lambda j, b=_in_start(n) // SUBLANES: (b, 0)) for n in SMALL_ORDER],
        out_specs=[pl.BlockSpec((D_MODEL, tn), lambda j: (0, j)),
                   pl.BlockSpec((D_MODEL, N_GATES), lambda j: (0, 0))],
        out_shape=[jax.ShapeDtypeStruct((D_MODEL, BIG_COLS), BF16),
                   jax.ShapeDtypeStruct((D_MODEL, N_GATES), BF16)],
        compiler_params=_params("arbitrary"),
        name="repack",
    )(*([w_in_t] * (1 + len(SMALL_ORDER))))


def _inproj_kernel(x_ref, nw_ref, wbig_ref, wsm_ref, gbias_ref, alog_ref, big_ref, gates_ref, h_scr):
    @pl.when(pl.program_id(1) == 0)
    def _():
        hb = _rms(x_ref[...], nw_ref[...]).astype(BF16)
        h_scr[...] = hb
        raw = jnp.dot(hb, wsm_ref[...], preferred_element_type=F32)
        lane = lax.broadcasted_iota(jnp.int32, raw.shape, 1)
        z = raw + gbias_ref[...]
        g = -jnp.exp(alog_ref[...]) * _softplus(z)
        beta = jax.nn.sigmoid(raw)
        lf = -_softplus(-z)
        gates_ref[...] = jnp.where(lane < HEADS, g,
                                   jnp.where(lane < 2 * HEADS, beta,
                                             jnp.where(lane < 3 * HEADS, z, lf)))

    big_ref[...] = jnp.dot(h_scr[...], wbig_ref[...], preferred_element_type=F32)


def _inproj(x, nw, wbig, wsm, gbias, alog, *, tm, tn):
    m = x.shape[0]
    return pl.pallas_call(
        _inproj_kernel,
        grid=(m // tm, BIG_COLS // tn),
        in_specs=[
            pl.BlockSpec((tm, D_MODEL), lambda i, j: (i, 0)),
            pl.BlockSpec((1, D_MODEL), lambda i, j: (0, 0)),
            pl.BlockSpec((D_MODEL, tn), lambda i, j: (0, j)),
            pl.BlockSpec((D_MODEL, N_GATES), lambda i, j: (0, 0)),
            pl.BlockSpec((1, N_GATES), lambda i, j: (0, 0)),
            pl.BlockSpec((1, N_GATES), lambda i, j: (0, 0)),
        ],
        out_specs=[
            pl.BlockSpec((tm, tn), lambda i, j: (i, j)),
            pl.BlockSpec((tm, N_GATES), lambda i, j: (i, 0)),
        ],
        out_shape=[jax.ShapeDtypeStruct((m, BIG_COLS), F32),
                   jax.ShapeDtypeStruct((m, N_GATES), F32)],
        scratch_shapes=[pltpu.VMEM((tm, D_MODEL), BF16)],
        compiler_params=_params("parallel", "arbitrary"),
        name="inproj",
    )(x, nw, wbig, wsm, gbias, alog)


def _group_masks(n, rows):
    r = n * rows
    shift = rows.bit_length() - 1
    ri = lax.broadcasted_iota(jnp.int32, (r, r), 0)
    ci = lax.broadcasted_iota(jnp.int32, (r, r), 1)
    same = lax.shift_right_logical(ri, shift) == lax.shift_right_logical(ci, shift)
    return ri == ci, same & (ci <= ri), same & (ci < ri)


def _row_block(nrows, rows, n):
    shift = rows.bit_length() - 1
    ri = lax.broadcasted_iota(jnp.int32, (nrows, 1), 0)
    return lax.shift_right_logical(ri, shift) & (n - 1)


def _row_of(col, eye):
    return jnp.sum(jnp.where(eye, col, 0.0), axis=0, keepdims=True)


def _cumsum_col(col, eye, incl):
    return jnp.sum(jnp.where(incl, _row_of(col, eye), 0.0), axis=1, keepdims=True)


def _per_row(vals, rows):
    return jnp.concatenate([jnp.broadcast_to(v, (rows, 1)) for v in vals], axis=0)


def _block_diag(x, rblk, n):
    return jnp.concatenate([jnp.where(rblk == c, x, 0.0) for c in range(n)], axis=1)


def _conv_silu(hist_ref, s, cw_ref, col0, rows):
    cols = slice(col0, col0 + DH)
    base = SUBLANES - GDN_CONV + 1
    acc = hist_ref[s, pl.ds(base, rows), cols] * cw_ref[0:1, cols]
    for j in range(1, GDN_CONV):
        acc = acc + hist_ref[s, pl.ds(base + j, rows), cols] * cw_ref[j:j + 1, cols]
    return acc * jax.nn.sigmoid(acc)


def _seq_loader(ref, scr, nseq, valid, rows, base=0):
    if valid == rows:
        return lambda s, cols: ref[pl.ds(base + s * rows, rows), cols]
    assert base == 0
    for s in range(nseq):
        scr[s, pl.ds(0, valid), :] = ref[pl.ds(s * valid, valid), :]
        scr[s, pl.ds(valid, rows - valid), :] = jnp.zeros((rows - valid, scr.shape[-1]), F32)
    return lambda s, cols: scr[s, :, cols]


def _mix_dtype(valid):
    return BF16 if valid % (2 * SUBLANES) == 0 else F32


def _problems(nseq, gsz):
    probs = [(s, h) for s in range(nseq) for h in range(HEADS)]
    return [probs[i:i + gsz] for i in range(0, len(probs), gsz)]


def _gdn_kernel(*refs, nseq, rows, valid, levels, gsz, cps, ncast):
    qkv_ref, z_ref, gates_ref, cst_ref, s0_ref, cw_ref, nw_ref = refs[:7]
    cast_in = refs[7:7 + ncast]
    o_ref, s_ref, cst_out_ref = refs[7 + ncast:10 + ncast]
    cast_out = refs[10 + ncast:10 + 2 * ncast]
    hist, pads = refs[10 + 2 * ncast], refs[11 + 2 * ncast:]
    keep = GDN_CONV - 1

    for src, dst in zip(cast_in, cast_out):
        dst[...] = src[...].astype(BF16)

    @pl.when(pl.program_id(1) == 0)
    def _():
        hist[:, pl.ds(SUBLANES - keep, keep), :] = cst_ref[...]
        s_ref[...] = s0_ref[...]

    r = gsz * rows
    eye, incl, strict = _group_masks(gsz, rows)
    rvalid = (lax.broadcasted_iota(jnp.int32, (r, 1), 0) & (rows - 1)) < valid
    masks = (eye, incl, strict, eye.astype(F32), rvalid, _row_block(r, rows, gsz))
    for ck in range(cps):
        _gdn_chunk(ck * nseq * valid, qkv_ref, z_ref, gates_ref, cw_ref, nw_ref, o_ref, s_ref, hist, pads, masks,
                   nseq=nseq, rows=rows, valid=valid, levels=levels, gsz=gsz)
    cst_out_ref[...] = hist[:, pl.ds(SUBLANES - keep, keep), :]


def _gdn_chunk(base, qkv_ref, z_ref, gates_ref, cw_ref, nw_ref, o_ref, s_ref, hist, pads, masks,
               *, nseq, rows, valid, levels, gsz):
    eye, incl, strict, eye_f, rvalid, rblk = masks
    keep = GDN_CONV - 1
    r = gsz * rows
    for s in range(nseq):
        hist[s, pl.ds(SUBLANES, valid), :] = qkv_ref[pl.ds(base + s * valid, valid), :]
        if valid < rows:
            hist[s, pl.ds(SUBLANES + valid, rows - valid), :] = jnp.zeros((rows - valid, QKV), F32)
    zpad, gpad = pads if pads else (None, None)
    load_z = _seq_loader(z_ref, zpad, nseq, valid, rows, base)
    load_g = _seq_loader(gates_ref, gpad, nseq, valid, rows, base)

    for group in _problems(nseq, gsz):
        def stack(fn):
            return jnp.concatenate([fn(s, h) for s, h in group], axis=0)

        q = stack(lambda s, h: _conv_silu(hist, s, cw_ref, h * DH, rows))
        k = stack(lambda s, h: _conv_silu(hist, s, cw_ref, QK + h * DH, rows))
        v = stack(lambda s, h: _conv_silu(hist, s, cw_ref, 2 * QK + h * DH, rows))
        q = q * lax.rsqrt(jnp.sum(q * q, axis=-1, keepdims=True) + EPS) * (DH ** -0.5)
        k = k * lax.rsqrt(jnp.sum(k * k, axis=-1, keepdims=True) + EPS)
        g = jnp.where(rvalid, stack(lambda s, h: load_g(s, slice(h, h + 1))), 0.0)
        beta = jnp.where(rvalid, stack(lambda s, h: load_g(s, slice(HEADS + h, HEADS + h + 1))), 0.0)

        gc = _cumsum_col(g, eye, incl)
        gr = _row_of(gc, eye)
        decay = jnp.where(incl, jnp.exp(jnp.where(incl, gc - gr, 0.0)), 0.0)
        a = jnp.where(strict, beta * _dot_nt(k, k) * decay, 0.0)
        qk = _dot_nt(q, k) * decay
        bk = -a
        t = eye_f + bk
        if levels >= 2:
            bk = _dot(bk, bk)
            for _ in range(2, levels):
                t, bk = t + _dot(t, bk), _dot(bk, bk)
            t = t + _dot(t, bk)
        eg = jnp.exp(gc)
        uw = _dot(t, jnp.concatenate([v * beta, k * (beta * eg)], axis=1))
        u, w = uw[:, :DH], uw[:, DH:]

        s_old = [s_ref[s, h] for s, h in group]
        s_stack = jnp.concatenate(s_old, axis=0)
        v_new = u - _dot(_block_diag(w, rblk, gsz), s_stack)
        o = _dot(_block_diag(q * eg, rblk, gsz), s_stack) + _dot(qk, v_new)
        g_last = [gc[(i + 1) * rows - 1:(i + 1) * rows, :] for i in range(gsz)]
        kd = k * jnp.exp(_per_row(g_last, rows) - gc)
        if rows >= CHUNK:
            s_upd = [_dot_tn(kd[i * rows:(i + 1) * rows], v_new[i * rows:(i + 1) * rows]) for i in range(gsz)]
        else:
            s_all = _dot_tn(_block_diag(kd, rblk, gsz), v_new)
            s_upd = [s_all[i * DH:(i + 1) * DH] for i in range(gsz)]
        for i, (s, h) in enumerate(group):
            s_ref[s, h] = s_old[i] * jnp.exp(g_last[i]) + s_upd[i]

        zs = stack(lambda s, h: load_z(s, slice(h * DH, (h + 1) * DH)))
        on = _rms(o, nw_ref[...]) * (zs * jax.nn.sigmoid(zs))
        for i, (s, h) in enumerate(group):
            o_ref[pl.ds(base + s * valid, valid), h * DH:(h + 1) * DH] = on[i * rows:i * rows + valid].astype(o_ref.dtype)

    for s in range(nseq):
        hist[s, pl.ds(SUBLANES - keep, keep), :] = hist[s, pl.ds(SUBLANES + valid - keep, keep), :]


def _gdn(big, gates, conv_state, s0, conv_w, norm_w, casts=(), *, nb, nc, nseq, rows, valid, gsz, cps):
    assert nc % cps == 0 and (cps == 1 or nseq == 1)
    nc = nc // cps
    levels = max(1, math.ceil(math.log2(valid)))
    kern = functools.partial(_gdn_kernel, nseq=nseq, rows=rows, valid=valid, levels=levels, gsz=gsz, cps=cps,
                             ncast=len(casts))
    br = nseq * valid * cps
    scratch = [pltpu.VMEM((nseq, SUBLANES + rows, QKV), F32)]
    if valid < rows:
        scratch += [pltpu.VMEM((nseq, rows, QK), F32), pltpu.VMEM((nseq, rows, N_GATES), F32)]
    steps = nb // nseq * nc
    bf16_rows = 2 * SUBLANES
    assert all(a.shape[0] % (steps * bf16_rows) == 0 for a in casts)
    cast_specs = [pl.BlockSpec((a.shape[0] // steps, a.shape[1]), lambda b, c: (b * nc + c, 0)) for a in casts]
    return pl.pallas_call(
        kern,
        grid=(nb // nseq, nc),
        in_specs=[
            pl.BlockSpec((br, QKV), lambda b, c: (b * nc + c, 0)),
            pl.BlockSpec((br, QK), lambda b, c: (b * nc + c, 2 * QKV // QK)),
            pl.BlockSpec((br, N_GATES), lambda b, c: (b * nc + c, 0)),
            pl.BlockSpec((nseq, GDN_CONV - 1, QKV), lambda b, c: (b, 0, 0)),
            pl.BlockSpec((nseq, HEADS, DH, DH), lambda b, c: (b, 0, 0, 0)),
            pl.BlockSpec((GDN_CONV, QKV), lambda b, c: (0, 0)),
            pl.BlockSpec((1, DH), lambda b, c: (0, 0)),
            *cast_specs,
        ],
        out_specs=[
            pl.BlockSpec((br, QK), lambda b, c: (b * nc + c, 0)),
            pl.BlockSpec((nseq, HEADS, DH, DH), lambda b, c: (b, 0, 0, 0)),
            pl.BlockSpec((nseq, GDN_CONV - 1, QKV), lambda b, c: (b, 0, 0)),
            *cast_specs,
        ],
        out_shape=[jax.ShapeDtypeStruct((nb // nseq * nc * br, QK), _mix_dtype(valid)),
                   jax.ShapeDtypeStruct((nb, HEADS, DH, DH), F32),
                   jax.ShapeDtypeStruct((nb, GDN_CONV - 1, QKV), F32),
                   *[jax.ShapeDtypeStruct(a.shape, BF16) for a in casts]],
        scratch_shapes=scratch,
        compiler_params=_params("parallel", "arbitrary"),
        name="gdn",
    )(big, big, gates, conv_state, s0, conv_w, norm_w, *casts)


def _mlstm_kernel(qkv_ref, og_ref, gates_ref, c0_ref, n0_ref, m0_ref, h_ref, c_ref, n_ref, m_ref, *pads,
                  nseq, rows, valid, gsz):
    @pl.when(pl.program_id(1) == 0)
    def _():
        c_ref[...] = c0_ref[...]
        n_ref[...] = n0_ref[...]
        m_ref[...] = m0_ref[...]

    r = gsz * rows
    eye, incl, _ = _group_masks(gsz, rows)
    rvalid = (lax.broadcasted_iota(jnp.int32, (r, 1), 0) & (rows - 1)) < valid
    rblk = _row_block(r, rows, gsz)
    xpad, opad, gpad = pads if pads else (None, None, None)
    load_x = _seq_loader(qkv_ref, xpad, nseq, valid, rows)
    load_o = _seq_loader(og_ref, opad, nseq, valid, rows)
    load_g = _seq_loader(gates_ref, gpad, nseq, valid, rows)
    m_all = m_ref[...]

    for group in _problems(nseq, gsz):
        def stack(fn):
            return jnp.concatenate([fn(s, h) for s, h in group], axis=0)

        q = stack(lambda s, h: load_x(s, slice(h * DH, (h + 1) * DH)))
        k = stack(lambda s, h: load_x(s, slice(QK + h * DH, QK + (h + 1) * DH))) * (DH ** -0.5)
        v = stack(lambda s, h: load_x(s, slice(2 * QK + h * DH, 2 * QK + (h + 1) * DH)))
        ig = jnp.where(rvalid, stack(lambda s, h: load_g(s, slice(2 * HEADS + h, 2 * HEADS + h + 1))), NEG)
        lf = jnp.where(rvalid, stack(lambda s, h: load_g(s, slice(3 * HEADS + h, 3 * HEADS + h + 1))), 0.0)

        bc = _cumsum_col(lf, eye, incl)
        br = _row_of(bc, eye)
        igr = _row_of(ig, eye)
        d_log = jnp.where(incl, bc - br + igr, NEG)
        d_max = jnp.max(d_log, axis=1, keepdims=True)
        b_last = [bc[(i + 1) * rows - 1:(i + 1) * rows, :] for i in range(gsz)]
        e_log = _per_row(b_last, rows) - bc + ig
        e_max = [jnp.max(e_log[i * rows:(i + 1) * rows], axis=0, keepdims=True) for i in range(gsz)]
        qk = _dot_nt(q, k)

        c_old = [c_ref[s, h] for s, h in group]
        n_old = [n_ref[s, h:h + 1, :] for s, h in group]
        m_old = [m_all[s, :, h:h + 1] for s, h in group]
        inter = bc + _per_row(m_old, rows)
        mt = jnp.maximum(inter, d_max)
        wi = jnp.exp(inter - mt)
        p = jnp.where(incl, jnp.exp(d_log - mt), 0.0) * qk
        qc = _dot_nt(_block_diag(q, rblk, gsz), jnp.concatenate(c_old, axis=1))
        num = wi * qc + _dot(p, v)
        n_rows = jnp.concatenate([jnp.broadcast_to(nv, (rows, DH)) for nv in n_old], axis=0)
        den = wi * jnp.sum(q * n_rows, axis=-1, keepdims=True) + jnp.sum(p, axis=-1, keepdims=True)
        hv = num / jnp.maximum(jnp.abs(den), jnp.exp(-mt))

        m_new = [jnp.maximum(b_last[i] + m_old[i], e_max[i]) for i in range(gsz)]
        fw = [jnp.exp(b_last[i] + m_old[i] - m_new[i]) for i in range(gsz)]
        sw = jnp.exp(e_log - _per_row(m_new, rows))
        c_upd = _dot_tn(sw * v, _block_diag(k, rblk, gsz))
        swk = sw * k
        for i, (s, h) in enumerate(group):
            c_ref[s, h] = fw[i] * c_old[i] + c_upd[:, i * DH:(i + 1) * DH]
            n_ref[s, h:h + 1, :] = fw[i] * n_old[i] + jnp.sum(swk[i * rows:(i + 1) * rows], axis=0, keepdims=True)
            m_ref[s, :, h:h + 1] = m_new[i]

        og = stack(lambda s, h: load_o(s, slice(h * DH, (h + 1) * DH)))
        hg = hv * jax.nn.sigmoid(og)
        for i, (s, h) in enumerate(group):
            h_ref[pl.ds(s * valid, valid), h * DH:(h + 1) * DH] = hg[i * rows:i * rows + valid].astype(h_ref.dtype)


def _mlstm(big, gates, c0, n0, m0, *, nb, nc, nseq, rows, valid, gsz):
    kern = functools.partial(_mlstm_kernel, nseq=nseq, rows=rows, valid=valid, gsz=gsz)
    br = nseq * valid
    scratch = []
    if valid < rows:
        scratch = [pltpu.VMEM((nseq, rows, QKV), F32), pltpu.VMEM((nseq, rows, QK), F32),
                   pltpu.VMEM((nseq, rows, N_GATES), F32)]
    return pl.pallas_call(
        kern,
        grid=(nb // nseq, nc),
        in_specs=[
            pl.BlockSpec((br, QKV), lambda b, c: (b * nc + c, 1)),
            pl.BlockSpec((br, QK), lambda b, c: (b * nc + c, 2 * QKV // QK + 1)),
            pl.BlockSpec((br, N_GATES), lambda b, c: (b * nc + c, 0)),
            pl.BlockSpec((nseq, HEADS, DH, DH), lambda b, c: (b, 0, 0, 0)),
            pl.BlockSpec((nseq, HEADS, DH), lambda b, c: (b, 0, 0)),
            pl.BlockSpec((nseq, 1, HEADS), lambda b, c: (b, 0, 0)),
        ],
        out_specs=[
            pl.BlockSpec((br, QK), lambda b, c: (b * nc + c, 0)),
            pl.BlockSpec((nseq, HEADS, DH, DH), lambda b, c: (b, 0, 0, 0)),
            pl.BlockSpec((nseq, HEADS, DH), lambda b, c: (b, 0, 0)),
            pl.BlockSpec((nseq, 1, HEADS), lambda b, c: (b, 0, 0)),
        ],
        out_shape=[jax.ShapeDtypeStruct((nb * nc * valid, QK), _mix_dtype(valid)),
                   jax.ShapeDtypeStruct((nb, HEADS, DH, DH), F32),
                   jax.ShapeDtypeStruct((nb, HEADS, DH), F32),
                   jax.ShapeDtypeStruct((nb, 1, HEADS), F32)],
        scratch_shapes=scratch,
        compiler_params=_params("parallel", "arbitrary"),
        name="mlstm",
    )(big, big, gates, c0, n0, m0)


def _merge_kernel(go_ref, mh_ref, ga_ref, gb_ref, x_ref, wa_ref, wb_ref, wo_ref, x1_ref, mix_scr, *, nt, tn):
    j = pl.program_id(1)

    @pl.when(j < nt)
    def _():
        ya = jnp.dot(go_ref[...].astype(BF16), wa_ref[...], preferred_element_type=F32)
        yb = jnp.dot(mh_ref[...].astype(BF16), wb_ref[...], preferred_element_type=F32)
        mixed = jax.nn.sigmoid(ga_ref[...]) * ya + jax.nn.sigmoid(gb_ref[...]) * yb
        mixed = mixed.astype(BF16)
        for t in range(nt):
            @pl.when(j == t)
            def _():
                mix_scr[:, t * tn:(t + 1) * tn] = mixed

    @pl.when(j >= nt)
    def _():
        x1_ref[...] = x_ref[...] + jnp.dot(mix_scr[...], wo_ref[...], preferred_element_type=F32)


def _merge(go, mh, big, x, wa, wb, wo, *, tm, tn):
    m = x.shape[0]
    nt = D_MODEL // tn
    ga_blk = (2 * QKV + 2 * QK) // tn
    mix_j = lambda j: jnp.minimum(j, nt - 1)
    out_j = lambda j: jnp.maximum(j - nt, 0)
    return pl.pallas_call(
        functools.partial(_merge_kernel, nt=nt, tn=tn),
        grid=(m // tm, 2 * nt),
        in_specs=[
            pl.BlockSpec((tm, QK), lambda i, j: (i, 0)),
            pl.BlockSpec((tm, QK), lambda i, j: (i, 0)),
            pl.BlockSpec((tm, tn), lambda i, j: (i, ga_blk + mix_j(j))),
            pl.BlockSpec((tm, tn), lambda i, j: (i, ga_blk + nt + mix_j(j))),
            pl.BlockSpec((tm, tn), lambda i, j: (i, out_j(j))),
            pl.BlockSpec((QK, tn), lambda i, j: (0, mix_j(j))),
            pl.BlockSpec((QK, tn), lambda i, j: (0, mix_j(j))),
            pl.BlockSpec((D_MODEL, tn), lambda i, j: (0, out_j(j))),
        ],
        out_specs=pl.BlockSpec((tm, tn), lambda i, j: (i, out_j(j))),
        out_shape=jax.ShapeDtypeStruct((m, D_MODEL), F32),
        scratch_shapes=[pltpu.VMEM((tm, D_MODEL), BF16)],
        compiler_params=_params("parallel", "arbitrary"),
        name="merge",
    )(go, mh, big, big, x, wa, wb, wo)


def _ffn_kernel(*refs, tm, tf, seq_tiles, with_state, srows):
    keep = FFN_CONV - 1
    if with_state:
        (x1_ref, nw_ref, wg_ref, wv_ref, cwg_ref, cwv_ref, wd_ref, fnw_ref) = refs[:8]
        stg_refs, stv_refs = refs[8:8 + keep], refs[8 + keep:8 + 2 * keep]
        y_ref = refs[8 + 2 * keep]
        newg_refs, newv_refs = refs[9 + 2 * keep:9 + 3 * keep], refs[9 + 3 * keep:9 + 4 * keep]
        h2_scr, hist_g, hist_v, acc_scr, st_g, st_v, cp_g, cp_v = refs[9 + 4 * keep:]
        nsq = tm // srows
        nck = tf // LANES
    else:
        (x1_ref, halo_ref, nw_ref, wg_ref, wv_ref, cwg_ref, cwv_ref, wd_ref, fnw_ref,
         y_ref, upg_ref, upv_ref, h2_scr, hist_g, hist_v, acc_scr) = refs
    i = pl.program_id(0)
    f = pl.program_id(1)
    pad = SUBLANES

    @pl.when(f == 0)
    def _():
        if with_state:
            h2_scr[pl.ds(0, pad), :] = jnp.zeros((pad, D_MODEL), BF16)
        else:
            live = (i % seq_tiles != 0).astype(F32)
            h2_scr[pl.ds(0, pad), :] = (_rms(halo_ref[...], nw_ref[...]) * live).astype(BF16)
        h2_scr[pl.ds(pad, tm), :] = _rms(x1_ref[...], nw_ref[...]).astype(BF16)
        acc_scr[...] = jnp.zeros_like(acc_scr)

    h2 = h2_scr[...]
    hist_g[...] = jnp.dot(h2, wg_ref[...], preferred_element_type=F32)
    hist_v[...] = jnp.dot(h2, wv_ref[...], preferred_element_type=F32)

    if with_state:
        for st, cp, st_refs, hist, new_refs in ((st_g, cp_g, stg_refs, hist_g, newg_refs),
                                                (st_v, cp_v, stv_refs, hist_v, newv_refs)):
            st[...] = jnp.zeros_like(st)
            for c in range(nck):
                lanes = slice(c * LANES, (c + 1) * LANES)
                cp[c] = hist[:, lanes]
                for r in range(keep):
                    st[c, pl.ds(r, nsq, stride=srows), :] = st_refs[r][:, lanes]
                    new_refs[r][:, lanes] = cp[c, pl.ds(pad + srows - keep + r, nsq, stride=srows), :]
        rmod = lax.broadcasted_iota(jnp.int32, (tm, 1), 0) % srows
        slab = lambda st, off: jnp.concatenate([st[c, pl.ds(off, tm), :] for c in range(nck)], axis=1)
    else:
        upg_ref[...] = hist_g[pl.ds(tm, pad), :]
        upv_ref[...] = hist_v[pl.ds(tm, pad), :]

    def conv(hist, cw_ref, st):
        prev2 = hist[pl.ds(pad - 2, tm), :]
        prev1 = hist[pl.ds(pad - 1, tm), :]
        if with_state:
            prev2 = jnp.where(rmod < 2, slab(st, 0), prev2)
            prev1 = jnp.where(rmod < 1, slab(st, 1), prev1)
        return (prev2 * cw_ref[0:1, :] + prev1 * cw_ref[1:2, :]) + hist[pl.ds(pad, tm), :] * cw_ref[2:3, :]

    ug = conv(hist_g, cwg_ref, st_g if with_state else None)
    uv = conv(hist_v, cwv_ref, st_v if with_state else None)
    act = (ug * jax.nn.sigmoid(ug) * uv).astype(BF16)
    acc_scr[...] += jnp.dot(act, wd_ref[...], preferred_element_type=F32)

    @pl.when(f == pl.num_programs(1) - 1)
    def _():
        y_ref[...] = _rms(x1_ref[...] + acc_scr[...], fnw_ref[...])


def _ffn(x1, nw, wup, cw, wd, fnw, state=None, *, tm, tf, seq_tiles, srows=None):
    m = x1.shape[0]
    nf = D_FF // tf
    keep = FFN_CONV - 1
    with_state = state is not None
    kern = functools.partial(_ffn_kernel, tm=tm, tf=tf, seq_tiles=seq_tiles, with_state=with_state, srows=srows)
    in_specs = [pl.BlockSpec((tm, D_MODEL), lambda i, f: (i, 0))]
    args = [x1]
    if not with_state:
        in_specs.append(pl.BlockSpec((SUBLANES, D_MODEL),
                                     lambda i, f: (jnp.maximum(i * (tm // SUBLANES) - 1, 0), 0)))
        args.append(x1)
    in_specs += [
        pl.BlockSpec((1, D_MODEL), lambda i, f: (0, 0)),
        pl.BlockSpec((D_MODEL, tf), lambda i, f: (0, f)),
        pl.BlockSpec((D_MODEL, tf), lambda i, f: (0, f + nf)),
        pl.BlockSpec((FFN_CONV, tf), lambda i, f: (0, f)),
        pl.BlockSpec((FFN_CONV, tf), lambda i, f: (0, f + nf)),
        pl.BlockSpec((tf, D_MODEL), lambda i, f: (f, 0)),
        pl.BlockSpec((1, D_MODEL), lambda i, f: (0, 0)),
    ]
    args += [nw, wup, wup, cw, cw, wd, fnw]
    scratch = [pltpu.VMEM((SUBLANES + tm, D_MODEL), BF16),
               pltpu.VMEM((SUBLANES + tm, tf), F32),
               pltpu.VMEM((SUBLANES + tm, tf), F32),
               pltpu.VMEM((tm, D_MODEL), F32)]
    n_tiles = m // tm
    out_specs = [pl.BlockSpec((tm, D_MODEL), lambda i, f: (i, 0))]
    out_shape = [jax.ShapeDtypeStruct((m, D_MODEL), F32)]
    if with_state:
        nsq = tm // srows
        in_specs += [pl.BlockSpec((nsq, tf), lambda i, f: (i, f))] * keep
        in_specs += [pl.BlockSpec((nsq, tf), lambda i, f: (i, f + nf))] * keep
        args += list(state) * 2
        scratch += [pltpu.VMEM((tf // LANES, tm + SUBLANES, LANES), F32)] * 4
        out_specs += [pl.BlockSpec((nsq, tf), lambda i, f: (i, f))] * (2 * keep)
        out_shape += [jax.ShapeDtypeStruct((m // srows, D_FF), F32)] * (2 * keep)
    else:
        out_specs += [pl.BlockSpec((SUBLANES, tf), lambda i, f: (i, f))] * 2
        out_shape += [jax.ShapeDtypeStruct((n_tiles * SUBLANES, D_FF), F32)] * 2
    outs = pl.pallas_call(
        kern,
        grid=(n_tiles, nf),
        in_specs=in_specs,
        out_specs=out_specs,
        out_shape=out_shape,
        scratch_shapes=scratch,
        compiler_params=_params("parallel", "arbitrary"),
        name="ffn",
    )(*args)
    return outs[0], outs[1:]


def _layer(x, nb, nc, nseq, rows, valid, gdn_group, ml_group, states, w, *, tm_in, tm_merge, tm_ffn, seq_tiles,
           ffn_state):
    conv_state, s0, c0, n0, m0 = states
    big, gates = _inproj(x, w["norm_mix"], w["w_big"], w["w_small"], w["gate_bias"], w["a_log"],
                         tm=tm_in, tn=1536)
    names = [n for n in ("w_a", "w_b", "w_out", "w_up", "w_down") if w[n].dtype != BF16]
    go, s_new, conv_new, *cast = _gdn(big, gates, conv_state, s0, w["gdn_conv_w"], w["gdn_norm"],
                                      [w[n] for n in names], nb=nb, nc=nc, nseq=nseq, rows=rows, valid=valid,
                                      gsz=gdn_group[0], cps=gdn_group[1])
    w = {**w, **dict(zip(names, cast))}
    mh, c_new, n_new, m_new = _mlstm(big, gates, c0, n0, m0,
                                     nb=nb, nc=nc, nseq=nseq, rows=rows, valid=valid, gsz=ml_group)
    x1 = _merge(go, mh, big, x, w["w_a"], w["w_b"], w["w_out"], tm=tm_merge, tn=512)
    y, ffn_rows = _ffn(x1, w["norm_ffn"], w["w_up"], w["ffn_conv_w"], w["w_down"], w["norm_final"],
                       ffn_state, tm=tm_ffn, tf=512, seq_tiles=seq_tiles, srows=valid)
    return y, (conv_new, s_new, c_new, n_new, m_new.reshape(nb, HEADS)), ffn_rows, w


def kernel(x_prompt, x_sample, state_gdn_conv, state_gdn_S, state_ml_C, state_ml_n, state_ml_m, state_ffn_conv,
           norm_mix_w, w_in, gdn_conv_w, gdn_A_log, gdn_dt_bias, gdn_norm_w, w_branch_a, ml_b_i, ml_b_f,
           w_branch_b, w_out, norm_ffn_w, w_up, ffn_conv_w, w_down, norm_final_w):
    assert w_in.shape[0] == 1, "single-layer step"
    bp, tp, _ = x_prompt.shape
    bs, ts, _ = x_sample.shape
    assert FFN_CONV - 1 <= ts <= SUBLANES and GDN_CONV - 1 <= ts and tp % CHUNK == 0
    assert w_in.shape[1:] == (D_MODEL, IN_COLS)
    keep = FFN_CONV - 1

    w_big, w_small = _repack(w_in[0].T, tn=512)
    zeros8 = jnp.zeros((HEADS,), F32)
    w = {
        "norm_mix": norm_mix_w[0][None, :],
        "w_big": w_big,
        "w_small": w_small,
        "gate_bias": jnp.concatenate([gdn_dt_bias[0], zeros8, ml_b_i[0], ml_b_f[0]])[None, :],
        "a_log": jnp.concatenate([gdn_A_log[0], zeros8, zeros8, zeros8])[None, :],
        "gdn_conv_w": gdn_conv_w[0],
        "gdn_norm": gdn_norm_w[0][None, :],
        "w_a": w_branch_a[0],
        "w_b": w_branch_b[0],
        "w_out": w_out[0],
        "norm_ffn": norm_ffn_w[0][None, :],
        "w_up": w_up[0],
        "ffn_conv_w": ffn_conv_w[0],
        "w_down": w_down[0],
        "norm_final": norm_final_w[None, :],
    }

    xp = x_prompt.reshape(bp * tp, D_MODEL)
    p_states = (jnp.zeros((bp, GDN_CONV - 1, QKV), F32), jnp.zeros((bp, HEADS, DH, DH), F32),
                jnp.zeros((bp, HEADS, DH, DH), F32), jnp.zeros((bp, HEADS, DH), F32),
                jnp.zeros((bp, 1, HEADS), F32))
    tm_ffn = 512
    yp, p_new, up_p, w = _layer(
        xp, bp, tp // CHUNK, 1, CHUNK, CHUNK, (4, 4), HEADS, p_states, w,
        tm_in=1024, tm_merge=1024, tm_ffn=tm_ffn, seq_tiles=tp // tm_ffn, ffn_state=None)
    p_ffn_conv = jnp.concatenate(
        [u.reshape(bp, tp // tm_ffn, SUBLANES, D_FF)[:, -1, SUBLANES - keep:, :] for u in up_p], axis=-1)

    xs = x_sample.reshape(bs * ts, D_MODEL)
    s_states = (state_gdn_conv[0], state_gdn_S[0], state_ml_C[0], state_ml_n[0],
                state_ml_m[0].reshape(bs, 1, HEADS))
    ys, s_new, new_rows, _ = _layer(
        xs, bs, 1, 8, SUBLANES, ts, (2 * HEADS, 1), 2 * HEADS, s_states, w,
        tm_in=bs * ts, tm_merge=bs * ts, tm_ffn=bs * ts, seq_tiles=1,
        ffn_state=[state_ffn_conv[0, :, r, :] for r in range(keep)])
    s_ffn_conv = jnp.stack([jnp.concatenate([new_rows[r], new_rows[keep + r]], axis=-1) for r in range(keep)], axis=1)

    lead = lambda t: tuple(a[None] for a in t)
    return (yp.reshape(bp, tp, D_MODEL), ys.reshape(bs, ts, D_MODEL),
            *lead(p_new), p_ffn_conv[None], *lead(s_new), s_ffn_conv[None])
```

```python
import functools
import math

import jax
import jax.numpy as jnp
from jax import lax
from jax.experimental import pallas as pl
from jax.experimental.pallas import tpu as pltpu

F32 = jnp.float32
BF16 = jnp.bfloat16

D_MODEL = 2048
HEADS = 8
DH = 128
QK = HEADS * DH
QKV = 3 * QK
D_FF = 5632
GDN_CONV = 4
FFN_CONV = 3
EPS = 1e-6
CHUNK = 64
N_GATES = 4 * HEADS
SUBLANES = 8
NEG = -1e30

BIG_COLS = 2 * QKV + 2 * QK + 2 * D_MODEL

VMEM_LIMIT = 56 * 1024 * 1024


TM_ROWS = 1024
TN_INPROJ = 1536
TN_MERGE = 512
TM_FFN = 512
TF_FFN = 512
TN_REPACK = 512
PROMPT_GDN_GROUP = (4, 4)
PROMPT_ML_GROUP = HEADS
SAMPLE_SEQS = 8
SAMPLE_GROUP = 2 * HEADS


def _params(*sem):
    return pltpu.CompilerParams(dimension_semantics=sem, vmem_limit_bytes=VMEM_LIMIT)


def _dot(a, b):
    return jnp.dot(a.astype(BF16), b.astype(BF16), preferred_element_type=F32)


def _dot_nt(a, b):
    return lax.dot_general(a.astype(BF16), b.astype(BF16), (((1,), (1,)), ((), ())),
                           preferred_element_type=F32)


def _dot_tn(a, b):
    return lax.dot_general(a.astype(BF16), b.astype(BF16), (((0,), (0,)), ((), ())),
                           preferred_element_type=F32)


def _softplus(x):
    return jnp.maximum(x, 0.0) + jnp.log1p(jnp.exp(-jnp.abs(x)))


def _rms(x, w):
    return x * lax.rsqrt(jnp.mean(x * x, axis=-1, keepdims=True) + EPS) * w


IN_SECTIONS = (("gqkv", QKV), ("gz", QK), ("ga", HEADS), ("gb", HEADS), ("mqkv", QKV),
               ("mi", HEADS), ("mf", HEADS), ("mo", QK), ("gA", D_MODEL), ("gB", D_MODEL))
BIG_ORDER = ("gqkv", "mqkv", "gz", "mo", "gA", "gB")
IN_COLS = sum(size for _, size in IN_SECTIONS)


def _in_start(name):
    off = 0
    for n, size in IN_SECTIONS:
        if n == name:
            return off
        off += size
    raise KeyError(name)


def _big_tiles(tn):
    starts = []
    for name in BIG_ORDER:
        size = dict(IN_SECTIONS)[name]
        assert size % tn == 0
        starts += [_in_start(name) + t * tn for t in range(size // tn)]
    return starts


def _tile_lookup(j, values):
    out = jnp.int32(values[-1])
    for t in range(len(values) - 2, -1, -1):
        out = jnp.where(j <= t, jnp.int32(values[t]), out)
    return out


LANES = 128
SMALL_ORDER = ("ga", "gb", "mi", "mf")


def _repack_kernel(*refs):
    wt_ref, g_refs, out_ref, small_ref = refs[0], refs[1:1 + len(SMALL_ORDER)], refs[-2], refs[-1]

    @pl.when(pl.program_id(0) == 0)
    def _():
        cols = [g_ref[...].T[:, _in_start(n) % SUBLANES:_in_start(n) % SUBLANES + HEADS]
                for n, g_ref in zip(SMALL_ORDER, g_refs)]
        small_ref[...] = jnp.concatenate(cols, axis=1).astype(BF16)

    out_ref[...] = wt_ref[...].T.astype(BF16)


def _repack(w_in_t, *, tn):
    starts = _big_tiles(tn)
    assert all(s % SUBLANES == 0 for s in starts)
    rows8 = [s // SUBLANES for s in starts]
    return pl.pallas_call(
        _repack_kernel,
        grid=(BIG_COLS // tn,),
        in_specs=[pl.BlockSpec((pl.Element(tn), pl.Element(D_MODEL)),
                               lambda j: (_tile_lookup(j, rows8) * SUBLANES, 0))]
        + [pl.BlockSpec((SUBLANES, D_MODEL), lambda j, b=_in_start(n) // SUBLANES: (b, 0)) for n in SMALL_ORDER],
        out_specs=[pl.BlockSpec((D_MODEL, tn), lambda j: (0, j)),
                   pl.BlockSpec((D_MODEL, N_GATES), lambda j: (0, 0))],
        out_shape=[jax.ShapeDtypeStruct((D_MODEL, BIG_COLS), BF16),
                   jax.ShapeDtypeStruct((D_MODEL, N_GATES), BF16)],
        compiler_params=_params("arbitrary"),
        name="repack",
    )(*([w_in_t] * (1 + len(SMALL_ORDER))))


def _inproj_kernel(x_ref, nw_ref, wbig_ref, wsm_ref, gbias_ref, alog_ref, big_ref, gates_ref, h_scr):
    @pl.when(pl.program_id(1) == 0)
    def _():
        hb = _rms(x_ref[...], nw_ref[...]).astype(BF16)
        h_scr[...] = hb
        raw = jnp.dot(hb, wsm_ref[...], preferred_element_type=F32)
        lane = lax.broadcasted_iota(jnp.int32, raw.shape, 1)
        z = raw + gbias_ref[...]
        g = -jnp.exp(alog_ref[...]) * _softplus(z)
        beta = jax.nn.sigmoid(raw)
        lf = -_softplus(-z)
        gates_ref[...] = jnp.where(lane < HEADS, g,
                                   jnp.where(lane < 2 * HEADS, beta,
                                             jnp.where(lane < 3 * HEADS, z, lf)))

    big_ref[...] = jnp.dot(h_scr[...], wbig_ref[...], preferred_element_type=F32)


def _inproj(x, nw, wbig, wsm, gbias, alog, *, tm, tn):
    m = x.shape[0]
    return pl.pallas_call(
        _inproj_kernel,
        grid=(m // tm, BIG_COLS // tn),
        in_specs=[
            pl.BlockSpec((tm, D_MODEL), lambda i, j: (i, 0)),
            pl.BlockSpec((1, D_MODEL), lambda i, j: (0, 0)),
            pl.BlockSpec((D_MODEL, tn), lambda i, j: (0, j)),
            pl.BlockSpec((D_MODEL, N_GATES), lambda i, j: (0, 0)),
            pl.BlockSpec((1, N_GATES), lambda i, j: (0, 0)),
            pl.BlockSpec((1, N_GATES), lambda i, j: (0, 0)),
        ],
        out_specs=[
            pl.BlockSpec((tm, tn), lambda i, j: (i, j)),
            pl.BlockSpec((tm, N_GATES), lambda i, j: (i, 0)),
        ],
        out_shape=[jax.ShapeDtypeStruct((m, BIG_COLS), F32),
                   jax.ShapeDtypeStruct((m, N_GATES), F32)],
        scratch_shapes=[pltpu.VMEM((tm, D_MODEL), BF16)],
        compiler_params=_params("parallel", "arbitrary"),
        name="inproj",
    )(x, nw, wbig, wsm, gbias, alog)


def _group_masks(n, rows):
    r = n * rows
    shift = rows.bit_length() - 1
    ri = lax.broadcasted_iota(jnp.int32, (r, r), 0)
    ci = lax.broadcasted_iota(jnp.int32, (r, r), 1)
    same = lax.shift_right_logical(ri, shift) == lax.shift_right_logical(ci, shift)
    return ri == ci, same & (ci <= ri), same & (ci < ri)


def _row_block(nrows, rows, n):
    shift = rows.bit_length() - 1
    ri = lax.broadcasted_iota(jnp.int32, (nrows, 1), 0)
    return lax.shift_right_logical(ri, shift) & (n - 1)


def _row_of(col, eye):
    return jnp.sum(jnp.where(eye, col, 0.0), axis=0, keepdims=True)


def _cumsum_col(col, eye, incl):
    return jnp.sum(jnp.where(incl, _row_of(col, eye), 0.0), axis=1, keepdims=True)


def _per_row(vals, rows):
    return jnp.concatenate([jnp.broadcast_to(v, (rows, 1)) for v in vals], axis=0)


def _block_diag(x, rblk, n):
    return jnp.concatenate([jnp.where(rblk == c, x, 0.0) for c in range(n)], axis=1)


def _conv_silu(hist_ref, s, cw_ref, col0, rows):
    cols = slice(col0, col0 + DH)
    base = SUBLANES - GDN_CONV + 1
    acc = hist_ref[s, pl.ds(base, rows), cols] * cw_ref[0:1, cols]
    for j in range(1, GDN_CONV):
        acc = acc + hist_ref[s, pl.ds(base + j, rows), cols] * cw_ref[j:j + 1, cols]
    return acc * jax.nn.sigmoid(acc)


def _seq_loader(ref, scr, nseq, valid, rows, base=0):
    if valid == rows:
        return lambda s, cols: ref[pl.ds(base + s * rows, rows), cols]
    assert base == 0
    for s in range(nseq):
        scr[s, pl.ds(0, valid), :] = ref[pl.ds(s * valid, valid), :]
        scr[s, pl.ds(valid, rows - valid), :] = jnp.zeros((rows - valid, scr.shape[-1]), F32)
    return lambda s, cols: scr[s, :, cols]


def _mix_dtype(valid):
    return BF16 if valid % (2 * SUBLANES) == 0 else F32


def _problems(nseq, gsz):
    probs = [(s, h) for s in range(nseq) for h in range(HEADS)]
    return [probs[i:i + gsz] for i in range(0, len(probs), gsz)]


def _gdn_kernel(*refs, nseq, rows, valid, levels, gsz, cps, ncast):
    qkv_ref, z_ref, gates_ref, cst_ref, s0_ref, cw_ref, nw_ref = refs[:7]
    cast_in = refs[7:7 + ncast]
    o_ref, s_ref, cst_out_ref = refs[7 + ncast:10 + ncast]
    cast_out = refs[10 + ncast:10 + 2 * ncast]
    hist, pads = refs[10 + 2 * ncast], refs[11 + 2 * ncast:]
    keep = GDN_CONV - 1

    for src, dst in zip(cast_in, cast_out):
        dst[...] = src[...].astype(BF16)

    @pl.when(pl.program_id(1) == 0)
    def _():
        hist[:, pl.ds(SUBLANES - keep, keep), :] = cst_ref[...]
        s_ref[...] = s0_ref[...]

    r = gsz * rows
    eye, incl, strict = _group_masks(gsz, rows)
    rvalid = (lax.broadcasted_iota(jnp.int32, (r, 1), 0) & (rows - 1)) < valid
    masks = (eye, incl, strict, eye.astype(F32), rvalid, _row_block(r, rows, gsz))
    for ck in range(cps):
        _gdn_chunk(ck * nseq * valid, qkv_ref, z_ref, gates_ref, cw_ref, nw_ref, o_ref, s_ref, hist, pads, masks,
                   nseq=nseq, rows=rows, valid=valid, levels=levels, gsz=gsz)
    cst_out_ref[...] = hist[:, pl.ds(SUBLANES - keep, keep), :]


def _gdn_chunk(base, qkv_ref, z_ref, gates_ref, cw_ref, nw_ref, o_ref, s_ref, hist, pads, masks,
               *, nseq, rows, valid, levels, gsz):
    eye, incl, strict, eye_f, rvalid, rblk = masks
    keep = GDN_CONV - 1
    r = gsz * rows
    for s in range(nseq):
        hist[s, pl.ds(SUBLANES, valid), :] = qkv_ref[pl.ds(base + s * valid, valid), :]
        if valid < rows:
            hist[s, pl.ds(SUBLANES + valid, rows - valid), :] = jnp.zeros((rows - valid, QKV), F32)
    zpad, gpad = pads if pads else (None, None)
    load_z = _seq_loader(z_ref, zpad, nseq, valid, rows, base)
    load_g = _seq_loader(gates_ref, gpad, nseq, valid, rows, base)

    for group in _problems(nseq, gsz):
        def stack(fn):
            return jnp.concatenate([fn(s, h) for s, h in group], axis=0)

        q = stack(lambda s, h: _conv_silu(hist, s, cw_ref, h * DH, rows))
        k = stack(lambda s, h: _conv_silu(hist, s, cw_ref, QK + h * DH, rows))
        v = stack(lambda s, h: _conv_silu(hist, s, cw_ref, 2 * QK + h * DH, rows))
        q = q * lax.rsqrt(jnp.sum(q * q, axis=-1, keepdims=True) + EPS) * (DH ** -0.5)
        k = k * lax.rsqrt(jnp.sum(k * k, axis=-1, keepdims=True) + EPS)
        g = jnp.where(rvalid, stack(lambda s, h: load_g(s, slice(h, h + 1))), 0.0)
        beta = jnp.where(rvalid, stack(lambda s, h: load_g(s, slice(HEADS + h, HEADS + h + 1))), 0.0)

        gc = _cumsum_col(g, eye, incl)
        gr = _row_of(gc, eye)
        decay = jnp.where(incl, jnp.exp(jnp.where(incl, gc - gr, 0.0)), 0.0)
        a = jnp.where(strict, beta * _dot_nt(k, k) * decay, 0.0)
        qk = _dot_nt(q, k) * decay
        bk = -a
        t = eye_f + bk
        if levels >= 2:
            bk = _dot(bk, bk)
            for _ in range(2, levels):
                t, bk = t + _dot(t, bk), _dot(bk, bk)
            t = t + _dot(t, bk)
        eg = jnp.exp(gc)
        uw = _dot(t, jnp.concatenate([v * beta, k * (beta * eg)], axis=1))
        u, w = uw[:, :DH], uw[:, DH:]

        s_old = [s_ref[s, h] for s, h in group]
        s_stack = jnp.concatenate(s_old, axis=0)
        v_new = u - _dot(_block_diag(w, rblk, gsz), s_stack)
        o = _dot(_block_diag(q * eg, rblk, gsz), s_stack) + _dot(qk, v_new)
        g_last = [gc[(i + 1) * rows - 1:(i + 1) * rows, :] for i in range(gsz)]
        kd = k * jnp.exp(_per_row(g_last, rows) - gc)
        if rows >= CHUNK:
            s_upd = [_dot_tn(kd[i * rows:(i + 1) * rows], v_new[i * rows:(i + 1) * rows]) for i in range(gsz)]
        else:
            s_all = _dot_tn(_block_diag(kd, rblk, gsz), v_new)
            s_upd = [s_all[i * DH:(i + 1) * DH] for i in range(gsz)]
        for i, (s, h) in enumerate(group):
            s_ref[s, h] = s_old[i] * jnp.exp(g_last[i]) + s_upd[i]

        zs = stack(lambda s, h: load_z(s, slice(h * DH, (h + 1) * DH)))
        on = _rms(o, nw_ref[...]) * (zs * jax.nn.sigmoid(zs))
        for i, (s, h) in enumerate(group):
            o_ref[pl.ds(base + s * valid, valid), h * DH:(h + 1) * DH] = on[i * rows:i * rows + valid].astype(o_ref.dtype)

    for s in range(nseq):
        hist[s, pl.ds(SUBLANES - keep, keep), :] = hist[s, pl.ds(SUBLANES + valid - keep, keep), :]


def _gdn(big, gates, conv_state, s0, conv_w, norm_w, casts=(), *, nb, nc, nseq, rows, valid, gsz, cps):
    assert nc % cps == 0 and (cps == 1 or nseq == 1)
    nc = nc // cps
    levels = max(1, math.ceil(math.log2(valid)))
    kern = functools.partial(_gdn_kernel, nseq=nseq, rows=rows, valid=valid, levels=levels, gsz=gsz, cps=cps,
                             ncast=len(casts))
    br = nseq * valid * cps
    scratch = [pltpu.VMEM((nseq, SUBLANES + rows, QKV), F32)]
    if valid < rows:
        scratch += [pltpu.VMEM((nseq, rows, QK), F32), pltpu.VMEM((nseq, rows, N_GATES), F32)]
    steps = nb // nseq * nc
    bf16_rows = 2 * SUBLANES
    assert all(a.shape[0] % (steps * bf16_rows) == 0 for a in casts)
    cast_specs = [pl.BlockSpec((a.shape[0] // steps, a.shape[1]), lambda b, c: (b * nc + c, 0)) for a in casts]
    return pl.pallas_call(
        kern,
        grid=(nb // nseq, nc),
        in_specs=[
            pl.BlockSpec((br, QKV), lambda b, c: (b * nc + c, 0)),
            pl.BlockSpec((br, QK), lambda b, c: (b * nc + c, 2 * QKV // QK)),
            pl.BlockSpec((br, N_GATES), lambda b, c: (b * nc + c, 0)),
            pl.BlockSpec((nseq, GDN_CONV - 1, QKV), lambda b, c: (b, 0, 0)),
            pl.BlockSpec((nseq, HEADS, DH, DH), lambda b, c: (b, 0, 0, 0)),
            pl.BlockSpec((GDN_CONV, QKV), lambda b, c: (0, 0)),
            pl.BlockSpec((1, DH), lambda b, c: (0, 0)),
            *cast_specs,
        ],
        out_specs=[
            pl.BlockSpec((br, QK), lambda b, c: (b * nc + c, 0)),
            pl.BlockSpec((nseq, HEADS, DH, DH), lambda b, c: (b, 0, 0, 0)),
            pl.BlockSpec((nseq, GDN_CONV - 1, QKV), lambda b, c: (b, 0, 0)),
            *cast_specs,
        ],
        out_shape=[jax.ShapeDtypeStruct((nb // nseq * nc * br, QK), _mix_dtype(valid)),
                   jax.ShapeDtypeStruct((nb, HEADS, DH, DH), F32),
                   jax.ShapeDtypeStruct((nb, GDN_CONV - 1, QKV), F32),
                   *[jax.ShapeDtypeStruct(a.shape, BF16) for a in casts]],
        scratch_shapes=scratch,
        compiler_params=_params("parallel", "arbitrary"),
        name="gdn",
    )(big, big, gates, conv_state, s0, conv_w, norm_w, *casts)


def _mlstm_kernel(qkv_ref, og_ref, gates_ref, c0_ref, n0_ref, m0_ref, h_ref, c_ref, n_ref, m_ref, *pads,
                  nseq, rows, valid, gsz):
    @pl.when(pl.program_id(1) == 0)
    def _():
        c_ref[...] = c0_ref[...]
        n_ref[...] = n0_ref[...]
        m_ref[...] = m0_ref[...]

    r = gsz * rows
    eye, incl, _ = _group_masks(gsz, rows)
    rvalid = (lax.broadcasted_iota(jnp.int32, (r, 1), 0) & (rows - 1)) < valid
    rblk = _row_block(r, rows, gsz)
    xpad, opad, gpad = pads if pads else (None, None, None)
    load_x = _seq_loader(qkv_ref, xpad, nseq, valid, rows)
    load_o = _seq_loader(og_ref, opad, nseq, valid, rows)
    load_g = _seq_loader(gates_ref, gpad, nseq, valid, rows)
    m_all = m_ref[...]

    for group in _problems(nseq, gsz):
        def stack(fn):
            return jnp.concatenate([fn(s, h) for s, h in group], axis=0)

        q = stack(lambda s, h: load_x(s, slice(h * DH, (h + 1) * DH)))
        k = stack(lambda s, h: load_x(s, slice(QK + h * DH, QK + (h + 1) * DH))) * (DH ** -0.5)
        v = stack(lambda s, h: load_x(s, slice(2 * QK + h * DH, 2 * QK + (h + 1) * DH)))
        ig = jnp.where(rvalid, stack(lambda s, h: load_g(s, slice(2 * HEADS + h, 2 * HEADS + h + 1))), NEG)
        lf = jnp.where(rvalid, stack(lambda s, h: load_g(s, slice(3 * HEADS + h, 3 * HEADS + h + 1))), 0.0)

        bc = _cumsum_col(lf, eye, incl)
        br = _row_of(bc, eye)
        igr = _row_of(ig, eye)
        d_log = jnp.where(incl, bc - br + igr, NEG)
        d_max = jnp.max(d_log, axis=1, keepdims=True)
        b_last = [bc[(i + 1) * rows - 1:(i + 1) * rows, :] for i in range(gsz)]
        e_log = _per_row(b_last, rows) - bc + ig
        e_max = [jnp.max(e_log[i * rows:(i + 1) * rows], axis=0, keepdims=True) for i in range(gsz)]
        qk = _dot_nt(q, k)

        c_old = [c_ref[s, h] for s, h in group]
        n_old = [n_ref[s, h:h + 1, :] for s, h in group]
        m_old = [m_all[s, :, h:h + 1] for s, h in group]
        inter = bc + _per_row(m_old, rows)
        mt = jnp.maximum(inter, d_max)
        wi = jnp.exp(inter - mt)
        p = jnp.where(incl, jnp.exp(d_log - mt), 0.0) * qk
        qc = _dot_nt(_block_diag(q, rblk, gsz), jnp.concatenate(c_old, axis=1))
        num = wi * qc + _dot(p, v)
        n_rows = jnp.concatenate([jnp.broadcast_to(nv, (rows, DH)) for nv in n_old], axis=0)
        den = wi * jnp.sum(q * n_rows, axis=-1, keepdims=True) + jnp.sum(p, axis=-1, keepdims=True)
        hv = num / jnp.maximum(jnp.abs(den), jnp.exp(-mt))

        m_new = [jnp.maximum(b_last[i] + m_old[i], e_max[i]) for i in range(gsz)]
        fw = [jnp.exp(b_last[i] + m_old[i] - m_new[i]) for i in range(gsz)]
        sw = jnp.exp(e_log - _per_row(m_new, rows))
        c_upd = _dot_tn(sw * v, _block_diag(k, rblk, gsz))
        swk = sw * k
        for i, (s, h) in enumerate(group):
            c_ref[s, h] = fw[i] * c_old[i] + c_upd[:, i * DH:(i + 1) * DH]
            n_ref[s, h:h + 1, :] = fw[i] * n_old[i] + jnp.sum(swk[i * rows:(i + 1) * rows], axis=0, keepdims=True)
            m_ref[s, :, h:h + 1] = m_new[i]

        og = stack(lambda s, h: load_o(s, slice(h * DH, (h + 1) * DH)))
        hg = hv * jax.nn.sigmoid(og)
        for i, (s, h) in enumerate(group):
            h_ref[pl.ds(s * valid, valid), h * DH:(h + 1) * DH] = hg[i * rows:i * rows + valid].astype(h_ref.dtype)


def _mlstm(big, gates, c0, n0, m0, *, nb, nc, nseq, rows, valid, gsz):
    kern = functools.partial(_mlstm_kernel, nseq=nseq, rows=rows, valid=valid, gsz=gsz)
    br = nseq * valid
    scratch = []
    if valid < rows:
        scratch = [pltpu.VMEM((nseq, rows, QKV), F32), pltpu.VMEM((nseq, rows, QK), F32),
                   pltpu.VMEM((nseq, rows, N_GATES), F32)]
    return pl.pallas_call(
        kern,
        grid=(nb // nseq, nc),
        in_specs=[
            pl.BlockSpec((br, QKV), lambda b, c: (b * nc + c, 1)),
            pl.BlockSpec((br, QK), lambda b, c: (b * nc + c, 2 * QKV // QK + 1)),
            pl.BlockSpec((br, N_GATES), lambda b, c: (b * nc + c, 0)),
            pl.BlockSpec((nseq, HEADS, DH, DH), lambda b, c: (b, 0, 0, 0)),
            pl.BlockSpec((nseq, HEADS, DH), lambda b, c: (b, 0, 0)),
            pl.BlockSpec((nseq, 1, HEADS), lambda b, c: (b, 0, 0)),
        ],
        out_specs=[
            pl.BlockSpec((br, QK), lambda b, c: (b * nc + c, 0)),
            pl.BlockSpec((nseq, HEADS, DH, DH), lambda b, c: (b, 0, 0, 0)),
            pl.BlockSpec((nseq, HEADS, DH), lambda b, c: (b, 0, 0)),
            pl.BlockSpec((nseq, 1, HEADS), lambda b, c: (b, 0, 0)),
        ],
        out_shape=[jax.ShapeDtypeStruct((nb * nc * valid, QK), _mix_dtype(valid)),
                   jax.ShapeDtypeStruct((nb, HEADS, DH, DH), F32),
                   jax.ShapeDtypeStruct((nb, HEADS, DH), F32),
                   jax.ShapeDtypeStruct((nb, 1, HEADS), F32)],
        scratch_shapes=scratch,
        compiler_params=_params("parallel", "arbitrary"),
        name="mlstm",
    )(big, big, gates, c0, n0, m0)


def _merge_kernel(go_ref, mh_ref, ga_ref, gb_ref, x_ref, wa_ref, wb_ref, wo_ref, x1_ref, mix_scr, *, nt, tn):
    j = pl.program_id(1)

    @pl.when(j < nt)
    def _():
        ya = jnp.dot(go_ref[...].astype(BF16), wa_ref[...], preferred_element_type=F32)
        yb = jnp.dot(mh_ref[...].astype(BF16), wb_ref[...], preferred_element_type=F32)
        mixed = jax.nn.sigmoid(ga_ref[...]) * ya + jax.nn.sigmoid(gb_ref[...]) * yb
        mixed = mixed.astype(BF16)
        for t in range(nt):
            @pl.when(j == t)
            def _():
                mix_scr[:, t * tn:(t + 1) * tn] = mixed

    @pl.when(j >= nt)
    def _():
        x1_ref[...] = x_ref[...] + jnp.dot(mix_scr[...], wo_ref[...], preferred_element_type=F32)


def _merge(go, mh, big, x, wa, wb, wo, *, tm, tn):
    m = x.shape[0]
    nt = D_MODEL // tn
    ga_blk = (2 * QKV + 2 * QK) // tn
    mix_j = lambda j: jnp.minimum(j, nt - 1)
    out_j = lambda j: jnp.maximum(j - nt, 0)
    return pl.pallas_call(
        functools.partial(_merge_kernel, nt=nt, tn=tn),
        grid=(m // tm, 2 * nt),
        in_specs=[
            pl.BlockSpec((tm, QK), lambda i, j: (i, 0)),
            pl.BlockSpec((tm, QK), lambda i, j: (i, 0)),
            pl.BlockSpec((tm, tn), lambda i, j: (i, ga_blk + mix_j(j))),
            pl.BlockSpec((tm, tn), lambda i, j: (i, ga_blk + nt + mix_j(j))),
            pl.BlockSpec((tm, tn), lambda i, j: (i, out_j(j))),
            pl.BlockSpec((QK, tn), lambda i, j: (0, mix_j(j))),
            pl.BlockSpec((QK, tn), lambda i, j: (0, mix_j(j))),
            pl.BlockSpec((D_MODEL, tn), lambda i, j: (0, out_j(j))),
        ],
        out_specs=pl.BlockSpec((tm, tn), lambda i, j: (i, out_j(j))),
        out_shape=jax.ShapeDtypeStruct((m, D_MODEL), F32),
        scratch_shapes=[pltpu.VMEM((tm, D_MODEL), BF16)],
        compiler_params=_params("parallel", "arbitrary"),
        name="merge",
    )(go, mh, big, big, x, wa, wb, wo)


def _ffn_kernel(*refs, tm, tf, seq_tiles, with_state, srows):
    keep = FFN_CONV - 1
    if with_state:
        (x1_ref, nw_ref, wg_ref, wv_ref, cwg_ref, cwv_ref, wd_ref, fnw_ref) = refs[:8]
        stg_refs, stv_refs = refs[8:8 + keep], refs[8 + keep:8 + 2 * keep]
        y_ref = refs[8 + 2 * keep]
        newg_refs, newv_refs = refs[9 + 2 * keep:9 + 3 * keep], refs[9 + 3 * keep:9 + 4 * keep]
        h2_scr, hist_g, hist_v, acc_scr, st_g, st_v, cp_g, cp_v = refs[9 + 4 * keep:]
        nsq = tm // srows
        nck = tf // LANES
    else:
        (x1_ref, halo_ref, nw_ref, wg_ref, wv_ref, cwg_ref, cwv_ref, wd_ref, fnw_ref,
         y_ref, upg_ref, upv_ref, h2_scr, hist_g, hist_v, acc_scr) = refs
    i = pl.program_id(0)
    f = pl.program_id(1)
    pad = SUBLANES

    @pl.when(f == 0)
    def _():
        if with_state:
            h2_scr[pl.ds(0, pad), :] = jnp.zeros((pad, D_MODEL), BF16)
        else:
            live = (i % seq_tiles != 0).astype(F32)
            h2_scr[pl.ds(0, pad), :] = (_rms(halo_ref[...], nw_ref[...]) * live).astype(BF16)
        h2_scr[pl.ds(pad, tm), :] = _rms(x1_ref[...], nw_ref[...]).astype(BF16)
        acc_scr[...] = jnp.zeros_like(acc_scr)

    h2 = h2_scr[...]
    hist_g[...] = jnp.dot(h2, wg_ref[...], preferred_element_type=F32)
    hist_v[...] = jnp.dot(h2, wv_ref[...], preferred_element_type=F32)

    if with_state:
        for st, cp, st_refs, hist, new_refs in ((st_g, cp_g, stg_refs, hist_g, newg_refs),
                                                (st_v, cp_v, stv_refs, hist_v, newv_refs)):
            st[...] = jnp.zeros_like(st)
            for c in range(nck):
                lanes = slice(c * LANES, (c + 1) * LANES)
                cp[c] = hist[:, lanes]
                for r in range(keep):
                    st[c, pl.ds(r, nsq, stride=srows), :] = st_refs[r][:, lanes]
                    new_refs[r][:, lanes] = cp[c, pl.ds(pad + srows - keep + r, nsq, stride=srows), :]
        rmod = lax.broadcasted_iota(jnp.int32, (tm, 1), 0) % srows
        slab = lambda st, off: jnp.concatenate([st[c, pl.ds(off, tm), :] for c in range(nck)], axis=1)
    else:
        upg_ref[...] = hist_g[pl.ds(tm, pad), :]
        upv_ref[...] = hist_v[pl.ds(tm, pad), :]

    def conv(hist, cw_ref, st):
        prev2 = hist[pl.ds(pad - 2, tm), :]
        prev1 = hist[pl.ds(pad - 1, tm), :]
        if with_state:
            prev2 = jnp.where(rmod < 2, slab(st, 0), prev2)
            prev1 = jnp.where(rmod < 1, slab(st, 1), prev1)
        return (prev2 * cw_ref[0:1, :] + prev1 * cw_ref[1:2, :]) + hist[pl.ds(pad, tm), :] * cw_ref[2:3, :]

    ug = conv(hist_g, cwg_ref, st_g if with_state else None)
    uv = conv(hist_v, cwv_ref, st_v if with_state else None)
    act = (ug * jax.nn.sigmoid(ug) * uv).astype(BF16)
    acc_scr[...] += jnp.dot(act, wd_ref[...], preferred_element_type=F32)

    @pl.when(f == pl.num_programs(1) - 1)
    def _():
        y_ref[...] = _rms(x1_ref[...] + acc_scr[...], fnw_ref[...])


def _ffn(x1, nw, wup, cw, wd, fnw, state=None, *, tm, tf, seq_tiles, srows=None):
    m = x1.shape[0]
    nf = D_FF // tf
    keep = FFN_CONV - 1
    with_state = state is not None
    kern = functools.partial(_ffn_kernel, tm=tm, tf=tf, seq_tiles=seq_tiles, with_state=with_state, srows=srows)
    in_specs = [pl.BlockSpec((tm, D_MODEL), lambda i, f: (i, 0))]
    args = [x1]
    if not with_state:
        in_specs.append(pl.BlockSpec((SUBLANES, D_MODEL),
                                     lambda i, f: (jnp.maximum(i * (tm // SUBLANES) - 1, 0), 0)))
        args.append(x1)
    in_specs += [
        pl.BlockSpec((1, D_MODEL), lambda i, f: (0, 0)),
        pl.BlockSpec((D_MODEL, tf), lambda i, f: (0, f)),
        pl.BlockSpec((D_MODEL, tf), lambda i, f: (0, f + nf)),
        pl.BlockSpec((FFN_CONV, tf), lambda i, f: (0, f)),
        pl.BlockSpec((FFN_CONV, tf), lambda i, f: (0, f + nf)),
        pl.BlockSpec((tf, D_MODEL), lambda i, f: (f, 0)),
        pl.BlockSpec((1, D_MODEL), lambda i, f: (0, 0)),
    ]
    args += [nw, wup, wup, cw, cw, wd, fnw]
    scratch = [pltpu.VMEM((SUBLANES + tm, D_MODEL), BF16),
               pltpu.VMEM((SUBLANES + tm, tf), F32),
               pltpu.VMEM((SUBLANES + tm, tf), F32),
               pltpu.VMEM((tm, D_MODEL), F32)]
    n_tiles = m // tm
    out_specs = [pl.BlockSpec((tm, D_MODEL), lambda i, f: (i, 0))]
    out_shape = [jax.ShapeDtypeStruct((m, D_MODEL), F32)]
    if with_state:
        nsq = tm // srows
        in_specs += [pl.BlockSpec((nsq, tf), lambda i, f: (i, f))] * keep
        in_specs += [pl.BlockSpec((nsq, tf), lambda i, f: (i, f + nf))] * keep
        args += list(state) * 2
        scratch += [pltpu.VMEM((tf // LANES, tm + SUBLANES, LANES), F32)] * 4
        out_specs += [pl.BlockSpec((nsq, tf), lambda i, f: (i, f))] * (2 * keep)
        out_shape += [jax.ShapeDtypeStruct((m // srows, D_FF), F32)] * (2 * keep)
    else:
        out_specs += [pl.BlockSpec((SUBLANES, tf), lambda i, f: (i, f))] * 2
        out_shape += [jax.ShapeDtypeStruct((n_tiles * SUBLANES, D_FF), F32)] * 2
    outs = pl.pallas_call(
        kern,
        grid=(n_tiles, nf),
        in_specs=in_specs,
        out_specs=out_specs,
        out_shape=out_shape,
        scratch_shapes=scratch,
        compiler_params=_params("parallel", "arbitrary"),
        name="ffn",
    )(*args)
    return outs[0], outs[1:]


def _layer(x, nb, nc, nseq, rows, valid, gdn_group, ml_group, states, w, *, tm_in, tm_merge, tm_ffn, seq_tiles,
           ffn_state):
    conv_state, s0, c0, n0, m0 = states
    big, gates = _inproj(x, w["norm_mix"], w["w_big"], w["w_small"], w["gate_bias"], w["a_log"],
                         tm=tm_in, tn=TN_INPROJ)
    names = [n for n in ("w_a", "w_b", "w_out", "w_up", "w_down") if w[n].dtype != BF16]
    go, s_new, conv_new, *cast = _gdn(big, gates, conv_state, s0, w["gdn_conv_w"], w["gdn_norm"],
                                      [w[n] for n in names], nb=nb, nc=nc, nseq=nseq, rows=rows, valid=valid,
                                      gsz=gdn_group[0], cps=gdn_group[1])
    w = {**w, **dict(zip(names, cast))}
    mh, c_new, n_new, m_new = _mlstm(big, gates, c0, n0, m0,
                                     nb=nb, nc=nc, nseq=nseq, rows=rows, valid=valid, gsz=ml_group)
    x1 = _merge(go, mh, big, x, w["w_a"], w["w_b"], w["w_out"], tm=tm_merge, tn=TN_MERGE)
    y, ffn_rows = _ffn(x1, w["norm_ffn"], w["w_up"], w["ffn_conv_w"], w["w_down"], w["norm_final"],
                       ffn_state, tm=tm_ffn, tf=TF_FFN, seq_tiles=seq_tiles, srows=valid)
    return y, (conv_new, s_new, c_new, n_new, m_new.reshape(nb, HEADS)), ffn_rows, w


def kernel(x_prompt, x_sample, state_gdn_conv, state_gdn_S, state_ml_C, state_ml_n, state_ml_m, state_ffn_conv,
           norm_mix_w, w_in, gdn_conv_w, gdn_A_log, gdn_dt_bias, gdn_norm_w, w_branch_a, ml_b_i, ml_b_f,
           w_branch_b, w_out, norm_ffn_w, w_up, ffn_conv_w, w_down, norm_final_w):
    assert w_in.shape[0] == 1, "single-layer step"
    bp, tp, _ = x_prompt.shape
    bs, ts, _ = x_sample.shape
    assert FFN_CONV - 1 <= ts <= SUBLANES and GDN_CONV - 1 <= ts and tp % CHUNK == 0
    assert w_in.shape[1:] == (D_MODEL, IN_COLS)
    keep = FFN_CONV - 1

    w_big, w_small = _repack(w_in[0].T, tn=TN_REPACK)
    zeros8 = jnp.zeros((HEADS,), F32)
    w = {
        "norm_mix": norm_mix_w[0][None, :],
        "w_big": w_big,
        "w_small": w_small,
        "gate_bias": jnp.concatenate([gdn_dt_bias[0], zeros8, ml_b_i[0], ml_b_f[0]])[None, :],
        "a_log": jnp.concatenate([gdn_A_log[0], zeros8, zeros8, zeros8])[None, :],
        "gdn_conv_w": gdn_conv_w[0],
        "gdn_norm": gdn_norm_w[0][None, :],
        "w_a": w_branch_a[0],
        "w_b": w_branch_b[0],
        "w_out": w_out[0],
        "norm_ffn": norm_ffn_w[0][None, :],
        "w_up": w_up[0],
        "ffn_conv_w": ffn_conv_w[0],
        "w_down": w_down[0],
        "norm_final": norm_final_w[None, :],
    }

    xp = x_prompt.reshape(bp * tp, D_MODEL)
    p_states = (jnp.zeros((bp, GDN_CONV - 1, QKV), F32), jnp.zeros((bp, HEADS, DH, DH), F32),
                jnp.zeros((bp, HEADS, DH, DH), F32), jnp.zeros((bp, HEADS, DH), F32),
                jnp.zeros((bp, 1, HEADS), F32))
    assert tp % TM_FFN == 0 and (bp * tp) % TM_ROWS == 0
    yp, p_new, up_p, w = _layer(
        xp, bp, tp // CHUNK, 1, CHUNK, CHUNK, PROMPT_GDN_GROUP, PROMPT_ML_GROUP, p_states, w,
        tm_in=TM_ROWS, tm_merge=TM_ROWS, tm_ffn=TM_FFN, seq_tiles=tp // TM_FFN, ffn_state=None)
    p_ffn_conv = jnp.concatenate(
        [u.reshape(bp, tp // TM_FFN, SUBLANES, D_FF)[:, -1, SUBLANES - keep:, :] for u in up_p], axis=-1)

    xs = x_sample.reshape(bs * ts, D_MODEL)
    s_states = (state_gdn_conv[0], state_gdn_S[0], state_ml_C[0], state_ml_n[0],
                state_ml_m[0].reshape(bs, 1, HEADS))
    sample_rows = min(bs * ts, TM_ROWS)
    assert (bs * ts) % sample_rows == 0 and bs % SAMPLE_SEQS == 0
    ys, s_new, new_rows, _ = _layer(
        xs, bs, 1, SAMPLE_SEQS, SUBLANES, ts, (SAMPLE_GROUP, 1), SAMPLE_GROUP, s_states, w,
        tm_in=sample_rows, tm_merge=sample_rows, tm_ffn=sample_rows, seq_tiles=1,
        ffn_state=[state_ffn_conv[0, :, r, :] for r in range(keep)])
    s_ffn_conv = jnp.stack([jnp.concatenate([new_rows[r], new_rows[keep + r]], axis=-1) for r in range(keep)], axis=1)

    lead = lambda t: tuple(a[None] for a in t)
    return (yp.reshape(bp, tp, D_MODEL), ys.reshape(bs, ts, D_MODEL),
            *lead(p_new), p_ffn_conv[None], *lead(s_new), s_ffn_conv[None])
```

```python
import functools
import math

import jax
import jax.numpy as jnp
from jax import lax
from jax.experimental import pallas as pl
from jax.experimental.pallas import tpu as pltpu

F32 = jnp.float32
BF16 = jnp.bfloat16

D_MODEL = 2048
HEADS = 8
DH = 128
QK = HEADS * DH
QKV = 3 * QK
D_FF = 5632
GDN_CONV = 4
FFN_CONV = 3
EPS = 1e-6
CHUNK = 64
N_GATES = 4 * HEADS
SUBLANES = 8
NEG = -1e30

BIG_COLS = 2 * QKV + 2 * QK + 2 * D_MODEL

VMEM_LIMIT = 56 * 1024 * 1024


TM_ROWS = 1024
TN_INPROJ = 1536
TN_MERGE = 512
TM_FFN = 512
TF_FFN = 512
TN_REPACK = 512
PROMPT_GDN_GROUP = (4, 4)
PROMPT_ML_GROUP = HEADS
SAMPLE_SEQS = 8
SAMPLE_GROUP = 2 * HEADS


def _params(*sem):
    return pltpu.CompilerParams(dimension_semantics=sem, vmem_limit_bytes=VMEM_LIMIT)


def _dot(a, b):
    return jnp.dot(a.astype(BF16), b.astype(BF16), preferred_element_type=F32)


def _dot_nt(a, b):
    return lax.dot_general(a.astype(BF16), b.astype(BF16), (((1,), (1,)), ((), ())),
                           preferred_element_type=F32)


def _dot_tn(a, b):
    return lax.dot_general(a.astype(BF16), b.astype(BF16), (((0,), (0,)), ((), ())),
                           preferred_element_type=F32)


def _softplus(x):
    return jnp.maximum(x, 0.0) + jnp.log1p(jnp.exp(-jnp.abs(x)))


def _rms(x, w):
    return x * lax.rsqrt(jnp.mean(x * x, axis=-1, keepdims=True) + EPS) * w


IN_SECTIONS = (("gqkv", QKV), ("gz", QK), ("ga", HEADS), ("gb", HEADS), ("mqkv", QKV),
               ("mi", HEADS), ("mf", HEADS), ("mo", QK), ("gA", D_MODEL), ("gB", D_MODEL))
BIG_ORDER = ("gqkv", "mqkv", "gz", "mo", "gA", "gB")
IN_COLS = sum(size for _, size in IN_SECTIONS)


def _in_start(name):
    off = 0
    for n, size in IN_SECTIONS:
        if n == name:
            return off
        off += size
    raise KeyError(name)


def _big_tiles(tn):
    starts = []
    for name in BIG_ORDER:
        size = dict(IN_SECTIONS)[name]
        assert size % tn == 0
        starts += [_in_start(name) + t * tn for t in range(size // tn)]
    return starts


def _tile_lookup(j, values):
    out = jnp.int32(values[-1])
    for t in range(len(values) - 2, -1, -1):
        out = jnp.where(j <= t, jnp.int32(values[t]), out)
    return out


LANES = 128
SMALL_ORDER = ("ga", "gb", "mi", "mf")


def _repack_kernel(*refs):
    wt_ref, g_refs, out_ref, small_ref = refs[0], refs[1:1 + len(SMALL_ORDER)], refs[-2], refs[-1]

    @pl.when(pl.program_id(0) == 0)
    def _():
        cols = [g_ref[...].T[:, _in_start(n) % SUBLANES:_in_start(n) % SUBLANES + HEADS]
                for n, g_ref in zip(SMALL_ORDER, g_refs)]
        small_ref[...] = jnp.concatenate(cols, axis=1).astype(BF16)

    out_ref[...] = wt_ref[...].T.astype(BF16)


def _repack(w_in_t, *, tn):
    starts = _big_tiles(tn)
    assert all(s % SUBLANES == 0 for s in starts)
    rows8 = [s // SUBLANES for s in starts]
    return pl.pallas_call(
        _repack_kernel,
        grid=(BIG_COLS // tn,),
        in_specs=[pl.BlockSpec((pl.Element(tn), pl.Element(D_MODEL)),
                               lambda j: (_tile_lookup(j, rows8) * SUBLANES, 0))]
        + [pl.BlockSpec((SUBLANES, D_MODEL), lambda j, b=_in_start(n) // SUBLANES: (b, 0)) for n in SMALL_ORDER],
        out_specs=[pl.BlockSpec((D_MODEL, tn), lambda j: (0, j)),
                   pl.BlockSpec((D_MODEL, N_GATES), lambda j: (0, 0))],
        out_shape=[jax.ShapeDtypeStruct((D_MODEL, BIG_COLS), BF16),
                   jax.ShapeDtypeStruct((D_MODEL, N_GATES), BF16)],
        compiler_params=_params("arbitrary"),
        name="repack",
    )(*([w_in_t] * (1 + len(SMALL_ORDER))))


def _inproj_kernel(x_ref, nw_ref, wbig_ref, wsm_ref, gbias_ref, alog_ref, big_ref, gates_ref, h_scr):
    @pl.when(pl.program_id(1) == 0)
    def _():
        hb = _rms(x_ref[...], nw_ref[...]).astype(BF16)
        h_scr[...] = hb
        raw = jnp.dot(hb, wsm_ref[...], preferred_element_type=F32)
        lane = lax.broadcasted_iota(jnp.int32, raw.shape, 1)
        z = raw + gbias_ref[...]
        g = -jnp.exp(alog_ref[...]) * _softplus(z)
        beta = jax.nn.sigmoid(raw)
        lf = -_softplus(-z)
        gates_ref[...] = jnp.where(lane < HEADS, g,
                                   jnp.where(lane < 2 * HEADS, beta,
                                             jnp.where(lane < 3 * HEADS, z, lf)))

    big_ref[...] = jnp.dot(h_scr[...], wbig_ref[...], preferred_element_type=F32)


def _inproj(x, nw, wbig, wsm, gbias, alog, *, tm, tn):
    m = x.shape[0]
    return pl.pallas_call(
        _inproj_kernel,
        grid=(m // tm, BIG_COLS // tn),
        in_specs=[
            pl.BlockSpec((tm, D_MODEL), lambda i, j: (i, 0)),
            pl.BlockSpec((1, D_MODEL), lambda i, j: (0, 0)),
            pl.BlockSpec((D_MODEL, tn), lambda i, j: (0, j)),
            pl.BlockSpec((D_MODEL, N_GATES), lambda i, j: (0, 0)),
            pl.BlockSpec((1, N_GATES), lambda i, j: (0, 0)),
            pl.BlockSpec((1, N_GATES), lambda i, j: (0, 0)),
        ],
        out_specs=[
            pl.BlockSpec((tm, tn), lambda i, j: (i, j)),
            pl.BlockSpec((tm, N_GATES), lambda i, j: (i, 0)),
        ],
        out_shape=[jax.ShapeDtypeStruct((m, BIG_COLS), F32),
                   jax.ShapeDtypeStruct((m, N_GATES), F32)],
        scratch_shapes=[pltpu.VMEM((tm, D_MODEL), BF16)],
        compiler_params=_params("parallel", "arbitrary"),
        name="inproj",
    )(x, nw, wbig, wsm, gbias, alog)


def _group_masks(n, rows):
    r = n * rows
    shift = rows.bit_length() - 1
    ri = lax.broadcasted_iota(jnp.int32, (r, r), 0)
    ci = lax.broadcasted_iota(jnp.int32, (r, r), 1)
    same = lax.shift_right_logical(ri, shift) == lax.shift_right_logical(ci, shift)
    return ri == ci, same & (ci <= ri), same & (ci < ri)


def _row_block(nrows, rows, n):
    shift = rows.bit_length() - 1
    ri = lax.broadcasted_iota(jnp.int32, (nrows, 1), 0)
    return lax.shift_right_logical(ri, shift) & (n - 1)


def _row_of(col, eye):
    return jnp.sum(jnp.where(eye, col, 0.0), axis=0, keepdims=True)


def _cumsum_col(col, eye, incl):
    return jnp.sum(jnp.where(incl, _row_of(col, eye), 0.0), axis=1, keepdims=True)


def _per_row(vals, rows):
    return jnp.concatenate([jnp.broadcast_to(v, (rows, 1)) for v in vals], axis=0)


def _block_diag(x, rblk, n):
    return jnp.concatenate([jnp.where(rblk == c, x, 0.0) for c in range(n)], axis=1)


def _conv_silu(hist_ref, s, cw_ref, col0, rows):
    cols = slice(col0, col0 + DH)
    base = SUBLANES - GDN_CONV + 1
    acc = hist_ref[s, pl.ds(base, rows), cols] * cw_ref[0:1, cols]
    for j in range(1, GDN_CONV):
        acc = acc + hist_ref[s, pl.ds(base + j, rows), cols] * cw_ref[j:j + 1, cols]
    return acc * jax.nn.sigmoid(acc)


def _seq_loader(ref, scr, nseq, valid, rows, base=0):
    if valid == rows:
        return lambda s, cols: ref[pl.ds(base + s * rows, rows), cols]
    assert base == 0
    for s in range(nseq):
        scr[s, pl.ds(0, valid), :] = ref[pl.ds(s * valid, valid), :]
        scr[s, pl.ds(valid, rows - valid), :] = jnp.zeros((rows - valid, scr.shape[-1]), F32)
    return lambda s, cols: scr[s, :, cols]


def _mix_dtype(valid):
    return BF16 if valid % (2 * SUBLANES) == 0 else F32


def _problems(nseq, gsz):
    probs = [(s, h) for s in range(nseq) for h in range(HEADS)]
    return [probs[i:i + gsz] for i in range(0, len(probs), gsz)]


def _gdn_kernel(*refs, nseq, rows, valid, levels, gsz, cps, ncast):
    qkv_ref, z_ref, gates_ref, cst_ref, s0_ref, cw_ref, nw_ref = refs[:7]
    cast_in = refs[7:7 + ncast]
    o_ref, s_ref, cst_out_ref = refs[7 + ncast:10 + ncast]
    cast_out = refs[10 + ncast:10 + 2 * ncast]
    hist, pads = refs[10 + 2 * ncast], refs[11 + 2 * ncast:]
    keep = GDN_CONV - 1

    for src, dst in zip(cast_in, cast_out):
        dst[...] = src[...].astype(BF16)

    @pl.when(pl.program_id(1) == 0)
    def _():
        hist[:, pl.ds(SUBLANES - keep, keep), :] = cst_ref[...]
        s_ref[...] = s0_ref[...]

    masks = _gdn_masks(gsz, rows, valid)
    for ck in range(cps):
        _gdn_chunk(ck * nseq * valid, qkv_ref, z_ref, gates_ref, cw_ref, nw_ref, o_ref, s_ref, hist, pads, masks,
                   nseq=nseq, rows=rows, valid=valid, levels=levels, gsz=gsz)
    cst_out_ref[...] = hist[:, pl.ds(SUBLANES - keep, keep), :]


def _gdn_masks(gsz, rows, valid):
    r = gsz * rows
    eye, incl, strict = _group_masks(gsz, rows)
    rvalid = (lax.broadcasted_iota(jnp.int32, (r, 1), 0) & (rows - 1)) < valid
    return eye, incl, strict, eye.astype(F32), rvalid, _row_block(r, rows, gsz)


def _gdn_chunk(base, qkv_ref, z_ref, gates_ref, cw_ref, nw_ref, o_ref, s_ref, hist, pads, masks,
               *, nseq, rows, valid, levels, gsz):
    eye, incl, strict, eye_f, rvalid, rblk = masks
    keep = GDN_CONV - 1
    r = gsz * rows
    for s in range(nseq):
        hist[s, pl.ds(SUBLANES, valid), :] = qkv_ref[pl.ds(base + s * valid, valid), :]
        if valid < rows:
            hist[s, pl.ds(SUBLANES + valid, rows - valid), :] = jnp.zeros((rows - valid, QKV), F32)
    zpad, gpad = pads if pads else (None, None)
    load_z = _seq_loader(z_ref, zpad, nseq, valid, rows, base)
    load_g = _seq_loader(gates_ref, gpad, nseq, valid, rows, base)

    for group in _problems(nseq, gsz):
        def stack(fn):
            return jnp.concatenate([fn(s, h) for s, h in group], axis=0)

        q = stack(lambda s, h: _conv_silu(hist, s, cw_ref, h * DH, rows))
        k = stack(lambda s, h: _conv_silu(hist, s, cw_ref, QK + h * DH, rows))
        v = stack(lambda s, h: _conv_silu(hist, s, cw_ref, 2 * QK + h * DH, rows))
        q = q * lax.rsqrt(jnp.sum(q * q, axis=-1, keepdims=True) + EPS) * (DH ** -0.5)
        k = k * lax.rsqrt(jnp.sum(k * k, axis=-1, keepdims=True) + EPS)
        g = jnp.where(rvalid, stack(lambda s, h: load_g(s, slice(h, h + 1))), 0.0)
        beta = jnp.where(rvalid, stack(lambda s, h: load_g(s, slice(HEADS + h, HEADS + h + 1))), 0.0)

        gc = _cumsum_col(g, eye, incl)
        gr = _row_of(gc, eye)
        decay = jnp.where(incl, jnp.exp(jnp.where(incl, gc - gr, 0.0)), 0.0)
        a = jnp.where(strict, beta * _dot_nt(k, k) * decay, 0.0)
        qk = _dot_nt(q, k) * decay
        bk = -a
        t = eye_f + bk
        if levels >= 2:
            bk = _dot(bk, bk)
            for _ in range(2, levels):
                t, bk = t + _dot(t, bk), _dot(bk, bk)
            t = t + _dot(t, bk)
        eg = jnp.exp(gc)
        uw = _dot(t, jnp.concatenate([v * beta, k * (beta * eg)], axis=1))
        u, w = uw[:, :DH], uw[:, DH:]

        s_old = [s_ref[s, h] for s, h in group]
        s_stack = jnp.concatenate(s_old, axis=0)
        v_new = u - _dot(_block_diag(w, rblk, gsz), s_stack)
        o = _dot(_block_diag(q * eg, rblk, gsz), s_stack) + _dot(qk, v_new)
        g_last = [gc[(i + 1) * rows - 1:(i + 1) * rows, :] for i in range(gsz)]
        kd = k * jnp.exp(_per_row(g_last, rows) - gc)
        if rows >= CHUNK:
            s_upd = [_dot_tn(kd[i * rows:(i + 1) * rows], v_new[i * rows:(i + 1) * rows]) for i in range(gsz)]
        else:
            s_all = _dot_tn(_block_diag(kd, rblk, gsz), v_new)
            s_upd = [s_all[i * DH:(i + 1) * DH] for i in range(gsz)]
        for i, (s, h) in enumerate(group):
            s_ref[s, h] = s_old[i] * jnp.exp(g_last[i]) + s_upd[i]

        zs = stack(lambda s, h: load_z(s, slice(h * DH, (h + 1) * DH)))
        on = _rms(o, nw_ref[...]) * (zs * jax.nn.sigmoid(zs))
        for i, (s, h) in enumerate(group):
            o_ref[pl.ds(base + s * valid, valid), h * DH:(h + 1) * DH] = on[i * rows:i * rows + valid].astype(o_ref.dtype)

    for s in range(nseq):
        hist[s, pl.ds(SUBLANES - keep, keep), :] = hist[s, pl.ds(SUBLANES + valid - keep, keep), :]


def _gdn(big, gates, conv_state, s0, conv_w, norm_w, casts=(), *, nb, nc, nseq, rows, valid, gsz, cps):
    assert nc % cps == 0 and (cps == 1 or nseq == 1)
    nc = nc // cps
    levels = max(1, math.ceil(math.log2(valid)))
    kern = functools.partial(_gdn_kernel, nseq=nseq, rows=rows, valid=valid, levels=levels, gsz=gsz, cps=cps,
                             ncast=len(casts))
    br = nseq * valid * cps
    scratch = [pltpu.VMEM((nseq, SUBLANES + rows, QKV), F32)]
    if valid < rows:
        scratch += [pltpu.VMEM((nseq, rows, QK), F32), pltpu.VMEM((nseq, rows, N_GATES), F32)]
    steps = nb // nseq * nc
    bf16_rows = 2 * SUBLANES
    assert all(a.shape[0] % (steps * bf16_rows) == 0 for a in casts)
    cast_specs = [pl.BlockSpec((a.shape[0] // steps, a.shape[1]), lambda b, c: (b * nc + c, 0)) for a in casts]
    return pl.pallas_call(
        kern,
        grid=(nb // nseq, nc),
        in_specs=[
            pl.BlockSpec((br, QKV), lambda b, c: (b * nc + c, 0)),
            pl.BlockSpec((br, QK), lambda b, c: (b * nc + c, 2 * QKV // QK)),
            pl.BlockSpec((br, N_GATES), lambda b, c: (b * nc + c, 0)),
            pl.BlockSpec((nseq, GDN_CONV - 1, QKV), lambda b, c: (b, 0, 0)),
            pl.BlockSpec((nseq, HEADS, DH, DH), lambda b, c: (b, 0, 0, 0)),
            pl.BlockSpec((GDN_CONV, QKV), lambda b, c: (0, 0)),
            pl.BlockSpec((1, DH), lambda b, c: (0, 0)),
            *cast_specs,
        ],
        out_specs=[
            pl.BlockSpec((br, QK), lambda b, c: (b * nc + c, 0)),
            pl.BlockSpec((nseq, HEADS, DH, DH), lambda b, c: (b, 0, 0, 0)),
            pl.BlockSpec((nseq, GDN_CONV - 1, QKV), lambda b, c: (b, 0, 0)),
            *cast_specs,
        ],
        out_shape=[jax.ShapeDtypeStruct((nb // nseq * nc * br, QK), _mix_dtype(valid)),
                   jax.ShapeDtypeStruct((nb, HEADS, DH, DH), F32),
                   jax.ShapeDtypeStruct((nb, GDN_CONV - 1, QKV), F32),
                   *[jax.ShapeDtypeStruct(a.shape, BF16) for a in casts]],
        scratch_shapes=scratch,
        compiler_params=_params("parallel", "arbitrary"),
        name="gdn",
    )(big, big, gates, conv_state, s0, conv_w, norm_w, *casts)


def _mlstm_kernel(qkv_ref, og_ref, gates_ref, c0_ref, n0_ref, m0_ref, h_ref, c_ref, n_ref, m_ref, *pads,
                  nseq, rows, valid, gsz):
    @pl.when(pl.program_id(1) == 0)
    def _():
        c_ref[...] = c0_ref[...]
        n_ref[...] = n0_ref[...]
        m_ref[...] = m0_ref[...]

    _mlstm_chunk(0, qkv_ref, og_ref, gates_ref, h_ref, c_ref, n_ref, m_ref, pads, _mlstm_masks(gsz, rows, valid),
                 nseq=nseq, rows=rows, valid=valid, gsz=gsz)


def _mlstm_masks(gsz, rows, valid):
    r = gsz * rows
    eye, incl, _ = _group_masks(gsz, rows)
    rvalid = (lax.broadcasted_iota(jnp.int32, (r, 1), 0) & (rows - 1)) < valid
    return eye, incl, rvalid, _row_block(r, rows, gsz)


def _mlstm_chunk(base, qkv_ref, og_ref, gates_ref, h_ref, c_ref, n_ref, m_ref, pads, masks,
                 *, nseq, rows, valid, gsz):
    eye, incl, rvalid, rblk = masks
    r = gsz * rows
    xpad, opad, gpad = pads if pads else (None, None, None)
    load_x = _seq_loader(qkv_ref, xpad, nseq, valid, rows, base)
    load_o = _seq_loader(og_ref, opad, nseq, valid, rows, base)
    load_g = _seq_loader(gates_ref, gpad, nseq, valid, rows, base)
    m_all = m_ref[...]

    for group in _problems(nseq, gsz):
        def stack(fn):
            return jnp.concatenate([fn(s, h) for s, h in group], axis=0)

        q = stack(lambda s, h: load_x(s, slice(h * DH, (h + 1) * DH)))
        k = stack(lambda s, h: load_x(s, slice(QK + h * DH, QK + (h + 1) * DH))) * (DH ** -0.5)
        v = stack(lambda s, h: load_x(s, slice(2 * QK + h * DH, 2 * QK + (h + 1) * DH)))
        ig = jnp.where(rvalid, stack(lambda s, h: load_g(s, slice(2 * HEADS + h, 2 * HEADS + h + 1))), NEG)
        lf = jnp.where(rvalid, stack(lambda s, h: load_g(s, slice(3 * HEADS + h, 3 * HEADS + h + 1))), 0.0)

        bc = _cumsum_col(lf, eye, incl)
        br = _row_of(bc, eye)
        igr = _row_of(ig, eye)
        d_log = jnp.where(incl, bc - br + igr, NEG)
        d_max = jnp.max(d_log, axis=1, keepdims=True)
        b_last = [bc[(i + 1) * rows - 1:(i + 1) * rows, :] for i in range(gsz)]
        e_log = _per_row(b_last, rows) - bc + ig
        e_max = [jnp.max(e_log[i * rows:(i + 1) * rows], axis=0, keepdims=True) for i in range(gsz)]
        qk = _dot_nt(q, k)

        c_old = [c_ref[s, h] for s, h in group]
        n_old = [n_ref[s, h:h + 1, :] for s, h in group]
        m_old = [m_all[s, :, h:h + 1] for s, h in group]
        inter = bc + _per_row(m_old, rows)
        mt = jnp.maximum(inter, d_max)
        wi = jnp.exp(inter - mt)
        p = jnp.where(incl, jnp.exp(d_log - mt), 0.0) * qk
        qc = _dot_nt(_block_diag(q, rblk, gsz), jnp.concatenate(c_old, axis=1))
        num = wi * qc + _dot(p, v)
        n_rows = jnp.concatenate([jnp.broadcast_to(nv, (rows, DH)) for nv in n_old], axis=0)
        den = wi * jnp.sum(q * n_rows, axis=-1, keepdims=True) + jnp.sum(p, axis=-1, keepdims=True)
        hv = num / jnp.maximum(jnp.abs(den), jnp.exp(-mt))

        m_new = [jnp.maximum(b_last[i] + m_old[i], e_max[i]) for i in range(gsz)]
        fw = [jnp.exp(b_last[i] + m_old[i] - m_new[i]) for i in range(gsz)]
        sw = jnp.exp(e_log - _per_row(m_new, rows))
        c_upd = _dot_tn(sw * v, _block_diag(k, rblk, gsz))
        swk = sw * k
        for i, (s, h) in enumerate(group):
            c_ref[s, h] = fw[i] * c_old[i] + c_upd[:, i * DH:(i + 1) * DH]
            n_ref[s, h:h + 1, :] = fw[i] * n_old[i] + jnp.sum(swk[i * rows:(i + 1) * rows], axis=0, keepdims=True)
            m_ref[s, :, h:h + 1] = m_new[i]

        og = stack(lambda s, h: load_o(s, slice(h * DH, (h + 1) * DH)))
        hg = hv * jax.nn.sigmoid(og)
        for i, (s, h) in enumerate(group):
            h_ref[pl.ds(base + s * valid, valid), h * DH:(h + 1) * DH] = hg[i * rows:i * rows + valid].astype(h_ref.dtype)


def _mlstm(big, gates, c0, n0, m0, *, nb, nc, nseq, rows, valid, gsz):
    kern = functools.partial(_mlstm_kernel, nseq=nseq, rows=rows, valid=valid, gsz=gsz)
    br = nseq * valid
    scratch = []
    if valid < rows:
        scratch = [pltpu.VMEM((nseq, rows, QKV), F32), pltpu.VMEM((nseq, rows, QK), F32),
                   pltpu.VMEM((nseq, rows, N_GATES), F32)]
    return pl.pallas_call(
        kern,
        grid=(nb // nseq, nc),
        in_specs=[
            pl.BlockSpec((br, QKV), lambda b, c: (b * nc + c, 1)),
            pl.BlockSpec((br, QK), lambda b, c: (b * nc + c, 2 * QKV // QK + 1)),
            pl.BlockSpec((br, N_GATES), lambda b, c: (b * nc + c, 0)),
            pl.BlockSpec((nseq, HEADS, DH, DH), lambda b, c: (b, 0, 0, 0)),
            pl.BlockSpec((nseq, HEADS, DH), lambda b, c: (b, 0, 0)),
            pl.BlockSpec((nseq, 1, HEADS), lambda b, c: (b, 0, 0)),
        ],
        out_specs=[
            pl.BlockSpec((br, QK), lambda b, c: (b * nc + c, 0)),
            pl.BlockSpec((nseq, HEADS, DH, DH), lambda b, c: (b, 0, 0, 0)),
            pl.BlockSpec((nseq, HEADS, DH), lambda b, c: (b, 0, 0)),
            pl.BlockSpec((nseq, 1, HEADS), lambda b, c: (b, 0, 0)),
        ],
        out_shape=[jax.ShapeDtypeStruct((nb * nc * valid, QK), _mix_dtype(valid)),
                   jax.ShapeDtypeStruct((nb, HEADS, DH, DH), F32),
                   jax.ShapeDtypeStruct((nb, HEADS, DH), F32),
                   jax.ShapeDtypeStruct((nb, 1, HEADS), F32)],
        scratch_shapes=scratch,
        compiler_params=_params("parallel", "arbitrary"),
        name="mlstm",
    )(big, big, gates, c0, n0, m0)


def _mixers_kernel(*refs, rows, levels, gdn_gsz, ml_gsz, cps, ncast):
    qkv_ref, z_ref, gates_ref, cst_ref, s0_ref, cw_ref, nw_ref = refs[:7]
    cast_in = refs[7:7 + ncast]
    mqkv_ref, og_ref, c0_ref, n0_ref, m0_ref = refs[7 + ncast:12 + ncast]
    outs = refs[12 + ncast:]
    o_ref, s_ref, cst_out_ref = outs[:3]
    cast_out = outs[3:3 + ncast]
    h_ref, c_ref, n_ref, m_ref, hist = outs[3 + ncast:8 + ncast]
    keep = GDN_CONV - 1

    for src, dst in zip(cast_in, cast_out):
        dst[...] = src[...].astype(BF16)

    @pl.when(pl.program_id(1) == 0)
    def _():
        hist[:, pl.ds(SUBLANES - keep, keep), :] = cst_ref[...]
        s_ref[...] = s0_ref[...]
        c_ref[...] = c0_ref[...]
        n_ref[...] = n0_ref[...]
        m_ref[...] = m0_ref[...]

    gmasks = _gdn_masks(gdn_gsz, rows, rows)
    mmasks = _mlstm_masks(ml_gsz, rows, rows)
    for ck in range(cps):
        _gdn_chunk(ck * rows, qkv_ref, z_ref, gates_ref, cw_ref, nw_ref, o_ref, s_ref, hist, (), gmasks,
                   nseq=1, rows=rows, valid=rows, levels=levels, gsz=gdn_gsz)
        _mlstm_chunk(ck * rows, mqkv_ref, og_ref, gates_ref, h_ref, c_ref, n_ref, m_ref, (), mmasks,
                     nseq=1, rows=rows, valid=rows, gsz=ml_gsz)
    cst_out_ref[...] = hist[:, pl.ds(SUBLANES - keep, keep), :]


def _mixers(big, gates, conv_state, s0, c0, n0, m0, conv_w, norm_w, casts, *, nb, nc, rows, gdn_gsz, ml_gsz, cps):
    assert nc % cps == 0
    nc = nc // cps
    levels = max(1, math.ceil(math.log2(rows)))
    kern = functools.partial(_mixers_kernel, rows=rows, levels=levels, gdn_gsz=gdn_gsz, ml_gsz=ml_gsz, cps=cps,
                             ncast=len(casts))
    br = rows * cps
    steps = nb * nc
    assert all(a.shape[0] % (steps * 2 * SUBLANES) == 0 for a in casts)
    cast_specs = [pl.BlockSpec((a.shape[0] // steps, a.shape[1]), lambda b, c: (b * nc + c, 0)) for a in casts]
    row_blk = lambda width, col: pl.BlockSpec((br, width), lambda b, c: (b * nc + c, col))
    per_seq = lambda *shape: pl.BlockSpec((1, *shape), lambda b, c: (b,) + (0,) * len(shape))
    z_col = 2 * QKV // QK
    return pl.pallas_call(
        kern,
        grid=(nb, nc),
        in_specs=[row_blk(QKV, 0), row_blk(QK, z_col), row_blk(N_GATES, 0),
                  per_seq(GDN_CONV - 1, QKV), per_seq(HEADS, DH, DH),
                  pl.BlockSpec((GDN_CONV, QKV), lambda b, c: (0, 0)), pl.BlockSpec((1, DH), lambda b, c: (0, 0)),
                  *cast_specs,
                  row_blk(QKV, 1), row_blk(QK, z_col + 1),
                  per_seq(HEADS, DH, DH), per_seq(HEADS, DH), per_seq(1, HEADS)],
        out_specs=[row_blk(QK, 0), per_seq(HEADS, DH, DH), per_seq(GDN_CONV - 1, QKV), *cast_specs,
                   row_blk(QK, 0), per_seq(HEADS, DH, DH), per_seq(HEADS, DH), per_seq(1, HEADS)],
        out_shape=[jax.ShapeDtypeStruct((nb * nc * br, QK), _mix_dtype(rows)),
                   jax.ShapeDtypeStruct((nb, HEADS, DH, DH), F32),
                   jax.ShapeDtypeStruct((nb, GDN_CONV - 1, QKV), F32),
                   *[jax.ShapeDtypeStruct(a.shape, BF16) for a in casts],
                   jax.ShapeDtypeStruct((nb * nc * br, QK), _mix_dtype(rows)),
                   jax.ShapeDtypeStruct((nb, HEADS, DH, DH), F32),
                   jax.ShapeDtypeStruct((nb, HEADS, DH), F32),
                   jax.ShapeDtypeStruct((nb, 1, HEADS), F32)],
        scratch_shapes=[pltpu.VMEM((1, SUBLANES + rows, QKV), F32)],
        compiler_params=_params("parallel", "arbitrary"),
        name="mixers",
    )(big, big, gates, conv_state, s0, conv_w, norm_w, *casts, big, big, c0, n0, m0)


def _merge_kernel(go_ref, mh_ref, ga_ref, gb_ref, x_ref, wa_ref, wb_ref, wo_ref, x1_ref, mix_scr, *, nt, tn):
    j = pl.program_id(1)

    @pl.when(j < nt)
    def _():
        ya = jnp.dot(go_ref[...].astype(BF16), wa_ref[...], preferred_element_type=F32)
        yb = jnp.dot(mh_ref[...].astype(BF16), wb_ref[...], preferred_element_type=F32)
        mixed = jax.nn.sigmoid(ga_ref[...]) * ya + jax.nn.sigmoid(gb_ref[...]) * yb
        mixed = mixed.astype(BF16)
        for t in range(nt):
            @pl.when(j == t)
            def _():
                mix_scr[:, t * tn:(t + 1) * tn] = mixed

    @pl.when(j >= nt)
    def _():
        x1_ref[...] = x_ref[...] + jnp.dot(mix_scr[...], wo_ref[...], preferred_element_type=F32)


def _merge(go, mh, big, x, wa, wb, wo, *, tm, tn):
    m = x.shape[0]
    nt = D_MODEL // tn
    ga_blk = (2 * QKV + 2 * QK) // tn
    mix_j = lambda j: jnp.minimum(j, nt - 1)
    out_j = lambda j: jnp.maximum(j - nt, 0)
    return pl.pallas_call(
        functools.partial(_merge_kernel, nt=nt, tn=tn),
        grid=(m // tm, 2 * nt),
        in_specs=[
            pl.BlockSpec((tm, QK), lambda i, j: (i, 0)),
            pl.BlockSpec((tm, QK), lambda i, j: (i, 0)),
            pl.BlockSpec((tm, tn), lambda i, j: (i, ga_blk + mix_j(j))),
            pl.BlockSpec((tm, tn), lambda i, j: (i, ga_blk + nt + mix_j(j))),
            pl.BlockSpec((tm, tn), lambda i, j: (i, out_j(j))),
            pl.BlockSpec((QK, tn), lambda i, j: (0, mix_j(j))),
            pl.BlockSpec((QK, tn), lambda i, j: (0, mix_j(j))),
            pl.BlockSpec((D_MODEL, tn), lambda i, j: (0, out_j(j))),
        ],
        out_specs=pl.BlockSpec((tm, tn), lambda i, j: (i, out_j(j))),
        out_shape=jax.ShapeDtypeStruct((m, D_MODEL), F32),
        scratch_shapes=[pltpu.VMEM((tm, D_MODEL), BF16)],
        compiler_params=_params("parallel", "arbitrary"),
        name="merge",
    )(go, mh, big, big, x, wa, wb, wo)


def _ffn_kernel(*refs, tm, tf, seq_tiles, with_state, srows):
    keep = FFN_CONV - 1
    if with_state:
        (x1_ref, nw_ref, wg_ref, wv_ref, cwg_ref, cwv_ref, wd_ref, fnw_ref) = refs[:8]
        stg_refs, stv_refs = refs[8:8 + keep], refs[8 + keep:8 + 2 * keep]
        y_ref = refs[8 + 2 * keep]
        newg_refs, newv_refs = refs[9 + 2 * keep:9 + 3 * keep], refs[9 + 3 * keep:9 + 4 * keep]
        h2_scr, hist_g, hist_v, acc_scr, st_g, st_v, cp_g, cp_v = refs[9 + 4 * keep:]
        nsq = tm // srows
        nck = tf // LANES
    else:
        (x1_ref, halo_ref, nw_ref, wg_ref, wv_ref, cwg_ref, cwv_ref, wd_ref, fnw_ref,
         y_ref, upg_ref, upv_ref, h2_scr, hist_g, hist_v, acc_scr) = refs
    i = pl.program_id(0)
    f = pl.program_id(1)
    pad = SUBLANES

    @pl.when(f == 0)
    def _():
        if with_state:
            h2_scr[pl.ds(0, pad), :] = jnp.zeros((pad, D_MODEL), BF16)
        else:
            live = (i % seq_tiles != 0).astype(F32)
            h2_scr[pl.ds(0, pad), :] = (_rms(halo_ref[...], nw_ref[...]) * live).astype(BF16)
        h2_scr[pl.ds(pad, tm), :] = _rms(x1_ref[...], nw_ref[...]).astype(BF16)
        acc_scr[...] = jnp.zeros_like(acc_scr)

    h2 = h2_scr[...]
    hist_g[...] = jnp.dot(h2, wg_ref[...], preferred_element_type=F32)
    hist_v[...] = jnp.dot(h2, wv_ref[...], preferred_element_type=F32)

    if with_state:
        for st, cp, st_refs, hist, new_refs in ((st_g, cp_g, stg_refs, hist_g, newg_refs),
                                                (st_v, cp_v, stv_refs, hist_v, newv_refs)):
            st[...] = jnp.zeros_like(st)
            for c in range(nck):
                lanes = slice(c * LANES, (c + 1) * LANES)
                cp[c] = hist[:, lanes]
                for r in range(keep):
                    st[c, pl.ds(r, nsq, stride=srows), :] = st_refs[r][:, lanes]
                    new_refs[r][:, lanes] = cp[c, pl.ds(pad + srows - keep + r, nsq, stride=srows), :]
        rmod = lax.broadcasted_iota(jnp.int32, (tm, 1), 0) % srows
        slab = lambda st, off: jnp.concatenate([st[c, pl.ds(off, tm), :] for c in range(nck)], axis=1)
    else:
        upg_ref[...] = hist_g[pl.ds(tm, pad), :]
        upv_ref[...] = hist_v[pl.ds(tm, pad), :]

    def conv(hist, cw_ref, st):
        prev2 = hist[pl.ds(pad - 2, tm), :]
        prev1 = hist[pl.ds(pad - 1, tm), :]
        if with_state:
            prev2 = jnp.where(rmod < 2, slab(st, 0), prev2)
            prev1 = jnp.where(rmod < 1, slab(st, 1), prev1)
        return (prev2 * cw_ref[0:1, :] + prev1 * cw_ref[1:2, :]) + hist[pl.ds(pad, tm), :] * cw_ref[2:3, :]

    ug = conv(hist_g, cwg_ref, st_g if with_state else None)
    uv = conv(hist_v, cwv_ref, st_v if with_state else None)
    act = (ug * jax.nn.sigmoid(ug) * uv).astype(BF16)
    acc_scr[...] += jnp.dot(act, wd_ref[...], preferred_element_type=F32)

    @pl.when(f == pl.num_programs(1) - 1)
    def _():
        y_ref[...] = _rms(x1_ref[...] + acc_scr[...], fnw_ref[...])


def _ffn(x1, nw, wup, cw, wd, fnw, state=None, *, tm, tf, seq_tiles, srows=None):
    m = x1.shape[0]
    nf = D_FF // tf
    keep = FFN_CONV - 1
    with_state = state is not None
    kern = functools.partial(_ffn_kernel, tm=tm, tf=tf, seq_tiles=seq_tiles, with_state=with_state, srows=srows)
    in_specs = [pl.BlockSpec((tm, D_MODEL), lambda i, f: (i, 0))]
    args = [x1]
    if not with_state:
        in_specs.append(pl.BlockSpec((SUBLANES, D_MODEL),
                                     lambda i, f: (jnp.maximum(i * (tm // SUBLANES) - 1, 0), 0)))
        args.append(x1)
    in_specs += [
        pl.BlockSpec((1, D_MODEL), lambda i, f: (0, 0)),
        pl.BlockSpec((D_MODEL, tf), lambda i, f: (0, f)),
        pl.BlockSpec((D_MODEL, tf), lambda i, f: (0, f + nf)),
        pl.BlockSpec((FFN_CONV, tf), lambda i, f: (0, f)),
        pl.BlockSpec((FFN_CONV, tf), lambda i, f: (0, f + nf)),
        pl.BlockSpec((tf, D_MODEL), lambda i, f: (f, 0)),
        pl.BlockSpec((1, D_MODEL), lambda i, f: (0, 0)),
    ]
    args += [nw, wup, wup, cw, cw, wd, fnw]
    scratch = [pltpu.VMEM((SUBLANES + tm, D_MODEL), BF16),
               pltpu.VMEM((SUBLANES + tm, tf), F32),
               pltpu.VMEM((SUBLANES + tm, tf), F32),
               pltpu.VMEM((tm, D_MODEL), F32)]
    n_tiles = m // tm
    out_specs = [pl.BlockSpec((tm, D_MODEL), lambda i, f: (i, 0))]
    out_shape = [jax.ShapeDtypeStruct((m, D_MODEL), F32)]
    if with_state:
        nsq = tm // srows
        in_specs += [pl.BlockSpec((nsq, tf), lambda i, f: (i, f))] * keep
        in_specs += [pl.BlockSpec((nsq, tf), lambda i, f: (i, f + nf))] * keep
        args += list(state) * 2
        scratch += [pltpu.VMEM((tf // LANES, tm + SUBLANES, LANES), F32)] * 4
        out_specs += [pl.BlockSpec((nsq, tf), lambda i, f: (i, f))] * (2 * keep)
        out_shape += [jax.ShapeDtypeStruct((m // srows, D_FF), F32)] * (2 * keep)
    else:
        out_specs += [pl.BlockSpec((SUBLANES, tf), lambda i, f: (i, f))] * 2
        out_shape += [jax.ShapeDtypeStruct((n_tiles * SUBLANES, D_FF), F32)] * 2
    outs = pl.pallas_call(
        kern,
        grid=(n_tiles, nf),
        in_specs=in_specs,
        out_specs=out_specs,
        out_shape=out_shape,
        scratch_shapes=scratch,
        compiler_params=_params("parallel", "arbitrary"),
        name="ffn",
    )(*args)
    return outs[0], outs[1:]


def _layer(x, nb, nc, nseq, rows, valid, gdn_group, ml_group, states, w, *, tm_in, tm_merge, tm_ffn, seq_tiles,
           ffn_state):
    conv_state, s0, c0, n0, m0 = states
    big, gates = _inproj(x, w["norm_mix"], w["w_big"], w["w_small"], w["gate_bias"], w["a_log"],
                         tm=tm_in, tn=TN_INPROJ)
    names = [n for n in ("w_a", "w_b", "w_out", "w_up", "w_down") if w[n].dtype != BF16]
    casts = [w[n] for n in names]
    if nseq == 1 and valid == rows:
        go, s_new, conv_new, *rest = _mixers(big, gates, conv_state, s0, c0, n0, m0, w["gdn_conv_w"], w["gdn_norm"],
                                             casts, nb=nb, nc=nc, rows=rows, gdn_gsz=gdn_group[0], ml_gsz=ml_group,
                                             cps=gdn_group[1])
        cast, (mh, c_new, n_new, m_new) = rest[:len(casts)], rest[len(casts):]
    else:
        go, s_new, conv_new, *cast = _gdn(big, gates, conv_state, s0, w["gdn_conv_w"], w["gdn_norm"], casts,
                                          nb=nb, nc=nc, nseq=nseq, rows=rows, valid=valid,
                                          gsz=gdn_group[0], cps=gdn_group[1])
        mh, c_new, n_new, m_new = _mlstm(big, gates, c0, n0, m0,
                                         nb=nb, nc=nc, nseq=nseq, rows=rows, valid=valid, gsz=ml_group)
    w = {**w, **dict(zip(names, cast))}
    x1 = _merge(go, mh, big, x, w["w_a"], w["w_b"], w["w_out"], tm=tm_merge, tn=TN_MERGE)
    y, ffn_rows = _ffn(x1, w["norm_ffn"], w["w_up"], w["ffn_conv_w"], w["w_down"], w["norm_final"],
                       ffn_state, tm=tm_ffn, tf=TF_FFN, seq_tiles=seq_tiles, srows=valid)
    return y, (conv_new, s_new, c_new, n_new, m_new.reshape(nb, HEADS)), ffn_rows, w


def kernel(x_prompt, x_sample, state_gdn_conv, state_gdn_S, state_ml_C, state_ml_n, state_ml_m, state_ffn_conv,
           norm_mix_w, w_in, gdn_conv_w, gdn_A_log, gdn_dt_bias, gdn_norm_w, w_branch_a, ml_b_i, ml_b_f,
           w_branch_b, w_out, norm_ffn_w, w_up, ffn_conv_w, w_down, norm_final_w):
    assert w_in.shape[0] == 1, "single-layer step"
    bp, tp, _ = x_prompt.shape
    bs, ts, _ = x_sample.shape
    assert FFN_CONV - 1 <= ts <= SUBLANES and GDN_CONV - 1 <= ts and tp % CHUNK == 0
    assert w_in.shape[1:] == (D_MODEL, IN_COLS)
    keep = FFN_CONV - 1

    w_big, w_small = _repack(w_in[0].T, tn=TN_REPACK)
    zeros8 = jnp.zeros((HEADS,), F32)
    w = {
        "norm_mix": norm_mix_w[0][None, :],
        "w_big": w_big,
        "w_small": w_small,
        "gate_bias": jnp.concatenate([gdn_dt_bias[0], zeros8, ml_b_i[0], ml_b_f[0]])[None, :],
        "a_log": jnp.concatenate([gdn_A_log[0], zeros8, zeros8, zeros8])[None, :],
        "gdn_conv_w": gdn_conv_w[0],
        "gdn_norm": gdn_norm_w[0][None, :],
        "w_a": w_branch_a[0],
        "w_b": w_branch_b[0],
        "w_out": w_out[0],
        "norm_ffn": norm_ffn_w[0][None, :],
        "w_up": w_up[0],
        "ffn_conv_w": ffn_conv_w[0],
        "w_down": w_down[0],
        "norm_final": norm_final_w[None, :],
    }

    xp = x_prompt.reshape(bp * tp, D_MODEL)
    p_states = (jnp.zeros((bp, GDN_CONV - 1, QKV), F32), jnp.zeros((bp, HEADS, DH, DH), F32),
                jnp.zeros((bp, HEADS, DH, DH), F32), jnp.zeros((bp, HEADS, DH), F32),
                jnp.zeros((bp, 1, HEADS), F32))
    assert tp % TM_FFN == 0 and (bp * tp) % TM_ROWS == 0
    yp, p_new, up_p, w = _layer(
        xp, bp, tp // CHUNK, 1, CHUNK, CHUNK, PROMPT_GDN_GROUP, PROMPT_ML_GROUP, p_states, w,
        tm_in=TM_ROWS, tm_merge=TM_ROWS, tm_ffn=TM_FFN, seq_tiles=tp // TM_FFN, ffn_state=None)
    p_ffn_conv = jnp.concatenate(
        [u.reshape(bp, tp // TM_FFN, SUBLANES, D_FF)[:, -1, SUBLANES - keep:, :] for u in up_p], axis=-1)

    xs = x_sample.reshape(bs * ts, D_MODEL)
    s_states = (state_gdn_conv[0], state_gdn_S[0], state_ml_C[0], state_ml_n[0],
                state_ml_m[0].reshape(bs, 1, HEADS))
    sample_rows = min(bs * ts, TM_ROWS)
    assert (bs * ts) % sample_rows == 0 and bs % SAMPLE_SEQS == 0
    ys, s_new, new_rows, _ = _layer(
        xs, bs, 1, SAMPLE_SEQS, SUBLANES, ts, (SAMPLE_GROUP, 1), SAMPLE_GROUP, s_states, w,
        tm_in=sample_rows, tm_merge=sample_rows, tm_ffn=sample_rows, seq_tiles=1,
        ffn_state=[state_ffn_conv[0, :, r, :] for r in range(keep)])
    s_ffn_conv = jnp.stack([jnp.concatenate([new_rows[r], new_rows[keep + r]], axis=-1) for r in range(keep)], axis=1)

    lead = lambda t: tuple(a[None] for a in t)
    return (yp.reshape(bp, tp, D_MODEL), ys.reshape(bs, ts, D_MODEL),
            *lead(p_new), p_ffn_conv[None], *lead(s_new), s_ffn_conv[None])
```

```python
import functools
import math

import jax
import jax.numpy as jnp
from jax import lax
from jax.experimental import pallas as pl
from jax.experimental.pallas import tpu as pltpu

F32 = jnp.float32
BF16 = jnp.bfloat16

D_MODEL = 2048
HEADS = 8
DH = 128
QK = HEADS * DH
QKV = 3 * QK
D_FF = 5632
GDN_CONV = 4
FFN_CONV = 3
EPS = 1e-6
CHUNK = 64
N_GATES = 4 * HEADS
SUBLANES = 8
NEG = -1e30

BIG_COLS = 2 * QKV + 2 * QK + 2 * D_MODEL

VMEM_LIMIT = 56 * 1024 * 1024


TM_ROWS = 1024
TN_INPROJ = 1536
TN_MERGE = 512
TM_FFN = 512
TF_FFN = 512
TN_REPACK = 512
PROMPT_GDN_GROUP = (4, 4)
PROMPT_ML_GROUP = 4
SAMPLE_SEQS = 8
SAMPLE_GROUP = 2 * HEADS


def _params(*sem):
    return pltpu.CompilerParams(dimension_semantics=sem, vmem_limit_bytes=VMEM_LIMIT)


def _dot(a, b):
    return jnp.dot(a.astype(BF16), b.astype(BF16), preferred_element_type=F32)


def _dot_nt(a, b):
    return lax.dot_general(a.astype(BF16), b.astype(BF16), (((1,), (1,)), ((), ())),
                           preferred_element_type=F32)


def _dot_tn(a, b):
    return lax.dot_general(a.astype(BF16), b.astype(BF16), (((0,), (0,)), ((), ())),
                           preferred_element_type=F32)


def _softplus(x):
    return jnp.maximum(x, 0.0) + jnp.log1p(jnp.exp(-jnp.abs(x)))


def _rms(x, w):
    return x * lax.rsqrt(jnp.mean(x * x, axis=-1, keepdims=True) + EPS) * w


IN_SECTIONS = (("gqkv", QKV), ("gz", QK), ("ga", HEADS), ("gb", HEADS), ("mqkv", QKV),
               ("mi", HEADS), ("mf", HEADS), ("mo", QK), ("gA", D_MODEL), ("gB", D_MODEL))
BIG_ORDER = ("gqkv", "mqkv", "gz", "mo", "gA", "gB")
IN_COLS = sum(size for _, size in IN_SECTIONS)


def _in_start(name):
    off = 0
    for n, size in IN_SECTIONS:
        if n == name:
            return off
        off += size
    raise KeyError(name)


def _big_tiles(tn):
    starts = []
    for name in BIG_ORDER:
        size = dict(IN_SECTIONS)[name]
        assert size % tn == 0
        starts += [_in_start(name) + t * tn for t in range(size // tn)]
    return starts


def _tile_lookup(j, values):
    out = jnp.int32(values[-1])
    for t in range(len(values) - 2, -1, -1):
        out = jnp.where(j <= t, jnp.int32(values[t]), out)
    return out


LANES = 128
SMALL_ORDER = ("ga", "gb", "mi", "mf")


def _repack_kernel(*refs):
    wt_ref, g_refs, out_ref, small_ref = refs[0], refs[1:1 + len(SMALL_ORDER)], refs[-2], refs[-1]

    @pl.when(pl.program_id(0) == 0)
    def _():
        cols = [g_ref[...].T[:, _in_start(n) % SUBLANES:_in_start(n) % SUBLANES + HEADS]
                for n, g_ref in zip(SMALL_ORDER, g_refs)]
        small_ref[...] = jnp.concatenate(cols, axis=1).astype(BF16)

    out_ref[...] = wt_ref[...].T.astype(BF16)


def _repack(w_in_t, *, tn):
    starts = _big_tiles(tn)
    assert all(s % SUBLANES == 0 for s in starts)
    rows8 = [s // SUBLANES for s in starts]
    return pl.pallas_call(
        _repack_kernel,
        grid=(BIG_COLS // tn,),
        in_specs=[pl.BlockSpec((pl.Element(tn), pl.Element(D_MODEL)),
                               lambda j: (_tile_lookup(j, rows8) * SUBLANES, 0))]
        + [pl.BlockSpec((SUBLANES, D_MODEL), lambda j, b=_in_start(n) // SUBLANES: (b, 0)) for n in SMALL_ORDER],
        out_specs=[pl.BlockSpec((D_MODEL, tn), lambda j: (0, j)),
                   pl.BlockSpec((D_MODEL, N_GATES), lambda j: (0, 0))],
        out_shape=[jax.ShapeDtypeStruct((D_MODEL, BIG_COLS), BF16),
                   jax.ShapeDtypeStruct((D_MODEL, N_GATES), BF16)],
        compiler_params=_params("arbitrary"),
        name="repack",
    )(*([w_in_t] * (1 + len(SMALL_ORDER))))


def _inproj_kernel(x_ref, nw_ref, wbig_ref, wsm_ref, gbias_ref, alog_ref, big_ref, gates_ref, h_scr):
    @pl.when(pl.program_id(1) == 0)
    def _():
        hb = _rms(x_ref[...], nw_ref[...]).astype(BF16)
        h_scr[...] = hb
        raw = jnp.dot(hb, wsm_ref[...], preferred_element_type=F32)
        lane = lax.broadcasted_iota(jnp.int32, raw.shape, 1)
        z = raw + gbias_ref[...]
        g = -jnp.exp(alog_ref[...]) * _softplus(z)
        beta = jax.nn.sigmoid(raw)
        lf = -_softplus(-z)
        gates_ref[...] = jnp.where(lane < HEADS, g,
                                   jnp.where(lane < 2 * HEADS, beta,
                                             jnp.where(lane < 3 * HEADS, z, lf)))

    big_ref[...] = jnp.dot(h_scr[...], wbig_ref[...], preferred_element_type=F32)


def _inproj(x, nw, wbig, wsm, gbias, alog, *, tm, tn):
    m = x.shape[0]
    return pl.pallas_call(
        _inproj_kernel,
        grid=(m // tm, BIG_COLS // tn),
        in_specs=[
            pl.BlockSpec((tm, D_MODEL), lambda i, j: (i, 0)),
            pl.BlockSpec((1, D_MODEL), lambda i, j: (0, 0)),
            pl.BlockSpec((D_MODEL, tn), lambda i, j: (0, j)),
            pl.BlockSpec((D_MODEL, N_GATES), lambda i, j: (0, 0)),
            pl.BlockSpec((1, N_GATES), lambda i, j: (0, 0)),
            pl.BlockSpec((1, N_GATES), lambda i, j: (0, 0)),
        ],
        out_specs=[
            pl.BlockSpec((tm, tn), lambda i, j: (i, j)),
            pl.BlockSpec((tm, N_GATES), lambda i, j: (i, 0)),
        ],
        out_shape=[jax.ShapeDtypeStruct((m, BIG_COLS), F32),
                   jax.ShapeDtypeStruct((m, N_GATES), F32)],
        scratch_shapes=[pltpu.VMEM((tm, D_MODEL), BF16)],
        compiler_params=_params("parallel", "arbitrary"),
        name="inproj",
    )(x, nw, wbig, wsm, gbias, alog)


def _group_masks(n, rows):
    r = n * rows
    shift = rows.bit_length() - 1
    ri = lax.broadcasted_iota(jnp.int32, (r, r), 0)
    ci = lax.broadcasted_iota(jnp.int32, (r, r), 1)
    same = lax.shift_right_logical(ri, shift) == lax.shift_right_logical(ci, shift)
    return ri == ci, same & (ci <= ri), same & (ci < ri)


def _row_block(nrows, rows, n):
    shift = rows.bit_length() - 1
    ri = lax.broadcasted_iota(jnp.int32, (nrows, 1), 0)
    return lax.shift_right_logical(ri, shift) & (n - 1)


def _row_of(col, eye):
    return jnp.sum(jnp.where(eye, col, 0.0), axis=0, keepdims=True)


def _cumsum_col(col, eye, incl):
    return jnp.sum(jnp.where(incl, _row_of(col, eye), 0.0), axis=1, keepdims=True)


def _per_row(vals, rows):
    return jnp.concatenate([jnp.broadcast_to(v, (rows, 1)) for v in vals], axis=0)


def _block_diag(x, rblk, n):
    return jnp.concatenate([jnp.where(rblk == c, x, 0.0) for c in range(n)], axis=1)


def _conv_silu(hist_ref, s, cw_ref, col0, rows):
    cols = slice(col0, col0 + DH)
    base = SUBLANES - GDN_CONV + 1
    acc = hist_ref[s, pl.ds(base, rows), cols] * cw_ref[0:1, cols]
    for j in range(1, GDN_CONV):
        acc = acc + hist_ref[s, pl.ds(base + j, rows), cols] * cw_ref[j:j + 1, cols]
    return acc * jax.nn.sigmoid(acc)


def _seq_loader(ref, scr, nseq, valid, rows, base=0):
    if valid == rows:
        return lambda s, cols: ref[pl.ds(base + s * rows, rows), cols]
    assert base == 0
    for s in range(nseq):
        scr[s, pl.ds(0, valid), :] = ref[pl.ds(s * valid, valid), :]
        scr[s, pl.ds(valid, rows - valid), :] = jnp.zeros((rows - valid, scr.shape[-1]), F32)
    return lambda s, cols: scr[s, :, cols]


def _mix_dtype(valid):
    return BF16 if valid % (2 * SUBLANES) == 0 else F32


def _problems(nseq, gsz):
    probs = [(s, h) for s in range(nseq) for h in range(HEADS)]
    return [probs[i:i + gsz] for i in range(0, len(probs), gsz)]


def _gdn_kernel(*refs, nseq, rows, valid, levels, gsz, cps, ncast):
    qkv_ref, z_ref, gates_ref, cst_ref, s0_ref, cw_ref, nw_ref = refs[:7]
    cast_in = refs[7:7 + ncast]
    o_ref, s_ref, cst_out_ref = refs[7 + ncast:10 + ncast]
    cast_out = refs[10 + ncast:10 + 2 * ncast]
    hist, pads = refs[10 + 2 * ncast], refs[11 + 2 * ncast:]
    keep = GDN_CONV - 1

    for src, dst in zip(cast_in, cast_out):
        dst[...] = src[...].astype(BF16)

    @pl.when(pl.program_id(1) == 0)
    def _():
        hist[:, pl.ds(SUBLANES - keep, keep), :] = cst_ref[...]
        s_ref[...] = s0_ref[...]

    masks = _gdn_masks(gsz, rows, valid)
    for ck in range(cps):
        _gdn_chunk(ck * nseq * valid, qkv_ref, z_ref, gates_ref, cw_ref, nw_ref, o_ref, s_ref, hist, pads, masks,
                   nseq=nseq, rows=rows, valid=valid, levels=levels, gsz=gsz)
    cst_out_ref[...] = hist[:, pl.ds(SUBLANES - keep, keep), :]


def _gdn_masks(gsz, rows, valid):
    r = gsz * rows
    eye, incl, strict = _group_masks(gsz, rows)
    rvalid = (lax.broadcasted_iota(jnp.int32, (r, 1), 0) & (rows - 1)) < valid
    return eye, incl, strict, eye.astype(F32), rvalid, _row_block(r, rows, gsz)


def _gdn_chunk(base, qkv_ref, z_ref, gates_ref, cw_ref, nw_ref, o_ref, s_ref, hist, pads, masks,
               *, nseq, rows, valid, levels, gsz):
    eye, incl, strict, eye_f, rvalid, rblk = masks
    keep = GDN_CONV - 1
    r = gsz * rows
    for s in range(nseq):
        hist[s, pl.ds(SUBLANES, valid), :] = qkv_ref[pl.ds(base + s * valid, valid), :]
        if valid < rows:
            hist[s, pl.ds(SUBLANES + valid, rows - valid), :] = jnp.zeros((rows - valid, QKV), F32)
    zpad, gpad = pads if pads else (None, None)
    load_z = _seq_loader(z_ref, zpad, nseq, valid, rows, base)
    load_g = _seq_loader(gates_ref, gpad, nseq, valid, rows, base)

    for group in _problems(nseq, gsz):
        def stack(fn):
            return jnp.concatenate([fn(s, h) for s, h in group], axis=0)

        q = stack(lambda s, h: _conv_silu(hist, s, cw_ref, h * DH, rows))
        k = stack(lambda s, h: _conv_silu(hist, s, cw_ref, QK + h * DH, rows))
        v = stack(lambda s, h: _conv_silu(hist, s, cw_ref, 2 * QK + h * DH, rows))
        q = q * lax.rsqrt(jnp.sum(q * q, axis=-1, keepdims=True) + EPS) * (DH ** -0.5)
        k = k * lax.rsqrt(jnp.sum(k * k, axis=-1, keepdims=True) + EPS)
        g = jnp.where(rvalid, stack(lambda s, h: load_g(s, slice(h, h + 1))), 0.0)
        beta = jnp.where(rvalid, stack(lambda s, h: load_g(s, slice(HEADS + h, HEADS + h + 1))), 0.0)

        gc = _cumsum_col(g, eye, incl)
        gr = _row_of(gc, eye)
        decay = jnp.where(incl, jnp.exp(jnp.where(incl, gc - gr, 0.0)), 0.0)
        a = jnp.where(strict, beta * _dot_nt(k, k) * decay, 0.0)
        qk = _dot_nt(q, k) * decay
        bk = -a
        t = eye_f + bk
        if levels >= 2:
            bk = _dot(bk, bk)
            for _ in range(2, levels):
                t, bk = t + _dot(t, bk), _dot(bk, bk)
            t = t + _dot(t, bk)
        eg = jnp.exp(gc)
        uw = _dot(t, jnp.concatenate([v * beta, k * (beta * eg)], axis=1))
        u, w = uw[:, :DH], uw[:, DH:]

        s_old = [s_ref[s, h] for s, h in group]
        s_stack = jnp.concatenate(s_old, axis=0)
        v_new = u - _dot(_block_diag(w, rblk, gsz), s_stack)
        o = _dot(_block_diag(q * eg, rblk, gsz), s_stack) + _dot(qk, v_new)
        g_last = [gc[(i + 1) * rows - 1:(i + 1) * rows, :] for i in range(gsz)]
        kd = k * jnp.exp(_per_row(g_last, rows) - gc)
        if rows >= CHUNK:
            s_upd = [_dot_tn(kd[i * rows:(i + 1) * rows], v_new[i * rows:(i + 1) * rows]) for i in range(gsz)]
        else:
            s_all = _dot_tn(_block_diag(kd, rblk, gsz), v_new)
            s_upd = [s_all[i * DH:(i + 1) * DH] for i in range(gsz)]
        for i, (s, h) in enumerate(group):
            s_ref[s, h] = s_old[i] * jnp.exp(g_last[i]) + s_upd[i]

        zs = stack(lambda s, h: load_z(s, slice(h * DH, (h + 1) * DH)))
        on = _rms(o, nw_ref[...]) * (zs * jax.nn.sigmoid(zs))
        for i, (s, h) in enumerate(group):
            o_ref[pl.ds(base + s * valid, valid), h * DH:(h + 1) * DH] = on[i * rows:i * rows + valid].astype(o_ref.dtype)

    for s in range(nseq):
        hist[s, pl.ds(SUBLANES - keep, keep), :] = hist[s, pl.ds(SUBLANES + valid - keep, keep), :]


def _gdn(big, gates, conv_state, s0, conv_w, norm_w, casts=(), *, nb, nc, nseq, rows, valid, gsz, cps):
    assert nc % cps == 0 and (cps == 1 or nseq == 1)
    nc = nc // cps
    levels = max(1, math.ceil(math.log2(valid)))
    kern = functools.partial(_gdn_kernel, nseq=nseq, rows=rows, valid=valid, levels=levels, gsz=gsz, cps=cps,
                             ncast=len(casts))
    br = nseq * valid * cps
    scratch = [pltpu.VMEM((nseq, SUBLANES + rows, QKV), F32)]
    if valid < rows:
        scratch += [pltpu.VMEM((nseq, rows, QK), F32), pltpu.VMEM((nseq, rows, N_GATES), F32)]
    steps = nb // nseq * nc
    bf16_rows = 2 * SUBLANES
    assert all(a.shape[0] % (steps * bf16_rows) == 0 for a in casts)
    cast_specs = [pl.BlockSpec((a.shape[0] // steps, a.shape[1]), lambda b, c: (b * nc + c, 0)) for a in casts]
    return pl.pallas_call(
        kern,
        grid=(nb // nseq, nc),
        in_specs=[
            pl.BlockSpec((br, QKV), lambda b, c: (b * nc + c, 0)),
            pl.BlockSpec((br, QK), lambda b, c: (b * nc + c, 2 * QKV // QK)),
            pl.BlockSpec((br, N_GATES), lambda b, c: (b * nc + c, 0)),
            pl.BlockSpec((nseq, GDN_CONV - 1, QKV), lambda b, c: (b, 0, 0)),
            pl.BlockSpec((nseq, HEADS, DH, DH), lambda b, c: (b, 0, 0, 0)),
            pl.BlockSpec((GDN_CONV, QKV), lambda b, c: (0, 0)),
            pl.BlockSpec((1, DH), lambda b, c: (0, 0)),
            *cast_specs,
        ],
        out_specs=[
            pl.BlockSpec((br, QK), lambda b, c: (b * nc + c, 0)),
            pl.BlockSpec((nseq, HEADS, DH, DH), lambda b, c: (b, 0, 0, 0)),
            pl.BlockSpec((nseq, GDN_CONV - 1, QKV), lambda b, c: (b, 0, 0)),
            *cast_specs,
        ],
        out_shape=[jax.ShapeDtypeStruct((nb // nseq * nc * br, QK), _mix_dtype(valid)),
                   jax.ShapeDtypeStruct((nb, HEADS, DH, DH), F32),
                   jax.ShapeDtypeStruct((nb, GDN_CONV - 1, QKV), F32),
                   *[jax.ShapeDtypeStruct(a.shape, BF16) for a in casts]],
        scratch_shapes=scratch,
        compiler_params=_params("parallel", "arbitrary"),
        name="gdn",
    )(big, big, gates, conv_state, s0, conv_w, norm_w, *casts)


def _mlstm_kernel(qkv_ref, og_ref, gates_ref, c0_ref, n0_ref, m0_ref, h_ref, c_ref, n_ref, m_ref, *pads,
                  nseq, rows, valid, gsz):
    @pl.when(pl.program_id(1) == 0)
    def _():
        c_ref[...] = c0_ref[...]
        n_ref[...] = n0_ref[...]
        m_ref[...] = m0_ref[...]

    _mlstm_chunk(0, qkv_ref, og_ref, gates_ref, h_ref, c_ref, n_ref, m_ref, pads, _mlstm_masks(gsz, rows, valid),
                 nseq=nseq, rows=rows, valid=valid, gsz=gsz)


def _mlstm_masks(gsz, rows, valid):
    r = gsz * rows
    eye, incl, _ = _group_masks(gsz, rows)
    rvalid = (lax.broadcasted_iota(jnp.int32, (r, 1), 0) & (rows - 1)) < valid
    return eye, incl, rvalid, _row_block(r, rows, gsz)


def _mlstm_chunk(base, qkv_ref, og_ref, gates_ref, h_ref, c_ref, n_ref, m_ref, pads, masks,
                 *, nseq, rows, valid, gsz):
    eye, incl, rvalid, rblk = masks
    r = gsz * rows
    xpad, opad, gpad = pads if pads else (None, None, None)
    load_x = _seq_loader(qkv_ref, xpad, nseq, valid, rows, base)
    load_o = _seq_loader(og_ref, opad, nseq, valid, rows, base)
    load_g = _seq_loader(gates_ref, gpad, nseq, valid, rows, base)
    m_all = m_ref[...]

    for group in _problems(nseq, gsz):
        def stack(fn):
            return jnp.concatenate([fn(s, h) for s, h in group], axis=0)

        q = stack(lambda s, h: load_x(s, slice(h * DH, (h + 1) * DH)))
        k = stack(lambda s, h: load_x(s, slice(QK + h * DH, QK + (h + 1) * DH))) * (DH ** -0.5)
        v = stack(lambda s, h: load_x(s, slice(2 * QK + h * DH, 2 * QK + (h + 1) * DH)))
        ig = jnp.where(rvalid, stack(lambda s, h: load_g(s, slice(2 * HEADS + h, 2 * HEADS + h + 1))), NEG)
        lf = jnp.where(rvalid, stack(lambda s, h: load_g(s, slice(3 * HEADS + h, 3 * HEADS + h + 1))), 0.0)

        bc = _cumsum_col(lf, eye, incl)
        br = _row_of(bc, eye)
        igr = _row_of(ig, eye)
        d_log = jnp.where(incl, bc - br + igr, NEG)
        d_max = jnp.max(d_log, axis=1, keepdims=True)
        b_last = [bc[(i + 1) * rows - 1:(i + 1) * rows, :] for i in range(gsz)]
        e_log = _per_row(b_last, rows) - bc + ig
        e_max = [jnp.max(e_log[i * rows:(i + 1) * rows], axis=0, keepdims=True) for i in range(gsz)]
        qk = _dot_nt(q, k)

        c_old = [c_ref[s, h] for s, h in group]
        n_old = [n_ref[s, h:h + 1, :] for s, h in group]
        m_old = [m_all[s, :, h:h + 1] for s, h in group]
        inter = bc + _per_row(m_old, rows)
        mt = jnp.maximum(inter, d_max)
        wi = jnp.exp(inter - mt)
        p = jnp.where(incl, jnp.exp(d_log - mt), 0.0) * qk
        qc = _dot_nt(_block_diag(q, rblk, gsz), jnp.concatenate(c_old, axis=1))
        num = wi * qc + _dot(p, v)
        n_rows = jnp.concatenate([jnp.broadcast_to(nv, (rows, DH)) for nv in n_old], axis=0)
        den = wi * jnp.sum(q * n_rows, axis=-1, keepdims=True) + jnp.sum(p, axis=-1, keepdims=True)
        hv = num / jnp.maximum(jnp.abs(den), jnp.exp(-mt))

        m_new = [jnp.maximum(b_last[i] + m_old[i], e_max[i]) for i in range(gsz)]
        fw = [jnp.exp(b_last[i] + m_old[i] - m_new[i]) for i in range(gsz)]
        sw = jnp.exp(e_log - _per_row(m_new, rows))
        c_upd = _dot_tn(sw * v, _block_diag(k, rblk, gsz))
        swk = sw * k
        for i, (s, h) in enumerate(group):
            c_ref[s, h] = fw[i] * c_old[i] + c_upd[:, i * DH:(i + 1) * DH]
            n_ref[s, h:h + 1, :] = fw[i] * n_old[i] + jnp.sum(swk[i * rows:(i + 1) * rows], axis=0, keepdims=True)
            m_ref[s, :, h:h + 1] = m_new[i]

        og = stack(lambda s, h: load_o(s, slice(h * DH, (h + 1) * DH)))
        hg = hv * jax.nn.sigmoid(og)
        for i, (s, h) in enumerate(group):
            h_ref[pl.ds(base + s * valid, valid), h * DH:(h + 1) * DH] = hg[i * rows:i * rows + valid].astype(h_ref.dtype)


def _mlstm(big, gates, c0, n0, m0, *, nb, nc, nseq, rows, valid, gsz):
    kern = functools.partial(_mlstm_kernel, nseq=nseq, rows=rows, valid=valid, gsz=gsz)
    br = nseq * valid
    scratch = []
    if valid < rows:
        scratch = [pltpu.VMEM((nseq, rows, QKV), F32), pltpu.VMEM((nseq, rows, QK), F32),
                   pltpu.VMEM((nseq, rows, N_GATES), F32)]
    return pl.pallas_call(
        kern,
        grid=(nb // nseq, nc),
        in_specs=[
            pl.BlockSpec((br, QKV), lambda b, c: (b * nc + c, 1)),
            pl.BlockSpec((br, QK), lambda b, c: (b * nc + c, 2 * QKV // QK + 1)),
            pl.BlockSpec((br, N_GATES), lambda b, c: (b * nc + c, 0)),
            pl.BlockSpec((nseq, HEADS, DH, DH), lambda b, c: (b, 0, 0, 0)),
            pl.BlockSpec((nseq, HEADS, DH), lambda b, c: (b, 0, 0)),
            pl.BlockSpec((nseq, 1, HEADS), lambda b, c: (b, 0, 0)),
        ],
        out_specs=[
            pl.BlockSpec((br, QK), lambda b, c: (b * nc + c, 0)),
            pl.BlockSpec((nseq, HEADS, DH, DH), lambda b, c: (b, 0, 0, 0)),
            pl.BlockSpec((nseq, HEADS, DH), lambda b, c: (b, 0, 0)),
            pl.BlockSpec((nseq, 1, HEADS), lambda b, c: (b, 0, 0)),
        ],
        out_shape=[jax.ShapeDtypeStruct((nb * nc * valid, QK), _mix_dtype(valid)),
                   jax.ShapeDtypeStruct((nb, HEADS, DH, DH), F32),
                   jax.ShapeDtypeStruct((nb, HEADS, DH), F32),
                   jax.ShapeDtypeStruct((nb, 1, HEADS), F32)],
        scratch_shapes=scratch,
        compiler_params=_params("parallel", "arbitrary"),
        name="mlstm",
    )(big, big, gates, c0, n0, m0)


def _mixers_kernel(*refs, rows, levels, gdn_gsz, ml_gsz, cps, ncast):
    qkv_ref, z_ref, gates_ref, cst_ref, s0_ref, cw_ref, nw_ref = refs[:7]
    cast_in = refs[7:7 + ncast]
    mqkv_ref, og_ref, c0_ref, n0_ref, m0_ref = refs[7 + ncast:12 + ncast]
    outs = refs[12 + ncast:]
    o_ref, s_ref, cst_out_ref = outs[:3]
    cast_out = outs[3:3 + ncast]
    h_ref, c_ref, n_ref, m_ref, hist = outs[3 + ncast:8 + ncast]
    keep = GDN_CONV - 1

    for src, dst in zip(cast_in, cast_out):
        dst[...] = src[...].astype(BF16)

    @pl.when(pl.program_id(1) == 0)
    def _():
        hist[:, pl.ds(SUBLANES - keep, keep), :] = cst_ref[...]
        s_ref[...] = s0_ref[...]
        c_ref[...] = c0_ref[...]
        n_ref[...] = n0_ref[...]
        m_ref[...] = m0_ref[...]

    gmasks = _gdn_masks(gdn_gsz, rows, rows)
    mmasks = _mlstm_masks(ml_gsz, rows, rows)
    for ck in range(cps):
        _gdn_chunk(ck * rows, qkv_ref, z_ref, gates_ref, cw_ref, nw_ref, o_ref, s_ref, hist, (), gmasks,
                   nseq=1, rows=rows, valid=rows, levels=levels, gsz=gdn_gsz)
        _mlstm_chunk(ck * rows, mqkv_ref, og_ref, gates_ref, h_ref, c_ref, n_ref, m_ref, (), mmasks,
                     nseq=1, rows=rows, valid=rows, gsz=ml_gsz)
    cst_out_ref[...] = hist[:, pl.ds(SUBLANES - keep, keep), :]


def _mixers(big, gates, conv_state, s0, c0, n0, m0, conv_w, norm_w, casts, *, nb, nc, rows, gdn_gsz, ml_gsz, cps):
    assert nc % cps == 0
    nc = nc // cps
    levels = max(1, math.ceil(math.log2(rows)))
    kern = functools.partial(_mixers_kernel, rows=rows, levels=levels, gdn_gsz=gdn_gsz, ml_gsz=ml_gsz, cps=cps,
                             ncast=len(casts))
    br = rows * cps
    steps = nb * nc
    assert all(a.shape[0] % (steps * 2 * SUBLANES) == 0 for a in casts)
    cast_specs = [pl.BlockSpec((a.shape[0] // steps, a.shape[1]), lambda b, c: (b * nc + c, 0)) for a in casts]
    row_blk = lambda width, col: pl.BlockSpec((br, width), lambda b, c: (b * nc + c, col))
    per_seq = lambda *shape: pl.BlockSpec((1, *shape), lambda b, c: (b,) + (0,) * len(shape))
    z_col = 2 * QKV // QK
    return pl.pallas_call(
        kern,
        grid=(nb, nc),
        in_specs=[row_blk(QKV, 0), row_blk(QK, z_col), row_blk(N_GATES, 0),
                  per_seq(GDN_CONV - 1, QKV), per_seq(HEADS, DH, DH),
                  pl.BlockSpec((GDN_CONV, QKV), lambda b, c: (0, 0)), pl.BlockSpec((1, DH), lambda b, c: (0, 0)),
                  *cast_specs,
                  row_blk(QKV, 1), row_blk(QK, z_col + 1),
                  per_seq(HEADS, DH, DH), per_seq(HEADS, DH), per_seq(1, HEADS)],
        out_specs=[row_blk(QK, 0), per_seq(HEADS, DH, DH), per_seq(GDN_CONV - 1, QKV), *cast_specs,
                   row_blk(QK, 0), per_seq(HEADS, DH, DH), per_seq(HEADS, DH), per_seq(1, HEADS)],
        out_shape=[jax.ShapeDtypeStruct((nb * nc * br, QK), _mix_dtype(rows)),
                   jax.ShapeDtypeStruct((nb, HEADS, DH, DH), F32),
                   jax.ShapeDtypeStruct((nb, GDN_CONV - 1, QKV), F32),
                   *[jax.ShapeDtypeStruct(a.shape, BF16) for a in casts],
                   jax.ShapeDtypeStruct((nb * nc * br, QK), _mix_dtype(rows)),
                   jax.ShapeDtypeStruct((nb, HEADS, DH, DH), F32),
                   jax.ShapeDtypeStruct((nb, HEADS, DH), F32),
                   jax.ShapeDtypeStruct((nb, 1, HEADS), F32)],
        scratch_shapes=[pltpu.VMEM((1, SUBLANES + rows, QKV), F32)],
        compiler_params=_params("parallel", "arbitrary"),
        name="mixers",
    )(big, big, gates, conv_state, s0, conv_w, norm_w, *casts, big, big, c0, n0, m0)


def _merge_kernel(go_ref, mh_ref, ga_ref, gb_ref, x_ref, wa_ref, wb_ref, wo_ref, x1_ref, mix_scr, *, nt, tn):
    j = pl.program_id(1)

    @pl.when(j < nt)
    def _():
        ya = jnp.dot(go_ref[...].astype(BF16), wa_ref[...], preferred_element_type=F32)
        yb = jnp.dot(mh_ref[...].astype(BF16), wb_ref[...], preferred_element_type=F32)
        mixed = jax.nn.sigmoid(ga_ref[...]) * ya + jax.nn.sigmoid(gb_ref[...]) * yb
        mixed = mixed.astype(BF16)
        for t in range(nt):
            @pl.when(j == t)
            def _():
                mix_scr[:, t * tn:(t + 1) * tn] = mixed

    @pl.when(j >= nt)
    def _():
        x1_ref[...] = x_ref[...] + jnp.dot(mix_scr[...], wo_ref[...], preferred_element_type=F32)


def _merge(go, mh, big, x, wa, wb, wo, *, tm, tn):
    m = x.shape[0]
    nt = D_MODEL // tn
    ga_blk = (2 * QKV + 2 * QK) // tn
    mix_j = lambda j: jnp.minimum(j, nt - 1)
    out_j = lambda j: jnp.maximum(j - nt, 0)
    return pl.pallas_call(
        functools.partial(_merge_kernel, nt=nt, tn=tn),
        grid=(m // tm, 2 * nt),
        in_specs=[
            pl.BlockSpec((tm, QK), lambda i, j: (i, 0)),
            pl.BlockSpec((tm, QK), lambda i, j: (i, 0)),
            pl.BlockSpec((tm, tn), lambda i, j: (i, ga_blk + mix_j(j))),
            pl.BlockSpec((tm, tn), lambda i, j: (i, ga_blk + nt + mix_j(j))),
            pl.BlockSpec((tm, tn), lambda i, j: (i, out_j(j))),
            pl.BlockSpec((QK, tn), lambda i, j: (0, mix_j(j))),
            pl.BlockSpec((QK, tn), lambda i, j: (0, mix_j(j))),
            pl.BlockSpec((D_MODEL, tn), lambda i, j: (0, out_j(j))),
        ],
        out_specs=pl.BlockSpec((tm, tn), lambda i, j: (i, out_j(j))),
        out_shape=jax.ShapeDtypeStruct((m, D_MODEL), F32),
        scratch_shapes=[pltpu.VMEM((tm, D_MODEL), BF16)],
        compiler_params=_params("parallel", "arbitrary"),
        name="merge",
    )(go, mh, big, big, x, wa, wb, wo)


def _ffn_kernel(*refs, tm, tf, seq_tiles, with_state, srows):
    keep = FFN_CONV - 1
    if with_state:
        (x1_ref, nw_ref, wg_ref, wv_ref, cwg_ref, cwv_ref, wd_ref, fnw_ref) = refs[:8]
        stg_refs, stv_refs = refs[8:8 + keep], refs[8 + keep:8 + 2 * keep]
        y_ref = refs[8 + 2 * keep]
        newg_refs, newv_refs = refs[9 + 2 * keep:9 + 3 * keep], refs[9 + 3 * keep:9 + 4 * keep]
        h2_scr, hist_g, hist_v, acc_scr, st_g, st_v, cp_g, cp_v = refs[9 + 4 * keep:]
        nsq = tm // srows
        nck = tf // LANES
    else:
        (x1_ref, halo_ref, nw_ref, wg_ref, wv_ref, cwg_ref, cwv_ref, wd_ref, fnw_ref,
         y_ref, upg_ref, upv_ref, h2_scr, hist_g, hist_v, acc_scr) = refs
    i = pl.program_id(0)
    f = pl.program_id(1)
    pad = SUBLANES

    @pl.when(f == 0)
    def _():
        if with_state:
            h2_scr[pl.ds(0, pad), :] = jnp.zeros((pad, D_MODEL), BF16)
        else:
            live = (i % seq_tiles != 0).astype(F32)
            h2_scr[pl.ds(0, pad), :] = (_rms(halo_ref[...], nw_ref[...]) * live).astype(BF16)
        h2_scr[pl.ds(pad, tm), :] = _rms(x1_ref[...], nw_ref[...]).astype(BF16)
        acc_scr[...] = jnp.zeros_like(acc_scr)

    h2 = h2_scr[...]
    hist_g[...] = jnp.dot(h2, wg_ref[...], preferred_element_type=F32)
    hist_v[...] = jnp.dot(h2, wv_ref[...], preferred_element_type=F32)

    if with_state:
        for st, cp, st_refs, hist, new_refs in ((st_g, cp_g, stg_refs, hist_g, newg_refs),
                                                (st_v, cp_v, stv_refs, hist_v, newv_refs)):
            st[...] = jnp.zeros_like(st)
            for c in range(nck):
                lanes = slice(c * LANES, (c + 1) * LANES)
                cp[c] = hist[:, lanes]
                for r in range(keep):
                    st[c, pl.ds(r, nsq, stride=srows), :] = st_refs[r][:, lanes]
                    new_refs[r][:, lanes] = cp[c, pl.ds(pad + srows - keep + r, nsq, stride=srows), :]
        rmod = lax.broadcasted_iota(jnp.int32, (tm, 1), 0) % srows
        slab = lambda st, off: jnp.concatenate([st[c, pl.ds(off, tm), :] for c in range(nck)], axis=1)
    else:
        upg_ref[...] = hist_g[pl.ds(tm, pad), :]
        upv_ref[...] = hist_v[pl.ds(tm, pad), :]

    def conv(hist, cw_ref, st):
        prev2 = hist[pl.ds(pad - 2, tm), :]
        prev1 = hist[pl.ds(pad - 1, tm), :]
        if with_state:
            prev2 = jnp.where(rmod < 2, slab(st, 0), prev2)
            prev1 = jnp.where(rmod < 1, slab(st, 1), prev1)
        return (prev2 * cw_ref[0:1, :] + prev1 * cw_ref[1:2, :]) + hist[pl.ds(pad, tm), :] * cw_ref[2:3, :]

    ug = conv(hist_g, cwg_ref, st_g if with_state else None)
    uv = conv(hist_v, cwv_ref, st_v if with_state else None)
    act = (ug * jax.nn.sigmoid(ug) * uv).astype(BF16)
    acc_scr[...] += jnp.dot(act, wd_ref[...], preferred_element_type=F32)

    @pl.when(f == pl.num_programs(1) - 1)
    def _():
        y_ref[...] = _rms(x1_ref[...] + acc_scr[...], fnw_ref[...])


def _ffn(x1, nw, wup, cw, wd, fnw, state=None, *, tm, tf, seq_tiles, srows=None):
    m = x1.shape[0]
    nf = D_FF // tf
    keep = FFN_CONV - 1
    with_state = state is not None
    kern = functools.partial(_ffn_kernel, tm=tm, tf=tf, seq_tiles=seq_tiles, with_state=with_state, srows=srows)
    in_specs = [pl.BlockSpec((tm, D_MODEL), lambda i, f: (i, 0))]
    args = [x1]
    if not with_state:
        in_specs.append(pl.BlockSpec((SUBLANES, D_MODEL),
                                     lambda i, f: (jnp.maximum(i * (tm // SUBLANES) - 1, 0), 0)))
        args.append(x1)
    in_specs += [
        pl.BlockSpec((1, D_MODEL), lambda i, f: (0, 0)),
        pl.BlockSpec((D_MODEL, tf), lambda i, f: (0, f)),
        pl.BlockSpec((D_MODEL, tf), lambda i, f: (0, f + nf)),
        pl.BlockSpec((FFN_CONV, tf), lambda i, f: (0, f)),
        pl.BlockSpec((FFN_CONV, tf), lambda i, f: (0, f + nf)),
        pl.BlockSpec((tf, D_MODEL), lambda i, f: (f, 0)),
        pl.BlockSpec((1, D_MODEL), lambda i, f: (0, 0)),
    ]
    args += [nw, wup, wup, cw, cw, wd, fnw]
    scratch = [pltpu.VMEM((SUBLANES + tm, D_MODEL), BF16),
               pltpu.VMEM((SUBLANES + tm, tf), F32),
               pltpu.VMEM((SUBLANES + tm, tf), F32),
               pltpu.VMEM((tm, D_MODEL), F32)]
    n_tiles = m // tm
    out_specs = [pl.BlockSpec((tm, D_MODEL), lambda i, f: (i, 0))]
    out_shape = [jax.ShapeDtypeStruct((m, D_MODEL), F32)]
    if with_state:
        nsq = tm // srows
        in_specs += [pl.BlockSpec((nsq, tf), lambda i, f: (i, f))] * keep
        in_specs += [pl.BlockSpec((nsq, tf), lambda i, f: (i, f + nf))] * keep
        args += list(state) * 2
        scratch += [pltpu.VMEM((tf // LANES, tm + SUBLANES, LANES), F32)] * 4
        out_specs += [pl.BlockSpec((nsq, tf), lambda i, f: (i, f))] * (2 * keep)
        out_shape += [jax.ShapeDtypeStruct((m // srows, D_FF), F32)] * (2 * keep)
    else:
        out_specs += [pl.BlockSpec((SUBLANES, tf), lambda i, f: (i, f))] * 2
        out_shape += [jax.ShapeDtypeStruct((n_tiles * SUBLANES, D_FF), F32)] * 2
    outs = pl.pallas_call(
        kern,
        grid=(n_tiles, nf),
        in_specs=in_specs,
        out_specs=out_specs,
        out_shape=out_shape,
        scratch_shapes=scratch,
        compiler_params=_params("parallel", "arbitrary"),
        name="ffn",
    )(*args)
    return outs[0], outs[1:]


def _layer(x, nb, nc, nseq, rows, valid, gdn_group, ml_group, states, w, *, tm_in, tm_merge, tm_ffn, seq_tiles,
           ffn_state):
    conv_state, s0, c0, n0, m0 = states
    big, gates = _inproj(x, w["norm_mix"], w["w_big"], w["w_small"], w["gate_bias"], w["a_log"],
                         tm=tm_in, tn=TN_INPROJ)
    names = [n for n in ("w_a", "w_b", "w_out", "w_up", "w_down") if w[n].dtype != BF16]
    casts = [w[n] for n in names]
    if nseq == 1 and valid == rows:
        go, s_new, conv_new, *rest = _mixers(big, gates, conv_state, s0, c0, n0, m0, w["gdn_conv_w"], w["gdn_norm"],
                                             casts, nb=nb, nc=nc, rows=rows, gdn_gsz=gdn_group[0], ml_gsz=ml_group,
                                             cps=gdn_group[1])
        cast, (mh, c_new, n_new, m_new) = rest[:len(casts)], rest[len(casts):]
    else:
        go, s_new, conv_new, *cast = _gdn(big, gates, conv_state, s0, w["gdn_conv_w"], w["gdn_norm"], casts,
                                          nb=nb, nc=nc, nseq=nseq, rows=rows, valid=valid,
                                          gsz=gdn_group[0], cps=gdn_group[1])
        mh, c_new, n_new, m_new = _mlstm(big, gates, c0, n0, m0,
                                         nb=nb, nc=nc, nseq=nseq, rows=rows, valid=valid, gsz=ml_group)
    w = {**w, **dict(zip(names, cast))}
    x1 = _merge(go, mh, big, x, w["w_a"], w["w_b"], w["w_out"], tm=tm_merge, tn=TN_MERGE)
    y, ffn_rows = _ffn(x1, w["norm_ffn"], w["w_up"], w["ffn_conv_w"], w["w_down"], w["norm_final"],
                       ffn_state, tm=tm_ffn, tf=TF_FFN, seq_tiles=seq_tiles, srows=valid)
    return y, (conv_new, s_new, c_new, n_new, m_new.reshape(nb, HEADS)), ffn_rows, w


def kernel(x_prompt, x_sample, state_gdn_conv, state_gdn_S, state_ml_C, state_ml_n, state_ml_m, state_ffn_conv,
           norm_mix_w, w_in, gdn_conv_w, gdn_A_log, gdn_dt_bias, gdn_norm_w, w_branch_a, ml_b_i, ml_b_f,
           w_branch_b, w_out, norm_ffn_w, w_up, ffn_conv_w, w_down, norm_final_w):
    assert w_in.shape[0] == 1, "single-layer step"
    bp, tp, _ = x_prompt.shape
    bs, ts, _ = x_sample.shape
    assert FFN_CONV - 1 <= ts <= SUBLANES and GDN_CONV - 1 <= ts and tp % CHUNK == 0
    assert w_in.shape[1:] == (D_MODEL, IN_COLS)
    keep = FFN_CONV - 1

    w_big, w_small = _repack(w_in[0].T, tn=TN_REPACK)
    zeros8 = jnp.zeros((HEADS,), F32)
    w = {
        "norm_mix": norm_mix_w[0][None, :],
        "w_big": w_big,
        "w_small": w_small,
        "gate_bias": jnp.concatenate([gdn_dt_bias[0], zeros8, ml_b_i[0], ml_b_f[0]])[None, :],
        "a_log": jnp.concatenate([gdn_A_log[0], zeros8, zeros8, zeros8])[None, :],
        "gdn_conv_w": gdn_conv_w[0],
        "gdn_norm": gdn_norm_w[0][None, :],
        "w_a": w_branch_a[0],
        "w_b": w_branch_b[0],
        "w_out": w_out[0],
        "norm_ffn": norm_ffn_w[0][None, :],
        "w_up": w_up[0],
        "ffn_conv_w": ffn_conv_w[0],
        "w_down": w_down[0],
        "norm_final": norm_final_w[None, :],
    }

    xp = x_prompt.reshape(bp * tp, D_MODEL)
    p_states = (jnp.zeros((bp, GDN_CONV - 1, QKV), F32), jnp.zeros((bp, HEADS, DH, DH), F32),
                jnp.zeros((bp, HEADS, DH, DH), F32), jnp.zeros((bp, HEADS, DH), F32),
                jnp.zeros((bp, 1, HEADS), F32))
    assert tp % TM_FFN == 0 and (bp * tp) % TM_ROWS == 0
    yp, p_new, up_p, w = _layer(
        xp, bp, tp // CHUNK, 1, CHUNK, CHUNK, PROMPT_GDN_GROUP, PROMPT_ML_GROUP, p_states, w,
        tm_in=TM_ROWS, tm_merge=TM_ROWS, tm_ffn=TM_FFN, seq_tiles=tp // TM_FFN, ffn_state=None)
    p_ffn_conv = jnp.concatenate(
        [u.reshape(bp, tp // TM_FFN, SUBLANES, D_FF)[:, -1, SUBLANES - keep:, :] for u in up_p], axis=-1)

    xs = x_sample.reshape(bs * ts, D_MODEL)
    s_states = (state_gdn_conv[0], state_gdn_S[0], state_ml_C[0], state_ml_n[0],
                state_ml_m[0].reshape(bs, 1, HEADS))
    sample_rows = min(bs * ts, TM_ROWS)
    assert (bs * ts) % sample_rows == 0 and bs % SAMPLE_SEQS == 0
    ys, s_new, new_rows, _ = _layer(
        xs, bs, 1, SAMPLE_SEQS, SUBLANES, ts, (SAMPLE_GROUP, 1), SAMPLE_GROUP, s_states, w,
        tm_in=sample_rows, tm_merge=sample_rows, tm_ffn=sample_rows, seq_tiles=1,
        ffn_state=[state_ffn_conv[0, :, r, :] for r in range(keep)])
    s_ffn_conv = jnp.stack([jnp.concatenate([new_rows[r], new_rows[keep + r]], axis=-1) for r in range(keep)], axis=1)

    lead = lambda t: tuple(a[None] for a in t)
    return (yp.reshape(bp, tp, D_MODEL), ys.reshape(bs, ts, D_MODEL),
            *lead(p_new), p_ffn_conv[None], *lead(s_new), s_ffn_conv[None])
```

```python
import functools
import math

import jax
import jax.numpy as jnp
from jax import lax
from jax.experimental import pallas as pl
from jax.experimental.pallas import tpu as pltpu

F32 = jnp.float32
BF16 = jnp.bfloat16

D_MODEL = 2048
HEADS = 8
DH = 128
QK = HEADS * DH
QKV = 3 * QK
D_FF = 5632
GDN_CONV = 4
FFN_CONV = 3
EPS = 1e-6
CHUNK = 64
N_GATES = 4 * HEADS
SUBLANES = 8
NEG = -1e30

BIG_COLS = 2 * QKV + 2 * QK + 2 * D_MODEL

VMEM_LIMIT = 56 * 1024 * 1024


TM_ROWS = 1024
TN_INPROJ = 1536
TN_MERGE = 512
TM_FFN = 512
TF_FFN = 512
TN_REPACK = 512
PROMPT_GDN_GROUP = (4, 4)
PROMPT_ML_GROUP = 4
SAMPLE_SEQS = 16
SAMPLE_GROUP = 2 * HEADS


def _params(*sem):
    return pltpu.CompilerParams(dimension_semantics=sem, vmem_limit_bytes=VMEM_LIMIT)


def _dot(a, b):
    return jnp.dot(a.astype(BF16), b.astype(BF16), preferred_element_type=F32)


def _dot_nt(a, b):
    return lax.dot_general(a.astype(BF16), b.astype(BF16), (((1,), (1,)), ((), ())),
                           preferred_element_type=F32)


def _dot_tn(a, b):
    return lax.dot_general(a.astype(BF16), b.astype(BF16), (((0,), (0,)), ((), ())),
                           preferred_element_type=F32)


def _softplus(x):
    return jnp.maximum(x, 0.0) + jnp.log1p(jnp.exp(-jnp.abs(x)))


def _rms(x, w):
    return x * lax.rsqrt(jnp.mean(x * x, axis=-1, keepdims=True) + EPS) * w


IN_SECTIONS = (("gqkv", QKV), ("gz", QK), ("ga", HEADS), ("gb", HEADS), ("mqkv", QKV),
               ("mi", HEADS), ("mf", HEADS), ("mo", QK), ("gA", D_MODEL), ("gB", D_MODEL))
BIG_ORDER = ("gqkv", "mqkv", "gz", "mo", "gA", "gB")
IN_COLS = sum(size for _, size in IN_SECTIONS)


def _in_start(name):
    off = 0
    for n, size in IN_SECTIONS:
        if n == name:
            return off
        off += size
    raise KeyError(name)


def _big_tiles(tn):
    starts = []
    for name in BIG_ORDER:
        size = dict(IN_SECTIONS)[name]
        assert size % tn == 0
        starts += [_in_start(name) + t * tn for t in range(size // tn)]
    return starts


def _tile_lookup(j, values):
    out = jnp.int32(values[-1])
    for t in range(len(values) - 2, -1, -1):
        out = jnp.where(j <= t, jnp.int32(values[t]), out)
    return out


LANES = 128
SMALL_ORDER = ("ga", "gb", "mi", "mf")


def _repack_kernel(*refs):
    wt_ref, g_refs, out_ref, small_ref = refs[0], refs[1:1 + len(SMALL_ORDER)], refs[-2], refs[-1]

    @pl.when(pl.program_id(0) == 0)
    def _():
        cols = [g_ref[...].T[:, _in_start(n) % SUBLANES:_in_start(n) % SUBLANES + HEADS]
                for n, g_ref in zip(SMALL_ORDER, g_refs)]
        small_ref[...] = jnp.concatenate(cols, axis=1).astype(BF16)

    out_ref[...] = wt_ref[...].T.astype(BF16)


def _repack(w_in_t, *, tn):
    starts = _big_tiles(tn)
    assert all(s % SUBLANES == 0 for s in starts)
    rows8 = [s // SUBLANES for s in starts]
    return pl.pallas_call(
        _repack_kernel,
        grid=(BIG_COLS // tn,),
        in_specs=[pl.BlockSpec((pl.Element(tn), pl.Element(D_MODEL)),
                               lambda j: (_tile_lookup(j, rows8) * SUBLANES, 0))]
        + [pl.BlockSpec((SUBLANES, D_MODEL), lambda j, b=_in_start(n) // SUBLANES: (b, 0)) for n in SMALL_ORDER],
        out_specs=[pl.BlockSpec((D_MODEL, tn), lambda j: (0, j)),
                   pl.BlockSpec((D_MODEL, N_GATES), lambda j: (0, 0))],
        out_shape=[jax.ShapeDtypeStruct((D_MODEL, BIG_COLS), BF16),
                   jax.ShapeDtypeStruct((D_MODEL, N_GATES), BF16)],
        compiler_params=_params("arbitrary"),
        name="repack",
    )(*([w_in_t] * (1 + len(SMALL_ORDER))))


def _inproj_kernel(x_ref, nw_ref, wbig_ref, wsm_ref, gbias_ref, alog_ref, big_ref, gates_ref, h_scr):
    @pl.when(pl.program_id(1) == 0)
    def _():
        hb = _rms(x_ref[...], nw_ref[...]).astype(BF16)
        h_scr[...] = hb
        raw = jnp.dot(hb, wsm_ref[...], preferred_element_type=F32)
        lane = lax.broadcasted_iota(jnp.int32, raw.shape, 1)
        z = raw + gbias_ref[...]
        g = -jnp.exp(alog_ref[...]) * _softplus(z)
        beta = jax.nn.sigmoid(raw)
        lf = -_softplus(-z)
        gates_ref[...] = jnp.where(lane < HEADS, g,
                                   jnp.where(lane < 2 * HEADS, beta,
                                             jnp.where(lane < 3 * HEADS, z, lf)))

    big_ref[...] = jnp.dot(h_scr[...], wbig_ref[...], preferred_element_type=F32)


def _inproj(x, nw, wbig, wsm, gbias, alog, *, tm, tn):
    m = x.shape[0]
    return pl.pallas_call(
        _inproj_kernel,
        grid=(m // tm, BIG_COLS // tn),
        in_specs=[
            pl.BlockSpec((tm, D_MODEL), lambda i, j: (i, 0)),
            pl.BlockSpec((1, D_MODEL), lambda i, j: (0, 0)),
            pl.BlockSpec((D_MODEL, tn), lambda i, j: (0, j)),
            pl.BlockSpec((D_MODEL, N_GATES), lambda i, j: (0, 0)),
            pl.BlockSpec((1, N_GATES), lambda i, j: (0, 0)),
            pl.BlockSpec((1, N_GATES), lambda i, j: (0, 0)),
        ],
        out_specs=[
            pl.BlockSpec((tm, tn), lambda i, j: (i, j)),
            pl.BlockSpec((tm, N_GATES), lambda i, j: (i, 0)),
        ],
        out_shape=[jax.ShapeDtypeStruct((m, BIG_COLS), F32),
                   jax.ShapeDtypeStruct((m, N_GATES), F32)],
        scratch_shapes=[pltpu.VMEM((tm, D_MODEL), BF16)],
        compiler_params=_params("parallel", "arbitrary"),
        name="inproj",
    )(x, nw, wbig, wsm, gbias, alog)


def _group_masks(n, rows):
    r = n * rows
    shift = rows.bit_length() - 1
    ri = lax.broadcasted_iota(jnp.int32, (r, r), 0)
    ci = lax.broadcasted_iota(jnp.int32, (r, r), 1)
    same = lax.shift_right_logical(ri, shift) == lax.shift_right_logical(ci, shift)
    return ri == ci, same & (ci <= ri), same & (ci < ri)


def _row_block(nrows, rows, n):
    shift = rows.bit_length() - 1
    ri = lax.broadcasted_iota(jnp.int32, (nrows, 1), 0)
    return lax.shift_right_logical(ri, shift) & (n - 1)


def _row_of(col, eye):
    return jnp.sum(jnp.where(eye, col, 0.0), axis=0, keepdims=True)


def _cumsum_col(col, eye, incl):
    return jnp.sum(jnp.where(incl, _row_of(col, eye), 0.0), axis=1, keepdims=True)


def _per_row(vals, rows):
    return jnp.concatenate([jnp.broadcast_to(v, (rows, 1)) for v in vals], axis=0)


def _block_diag(x, rblk, n):
    return jnp.concatenate([jnp.where(rblk == c, x, 0.0) for c in range(n)], axis=1)


def _conv_silu(hist_ref, s, cw_ref, col0, rows):
    cols = slice(col0, col0 + DH)
    base = SUBLANES - GDN_CONV + 1
    acc = hist_ref[s, pl.ds(base, rows), cols] * cw_ref[0:1, cols]
    for j in range(1, GDN_CONV):
        acc = acc + hist_ref[s, pl.ds(base + j, rows), cols] * cw_ref[j:j + 1, cols]
    return acc * jax.nn.sigmoid(acc)


def _seq_loader(ref, scr, nseq, valid, rows, base=0):
    if valid == rows:
        return lambda s, cols: ref[pl.ds(base + s * rows, rows), cols]
    assert base == 0
    for s in range(nseq):
        scr[s, pl.ds(0, valid), :] = ref[pl.ds(s * valid, valid), :]
        scr[s, pl.ds(valid, rows - valid), :] = jnp.zeros((rows - valid, scr.shape[-1]), F32)
    return lambda s, cols: scr[s, :, cols]


def _mix_dtype(valid):
    return BF16 if valid % (2 * SUBLANES) == 0 else F32


def _problems(nseq, gsz):
    probs = [(s, h) for s in range(nseq) for h in range(HEADS)]
    return [probs[i:i + gsz] for i in range(0, len(probs), gsz)]


def _gdn_kernel(*refs, nseq, rows, valid, levels, gsz, cps, ncast):
    qkv_ref, z_ref, gates_ref, cst_ref, s0_ref, cw_ref, nw_ref = refs[:7]
    cast_in = refs[7:7 + ncast]
    o_ref, s_ref, cst_out_ref = refs[7 + ncast:10 + ncast]
    cast_out = refs[10 + ncast:10 + 2 * ncast]
    hist, pads = refs[10 + 2 * ncast], refs[11 + 2 * ncast:]
    keep = GDN_CONV - 1

    for src, dst in zip(cast_in, cast_out):
        dst[...] = src[...].astype(BF16)

    @pl.when(pl.program_id(1) == 0)
    def _():
        hist[:, pl.ds(SUBLANES - keep, keep), :] = cst_ref[...]
        s_ref[...] = s0_ref[...]

    masks = _gdn_masks(gsz, rows, valid)
    for ck in range(cps):
        _gdn_chunk(ck * nseq * valid, qkv_ref, z_ref, gates_ref, cw_ref, nw_ref, o_ref, s_ref, hist, pads, masks,
                   nseq=nseq, rows=rows, valid=valid, levels=levels, gsz=gsz)
    cst_out_ref[...] = hist[:, pl.ds(SUBLANES - keep, keep), :]


def _gdn_masks(gsz, rows, valid):
    r = gsz * rows
    eye, incl, strict = _group_masks(gsz, rows)
    rvalid = (lax.broadcasted_iota(jnp.int32, (r, 1), 0) & (rows - 1)) < valid
    return eye, incl, strict, eye.astype(F32), rvalid, _row_block(r, rows, gsz)


def _gdn_chunk(base, qkv_ref, z_ref, gates_ref, cw_ref, nw_ref, o_ref, s_ref, hist, pads, masks,
               *, nseq, rows, valid, levels, gsz):
    eye, incl, strict, eye_f, rvalid, rblk = masks
    keep = GDN_CONV - 1
    r = gsz * rows
    for s in range(nseq):
        hist[s, pl.ds(SUBLANES, valid), :] = qkv_ref[pl.ds(base + s * valid, valid), :]
        if valid < rows:
            hist[s, pl.ds(SUBLANES + valid, rows - valid), :] = jnp.zeros((rows - valid, QKV), F32)
    zpad, gpad = pads if pads else (None, None)
    load_z = _seq_loader(z_ref, zpad, nseq, valid, rows, base)
    load_g = _seq_loader(gates_ref, gpad, nseq, valid, rows, base)

    for group in _problems(nseq, gsz):
        def stack(fn):
            return jnp.concatenate([fn(s, h) for s, h in group], axis=0)

        q = stack(lambda s, h: _conv_silu(hist, s, cw_ref, h * DH, rows))
        k = stack(lambda s, h: _conv_silu(hist, s, cw_ref, QK + h * DH, rows))
        v = stack(lambda s, h: _conv_silu(hist, s, cw_ref, 2 * QK + h * DH, rows))
        q = q * lax.rsqrt(jnp.sum(q * q, axis=-1, keepdims=True) + EPS) * (DH ** -0.5)
        k = k * lax.rsqrt(jnp.sum(k * k, axis=-1, keepdims=True) + EPS)
        g = jnp.where(rvalid, stack(lambda s, h: load_g(s, slice(h, h + 1))), 0.0)
        beta = jnp.where(rvalid, stack(lambda s, h: load_g(s, slice(HEADS + h, HEADS + h + 1))), 0.0)

        gc = _cumsum_col(g, eye, incl)
        gr = _row_of(gc, eye)
        decay = jnp.where(incl, jnp.exp(jnp.where(incl, gc - gr, 0.0)), 0.0)
        a = jnp.where(strict, beta * _dot_nt(k, k) * decay, 0.0)
        qk = _dot_nt(q, k) * decay
        bk = -a
        t = eye_f + bk
        if levels >= 2:
            bk = _dot(bk, bk)
            for _ in range(2, levels):
                t, bk = t + _dot(t, bk), _dot(bk, bk)
            t = t + _dot(t, bk)
        eg = jnp.exp(gc)
        uw = _dot(t, jnp.concatenate([v * beta, k * (beta * eg)], axis=1))
        u, w = uw[:, :DH], uw[:, DH:]

        s_old = [s_ref[s, h] for s, h in group]
        s_stack = jnp.concatenate(s_old, axis=0)
        v_new = u - _dot(_block_diag(w, rblk, gsz), s_stack)
        o = _dot(_block_diag(q * eg, rblk, gsz), s_stack) + _dot(qk, v_new)
        g_last = [gc[(i + 1) * rows - 1:(i + 1) * rows, :] for i in range(gsz)]
        kd = k * jnp.exp(_per_row(g_last, rows) - gc)
        if rows >= CHUNK:
            s_upd = [_dot_tn(kd[i * rows:(i + 1) * rows], v_new[i * rows:(i + 1) * rows]) for i in range(gsz)]
        else:
            s_all = _dot_tn(_block_diag(kd, rblk, gsz), v_new)
            s_upd = [s_all[i * DH:(i + 1) * DH] for i in range(gsz)]
        for i, (s, h) in enumerate(group):
            s_ref[s, h] = s_old[i] * jnp.exp(g_last[i]) + s_upd[i]

        zs = stack(lambda s, h: load_z(s, slice(h * DH, (h + 1) * DH)))
        on = _rms(o, nw_ref[...]) * (zs * jax.nn.sigmoid(zs))
        for i, (s, h) in enumerate(group):
            o_ref[pl.ds(base + s * valid, valid), h * DH:(h + 1) * DH] = on[i * rows:i * rows + valid].astype(o_ref.dtype)

    for s in range(nseq):
        hist[s, pl.ds(SUBLANES - keep, keep), :] = hist[s, pl.ds(SUBLANES + valid - keep, keep), :]


def _gdn(big, gates, conv_state, s0, conv_w, norm_w, casts=(), *, nb, nc, nseq, rows, valid, gsz, cps):
    assert nc % cps == 0 and (cps == 1 or nseq == 1)
    nc = nc // cps
    levels = max(1, math.ceil(math.log2(valid)))
    kern = functools.partial(_gdn_kernel, nseq=nseq, rows=rows, valid=valid, levels=levels, gsz=gsz, cps=cps,
                             ncast=len(casts))
    br = nseq * valid * cps
    scratch = [pltpu.VMEM((nseq, SUBLANES + rows, QKV), F32)]
    if valid < rows:
        scratch += [pltpu.VMEM((nseq, rows, QK), F32), pltpu.VMEM((nseq, rows, N_GATES), F32)]
    steps = nb // nseq * nc
    bf16_rows = 2 * SUBLANES
    assert all(a.shape[0] % (steps * bf16_rows) == 0 for a in casts)
    cast_specs = [pl.BlockSpec((a.shape[0] // steps, a.shape[1]), lambda b, c: (b * nc + c, 0)) for a in casts]
    return pl.pallas_call(
        kern,
        grid=(nb // nseq, nc),
        in_specs=[
            pl.BlockSpec((br, QKV), lambda b, c: (b * nc + c, 0)),
            pl.BlockSpec((br, QK), lambda b, c: (b * nc + c, 2 * QKV // QK)),
            pl.BlockSpec((br, N_GATES), lambda b, c: (b * nc + c, 0)),
            pl.BlockSpec((nseq, GDN_CONV - 1, QKV), lambda b, c: (b, 0, 0)),
            pl.BlockSpec((nseq, HEADS, DH, DH), lambda b, c: (b, 0, 0, 0)),
            pl.BlockSpec((GDN_CONV, QKV), lambda b, c: (0, 0)),
            pl.BlockSpec((1, DH), lambda b, c: (0, 0)),
            *cast_specs,
        ],
        out_specs=[
            pl.BlockSpec((br, QK), lambda b, c: (b * nc + c, 0)),
            pl.BlockSpec((nseq, HEADS, DH, DH), lambda b, c: (b, 0, 0, 0)),
            pl.BlockSpec((nseq, GDN_CONV - 1, QKV), lambda b, c: (b, 0, 0)),
            *cast_specs,
        ],
        out_shape=[jax.ShapeDtypeStruct((nb // nseq * nc * br, QK), _mix_dtype(valid)),
                   jax.ShapeDtypeStruct((nb, HEADS, DH, DH), F32),
                   jax.ShapeDtypeStruct((nb, GDN_CONV - 1, QKV), F32),
                   *[jax.ShapeDtypeStruct(a.shape, BF16) for a in casts]],
        scratch_shapes=scratch,
        compiler_params=_params("parallel", "arbitrary"),
        name="gdn",
    )(big, big, gates, conv_state, s0, conv_w, norm_w, *casts)


def _mlstm_kernel(qkv_ref, og_ref, gates_ref, c0_ref, n0_ref, m0_ref, h_ref, c_ref, n_ref, m_ref, *pads,
                  nseq, rows, valid, gsz):
    @pl.when(pl.program_id(1) == 0)
    def _():
        c_ref[...] = c0_ref[...]
        n_ref[...] = n0_ref[...]
        m_ref[...] = m0_ref[...]

    _mlstm_chunk(0, qkv_ref, og_ref, gates_ref, h_ref, c_ref, n_ref, m_ref, pads, _mlstm_masks(gsz, rows, valid),
                 nseq=nseq, rows=rows, valid=valid, gsz=gsz)


def _mlstm_masks(gsz, rows, valid):
    r = gsz * rows
    eye, incl, _ = _group_masks(gsz, rows)
    rvalid = (lax.broadcasted_iota(jnp.int32, (r, 1), 0) & (rows - 1)) < valid
    return eye, incl, rvalid, _row_block(r, rows, gsz)


def _mlstm_chunk(base, qkv_ref, og_ref, gates_ref, h_ref, c_ref, n_ref, m_ref, pads, masks,
                 *, nseq, rows, valid, gsz):
    eye, incl, rvalid, rblk = masks
    r = gsz * rows
    xpad, opad, gpad = pads if pads else (None, None, None)
    load_x = _seq_loader(qkv_ref, xpad, nseq, valid, rows, base)
    load_o = _seq_loader(og_ref, opad, nseq, valid, rows, base)
    load_g = _seq_loader(gates_ref, gpad, nseq, valid, rows, base)
    m_all = m_ref[...]

    for group in _problems(nseq, gsz):
        def stack(fn):
            return jnp.concatenate([fn(s, h) for s, h in group], axis=0)

        q = stack(lambda s, h: load_x(s, slice(h * DH, (h + 1) * DH)))
        k = stack(lambda s, h: load_x(s, slice(QK + h * DH, QK + (h + 1) * DH))) * (DH ** -0.5)
        v = stack(lambda s, h: load_x(s, slice(2 * QK + h * DH, 2 * QK + (h + 1) * DH)))
        ig = jnp.where(rvalid, stack(lambda s, h: load_g(s, slice(2 * HEADS + h, 2 * HEADS + h + 1))), NEG)
        lf = jnp.where(rvalid, stack(lambda s, h: load_g(s, slice(3 * HEADS + h, 3 * HEADS + h + 1))), 0.0)

        bc = _cumsum_col(lf, eye, incl)
        br = _row_of(bc, eye)
        igr = _row_of(ig, eye)
        d_log = jnp.where(incl, bc - br + igr, NEG)
        d_max = jnp.max(d_log, axis=1, keepdims=True)
        b_last = [bc[(i + 1) * rows - 1:(i + 1) * rows, :] for i in range(gsz)]
        e_log = _per_row(b_last, rows) - bc + ig
        e_max = [jnp.max(e_log[i * rows:(i + 1) * rows], axis=0, keepdims=True) for i in range(gsz)]
        qk = _dot_nt(q, k)

        c_old = [c_ref[s, h] for s, h in group]
        n_old = [n_ref[s, h:h + 1, :] for s, h in group]
        m_old = [m_all[s, :, h:h + 1] for s, h in group]
        inter = bc + _per_row(m_old, rows)
        mt = jnp.maximum(inter, d_max)
        wi = jnp.exp(inter - mt)
        p = jnp.where(incl, jnp.exp(d_log - mt), 0.0) * qk
        qc = _dot_nt(_block_diag(q, rblk, gsz), jnp.concatenate(c_old, axis=1))
        num = wi * qc + _dot(p, v)
        n_rows = jnp.concatenate([jnp.broadcast_to(nv, (rows, DH)) for nv in n_old], axis=0)
        den = wi * jnp.sum(q * n_rows, axis=-1, keepdims=True) + jnp.sum(p, axis=-1, keepdims=True)
        hv = num / jnp.maximum(jnp.abs(den), jnp.exp(-mt))

        m_new = [jnp.maximum(b_last[i] + m_old[i], e_max[i]) for i in range(gsz)]
        fw = [jnp.exp(b_last[i] + m_old[i] - m_new[i]) for i in range(gsz)]
        sw = jnp.exp(e_log - _per_row(m_new, rows))
        c_upd = _dot_tn(sw * v, _block_diag(k, rblk, gsz))
        swk = sw * k
        for i, (s, h) in enumerate(group):
            c_ref[s, h] = fw[i] * c_old[i] + c_upd[:, i * DH:(i + 1) * DH]
            n_ref[s, h:h + 1, :] = fw[i] * n_old[i] + jnp.sum(swk[i * rows:(i + 1) * rows], axis=0, keepdims=True)
            m_ref[s, :, h:h + 1] = m_new[i]

        og = stack(lambda s, h: load_o(s, slice(h * DH, (h + 1) * DH)))
        hg = hv * jax.nn.sigmoid(og)
        for i, (s, h) in enumerate(group):
            h_ref[pl.ds(base + s * valid, valid), h * DH:(h + 1) * DH] = hg[i * rows:i * rows + valid].astype(h_ref.dtype)


def _mlstm(big, gates, c0, n0, m0, *, nb, nc, nseq, rows, valid, gsz):
    kern = functools.partial(_mlstm_kernel, nseq=nseq, rows=rows, valid=valid, gsz=gsz)
    br = nseq * valid
    scratch = []
    if valid < rows:
        scratch = [pltpu.VMEM((nseq, rows, QKV), F32), pltpu.VMEM((nseq, rows, QK), F32),
                   pltpu.VMEM((nseq, rows, N_GATES), F32)]
    return pl.pallas_call(
        kern,
        grid=(nb // nseq, nc),
        in_specs=[
            pl.BlockSpec((br, QKV), lambda b, c: (b * nc + c, 1)),
            pl.BlockSpec((br, QK), lambda b, c: (b * nc + c, 2 * QKV // QK + 1)),
            pl.BlockSpec((br, N_GATES), lambda b, c: (b * nc + c, 0)),
            pl.BlockSpec((nseq, HEADS, DH, DH), lambda b, c: (b, 0, 0, 0)),
            pl.BlockSpec((nseq, HEADS, DH), lambda b, c: (b, 0, 0)),
            pl.BlockSpec((nseq, 1, HEADS), lambda b, c: (b, 0, 0)),
        ],
        out_specs=[
            pl.BlockSpec((br, QK), lambda b, c: (b * nc + c, 0)),
            pl.BlockSpec((nseq, HEADS, DH, DH), lambda b, c: (b, 0, 0, 0)),
            pl.BlockSpec((nseq, HEADS, DH), lambda b, c: (b, 0, 0)),
            pl.BlockSpec((nseq, 1, HEADS), lambda b, c: (b, 0, 0)),
        ],
        out_shape=[jax.ShapeDtypeStruct((nb * nc * valid, QK), _mix_dtype(valid)),
                   jax.ShapeDtypeStruct((nb, HEADS, DH, DH), F32),
                   jax.ShapeDtypeStruct((nb, HEADS, DH), F32),
                   jax.ShapeDtypeStruct((nb, 1, HEADS), F32)],
        scratch_shapes=scratch,
        compiler_params=_params("parallel", "arbitrary"),
        name="mlstm",
    )(big, big, gates, c0, n0, m0)


def _mixers_kernel(*refs, rows, levels, gdn_gsz, ml_gsz, cps, ncast):
    qkv_ref, z_ref, gates_ref, cst_ref, s0_ref, cw_ref, nw_ref = refs[:7]
    cast_in = refs[7:7 + ncast]
    mqkv_ref, og_ref, c0_ref, n0_ref, m0_ref = refs[7 + ncast:12 + ncast]
    outs = refs[12 + ncast:]
    o_ref, s_ref, cst_out_ref = outs[:3]
    cast_out = outs[3:3 + ncast]
    h_ref, c_ref, n_ref, m_ref, hist = outs[3 + ncast:8 + ncast]
    keep = GDN_CONV - 1

    for src, dst in zip(cast_in, cast_out):
        dst[...] = src[...].astype(BF16)

    @pl.when(pl.program_id(1) == 0)
    def _():
        hist[:, pl.ds(SUBLANES - keep, keep), :] = cst_ref[...]
        s_ref[...] = s0_ref[...]
        c_ref[...] = c0_ref[...]
        n_ref[...] = n0_ref[...]
        m_ref[...] = m0_ref[...]

    gmasks = _gdn_masks(gdn_gsz, rows, rows)
    mmasks = _mlstm_masks(ml_gsz, rows, rows)
    for ck in range(cps):
        _gdn_chunk(ck * rows, qkv_ref, z_ref, gates_ref, cw_ref, nw_ref, o_ref, s_ref, hist, (), gmasks,
                   nseq=1, rows=rows, valid=rows, levels=levels, gsz=gdn_gsz)
        _mlstm_chunk(ck * rows, mqkv_ref, og_ref, gates_ref, h_ref, c_ref, n_ref, m_ref, (), mmasks,
                     nseq=1, rows=rows, valid=rows, gsz=ml_gsz)
    cst_out_ref[...] = hist[:, pl.ds(SUBLANES - keep, keep), :]


def _mixers(big, gates, conv_state, s0, c0, n0, m0, conv_w, norm_w, casts, *, nb, nc, rows, gdn_gsz, ml_gsz, cps):
    assert nc % cps == 0
    nc = nc // cps
    levels = max(1, math.ceil(math.log2(rows)))
    kern = functools.partial(_mixers_kernel, rows=rows, levels=levels, gdn_gsz=gdn_gsz, ml_gsz=ml_gsz, cps=cps,
                             ncast=len(casts))
    br = rows * cps
    steps = nb * nc
    assert all(a.shape[0] % (steps * 2 * SUBLANES) == 0 for a in casts)
    cast_specs = [pl.BlockSpec((a.shape[0] // steps, a.shape[1]), lambda b, c: (b * nc + c, 0)) for a in casts]
    row_blk = lambda width, col: pl.BlockSpec((br, width), lambda b, c: (b * nc + c, col))
    per_seq = lambda *shape: pl.BlockSpec((1, *shape), lambda b, c: (b,) + (0,) * len(shape))
    z_col = 2 * QKV // QK
    return pl.pallas_call(
        kern,
        grid=(nb, nc),
        in_specs=[row_blk(QKV, 0), row_blk(QK, z_col), row_blk(N_GATES, 0),
                  per_seq(GDN_CONV - 1, QKV), per_seq(HEADS, DH, DH),
                  pl.BlockSpec((GDN_CONV, QKV), lambda b, c: (0, 0)), pl.BlockSpec((1, DH), lambda b, c: (0, 0)),
                  *cast_specs,
                  row_blk(QKV, 1), row_blk(QK, z_col + 1),
                  per_seq(HEADS, DH, DH), per_seq(HEADS, DH), per_seq(1, HEADS)],
        out_specs=[row_blk(QK, 0), per_seq(HEADS, DH, DH), per_seq(GDN_CONV - 1, QKV), *cast_specs,
                   row_blk(QK, 0), per_seq(HEADS, DH, DH), per_seq(HEADS, DH), per_seq(1, HEADS)],
        out_shape=[jax.ShapeDtypeStruct((nb * nc * br, QK), _mix_dtype(rows)),
                   jax.ShapeDtypeStruct((nb, HEADS, DH, DH), F32),
                   jax.ShapeDtypeStruct((nb, GDN_CONV - 1, QKV), F32),
                   *[jax.ShapeDtypeStruct(a.shape, BF16) for a in casts],
                   jax.ShapeDtypeStruct((nb * nc * br, QK), _mix_dtype(rows)),
                   jax.ShapeDtypeStruct((nb, HEADS, DH, DH), F32),
                   jax.ShapeDtypeStruct((nb, HEADS, DH), F32),
                   jax.ShapeDtypeStruct((nb, 1, HEADS), F32)],
        scratch_shapes=[pltpu.VMEM((1, SUBLANES + rows, QKV), F32)],
        compiler_params=_params("parallel", "arbitrary"),
        name="mixers",
    )(big, big, gates, conv_state, s0, conv_w, norm_w, *casts, big, big, c0, n0, m0)


def _merge_kernel(go_ref, mh_ref, ga_ref, gb_ref, x_ref, wa_ref, wb_ref, wo_ref, x1_ref, mix_scr, *, nt, tn):
    j = pl.program_id(1)

    @pl.when(j < nt)
    def _():
        ya = jnp.dot(go_ref[...].astype(BF16), wa_ref[...], preferred_element_type=F32)
        yb = jnp.dot(mh_ref[...].astype(BF16), wb_ref[...], preferred_element_type=F32)
        mixed = jax.nn.sigmoid(ga_ref[...]) * ya + jax.nn.sigmoid(gb_ref[...]) * yb
        mixed = mixed.astype(BF16)
        for t in range(nt):
            @pl.when(j == t)
            def _():
                mix_scr[:, t * tn:(t + 1) * tn] = mixed

    @pl.when(j >= nt)
    def _():
        x1_ref[...] = x_ref[...] + jnp.dot(mix_scr[...], wo_ref[...], preferred_element_type=F32)


def _merge(go, mh, big, x, wa, wb, wo, *, tm, tn):
    m = x.shape[0]
    nt = D_MODEL // tn
    ga_blk = (2 * QKV + 2 * QK) // tn
    mix_j = lambda j: jnp.minimum(j, nt - 1)
    out_j = lambda j: jnp.maximum(j - nt, 0)
    return pl.pallas_call(
        functools.partial(_merge_kernel, nt=nt, tn=tn),
        grid=(m // tm, 2 * nt),
        in_specs=[
            pl.BlockSpec((tm, QK), lambda i, j: (i, 0)),
            pl.BlockSpec((tm, QK), lambda i, j: (i, 0)),
            pl.BlockSpec((tm, tn), lambda i, j: (i, ga_blk + mix_j(j))),
            pl.BlockSpec((tm, tn), lambda i, j: (i, ga_blk + nt + mix_j(j))),
            pl.BlockSpec((tm, tn), lambda i, j: (i, out_j(j))),
            pl.BlockSpec((QK, tn), lambda i, j: (0, mix_j(j))),
            pl.BlockSpec((QK, tn), lambda i, j: (0, mix_j(j))),
            pl.BlockSpec((D_MODEL, tn), lambda i, j: (0, out_j(j))),
        ],
        out_specs=pl.BlockSpec((tm, tn), lambda i, j: (i, out_j(j))),
        out_shape=jax.ShapeDtypeStruct((m, D_MODEL), F32),
        scratch_shapes=[pltpu.VMEM((tm, D_MODEL), BF16)],
        compiler_params=_params("parallel", "arbitrary"),
        name="merge",
    )(go, mh, big, big, x, wa, wb, wo)


def _ffn_kernel(*refs, tm, tf, seq_tiles, with_state, srows):
    keep = FFN_CONV - 1
    if with_state:
        (x1_ref, nw_ref, wg_ref, wv_ref, cwg_ref, cwv_ref, wd_ref, fnw_ref) = refs[:8]
        stg_refs, stv_refs = refs[8:8 + keep], refs[8 + keep:8 + 2 * keep]
        y_ref = refs[8 + 2 * keep]
        newg_refs, newv_refs = refs[9 + 2 * keep:9 + 3 * keep], refs[9 + 3 * keep:9 + 4 * keep]
        h2_scr, hist_g, hist_v, acc_scr, st_g, st_v, cp_g, cp_v = refs[9 + 4 * keep:]
        nsq = tm // srows
        nck = tf // LANES
    else:
        (x1_ref, halo_ref, nw_ref, wg_ref, wv_ref, cwg_ref, cwv_ref, wd_ref, fnw_ref,
         y_ref, upg_ref, upv_ref, h2_scr, hist_g, hist_v, acc_scr) = refs
    i = pl.program_id(0)
    f = pl.program_id(1)
    pad = SUBLANES

    @pl.when(f == 0)
    def _():
        if with_state:
            h2_scr[pl.ds(0, pad), :] = jnp.zeros((pad, D_MODEL), BF16)
        else:
            live = (i % seq_tiles != 0).astype(F32)
            h2_scr[pl.ds(0, pad), :] = (_rms(halo_ref[...], nw_ref[...]) * live).astype(BF16)
        h2_scr[pl.ds(pad, tm), :] = _rms(x1_ref[...], nw_ref[...]).astype(BF16)
        acc_scr[...] = jnp.zeros_like(acc_scr)

    h2 = h2_scr[...]
    hist_g[...] = jnp.dot(h2, wg_ref[...], preferred_element_type=F32)
    hist_v[...] = jnp.dot(h2, wv_ref[...], preferred_element_type=F32)

    if with_state:
        for st, cp, st_refs, hist, new_refs in ((st_g, cp_g, stg_refs, hist_g, newg_refs),
                                                (st_v, cp_v, stv_refs, hist_v, newv_refs)):
            st[...] = jnp.zeros_like(st)
            for c in range(nck):
                lanes = slice(c * LANES, (c + 1) * LANES)
                cp[c] = hist[:, lanes]
                for r in range(keep):
                    st[c, pl.ds(r, nsq, stride=srows), :] = st_refs[r][:, lanes]
                    new_refs[r][:, lanes] = cp[c, pl.ds(pad + srows - keep + r, nsq, stride=srows), :]
        rmod = lax.broadcasted_iota(jnp.int32, (tm, 1), 0) % srows
        slab = lambda st, off: jnp.concatenate([st[c, pl.ds(off, tm), :] for c in range(nck)], axis=1)
    else:
        upg_ref[...] = hist_g[pl.ds(tm, pad), :]
        upv_ref[...] = hist_v[pl.ds(tm, pad), :]

    def conv(hist, cw_ref, st):
        prev2 = hist[pl.ds(pad - 2, tm), :]
        prev1 = hist[pl.ds(pad - 1, tm), :]
        if with_state:
            prev2 = jnp.where(rmod < 2, slab(st, 0), prev2)
            prev1 = jnp.where(rmod < 1, slab(st, 1), prev1)
        return (prev2 * cw_ref[0:1, :] + prev1 * cw_ref[1:2, :]) + hist[pl.ds(pad, tm), :] * cw_ref[2:3, :]

    ug = conv(hist_g, cwg_ref, st_g if with_state else None)
    uv = conv(hist_v, cwv_ref, st_v if with_state else None)
    act = (ug * jax.nn.sigmoid(ug) * uv).astype(BF16)
    acc_scr[...] += jnp.dot(act, wd_ref[...], preferred_element_type=F32)

    @pl.when(f == pl.num_programs(1) - 1)
    def _():
        y_ref[...] = _rms(x1_ref[...] + acc_scr[...], fnw_ref[...])


def _ffn(x1, nw, wup, cw, wd, fnw, state=None, *, tm, tf, seq_tiles, srows=None):
    m = x1.shape[0]
    nf = D_FF // tf
    keep = FFN_CONV - 1
    with_state = state is not None
    kern = functools.partial(_ffn_kernel, tm=tm, tf=tf, seq_tiles=seq_tiles, with_state=with_state, srows=srows)
    in_specs = [pl.BlockSpec((tm, D_MODEL), lambda i, f: (i, 0))]
    args = [x1]
    if not with_state:
        in_specs.append(pl.BlockSpec((SUBLANES, D_MODEL),
                                     lambda i, f: (jnp.maximum(i * (tm // SUBLANES) - 1, 0), 0)))
        args.append(x1)
    in_specs += [
        pl.BlockSpec((1, D_MODEL), lambda i, f: (0, 0)),
        pl.BlockSpec((D_MODEL, tf), lambda i, f: (0, f)),
        pl.BlockSpec((D_MODEL, tf), lambda i, f: (0, f + nf)),
        pl.BlockSpec((FFN_CONV, tf), lambda i, f: (0, f)),
        pl.BlockSpec((FFN_CONV, tf), lambda i, f: (0, f + nf)),
        pl.BlockSpec((tf, D_MODEL), lambda i, f: (f, 0)),
        pl.BlockSpec((1, D_MODEL), lambda i, f: (0, 0)),
    ]
    args += [nw, wup, wup, cw, cw, wd, fnw]
    scratch = [pltpu.VMEM((SUBLANES + tm, D_MODEL), BF16),
               pltpu.VMEM((SUBLANES + tm, tf), F32),
               pltpu.VMEM((SUBLANES + tm, tf), F32),
               pltpu.VMEM((tm, D_MODEL), F32)]
    n_tiles = m // tm
    out_specs = [pl.BlockSpec((tm, D_MODEL), lambda i, f: (i, 0))]
    out_shape = [jax.ShapeDtypeStruct((m, D_MODEL), F32)]
    if with_state:
        nsq = tm // srows
        in_specs += [pl.BlockSpec((nsq, tf), lambda i, f: (i, f))] * keep
        in_specs += [pl.BlockSpec((nsq, tf), lambda i, f: (i, f + nf))] * keep
        args += list(state) * 2
        scratch += [pltpu.VMEM((tf // LANES, tm + SUBLANES, LANES), F32)] * 4
        out_specs += [pl.BlockSpec((nsq, tf), lambda i, f: (i, f))] * (2 * keep)
        out_shape += [jax.ShapeDtypeStruct((m // srows, D_FF), F32)] * (2 * keep)
    else:
        out_specs += [pl.BlockSpec((SUBLANES, tf), lambda i, f: (i, f))] * 2
        out_shape += [jax.ShapeDtypeStruct((n_tiles * SUBLANES, D_FF), F32)] * 2
    outs = pl.pallas_call(
        kern,
        grid=(n_tiles, nf),
        in_specs=in_specs,
        out_specs=out_specs,
        out_shape=out_shape,
        scratch_shapes=scratch,
        compiler_params=_params("parallel", "arbitrary"),
        name="ffn",
    )(*args)
    return outs[0], outs[1:]


def _layer(x, nb, nc, nseq, rows, valid, gdn_group, ml_group, states, w, *, tm_in, tm_merge, tm_ffn, seq_tiles,
           ffn_state):
    conv_state, s0, c0, n0, m0 = states
    big, gates = _inproj(x, w["norm_mix"], w["w_big"], w["w_small"], w["gate_bias"], w["a_log"],
                         tm=tm_in, tn=TN_INPROJ)
    names = [n for n in ("w_a", "w_b", "w_out", "w_up", "w_down") if w[n].dtype != BF16]
    casts = [w[n] for n in names]
    if nseq == 1 and valid == rows:
        go, s_new, conv_new, *rest = _mixers(big, gates, conv_state, s0, c0, n0, m0, w["gdn_conv_w"], w["gdn_norm"],
                                             casts, nb=nb, nc=nc, rows=rows, gdn_gsz=gdn_group[0], ml_gsz=ml_group,
                                             cps=gdn_group[1])
        cast, (mh, c_new, n_new, m_new) = rest[:len(casts)], rest[len(casts):]
    else:
        go, s_new, conv_new, *cast = _gdn(big, gates, conv_state, s0, w["gdn_conv_w"], w["gdn_norm"], casts,
                                          nb=nb, nc=nc, nseq=nseq, rows=rows, valid=valid,
                                          gsz=gdn_group[0], cps=gdn_group[1])
        mh, c_new, n_new, m_new = _mlstm(big, gates, c0, n0, m0,
                                         nb=nb, nc=nc, nseq=nseq, rows=rows, valid=valid, gsz=ml_group)
    w = {**w, **dict(zip(names, cast))}
    x1 = _merge(go, mh, big, x, w["w_a"], w["w_b"], w["w_out"], tm=tm_merge, tn=TN_MERGE)
    y, ffn_rows = _ffn(x1, w["norm_ffn"], w["w_up"], w["ffn_conv_w"], w["w_down"], w["norm_final"],
                       ffn_state, tm=tm_ffn, tf=TF_FFN, seq_tiles=seq_tiles, srows=valid)
    return y, (conv_new, s_new, c_new, n_new, m_new.reshape(nb, HEADS)), ffn_rows, w


def kernel(x_prompt, x_sample, state_gdn_conv, state_gdn_S, state_ml_C, state_ml_n, state_ml_m, state_ffn_conv,
           norm_mix_w, w_in, gdn_conv_w, gdn_A_log, gdn_dt_bias, gdn_norm_w, w_branch_a, ml_b_i, ml_b_f,
           w_branch_b, w_out, norm_ffn_w, w_up, ffn_conv_w, w_down, norm_final_w):
    assert w_in.shape[0] == 1, "single-layer step"
    bp, tp, _ = x_prompt.shape
    bs, ts, _ = x_sample.shape
    assert FFN_CONV - 1 <= ts <= SUBLANES and GDN_CONV - 1 <= ts and tp % CHUNK == 0
    assert w_in.shape[1:] == (D_MODEL, IN_COLS)
    keep = FFN_CONV - 1

    w_big, w_small = _repack(w_in[0].T, tn=TN_REPACK)
    zeros8 = jnp.zeros((HEADS,), F32)
    w = {
        "norm_mix": norm_mix_w[0][None, :],
        "w_big": w_big,
        "w_small": w_small,
        "gate_bias": jnp.concatenate([gdn_dt_bias[0], zeros8, ml_b_i[0], ml_b_f[0]])[None, :],
        "a_log": jnp.concatenate([gdn_A_log[0], zeros8, zeros8, zeros8])[None, :],
        "gdn_conv_w": gdn_conv_w[0],
        "gdn_norm": gdn_norm_w[0][None, :],
        "w_a": w_branch_a[0],
        "w_b": w_branch_b[0],
        "w_out": w_out[0],
        "norm_ffn": norm_ffn_w[0][None, :],
        "w_up": w_up[0],
        "ffn_conv_w": ffn_conv_w[0],
        "w_down": w_down[0],
        "norm_final": norm_final_w[None, :],
    }

    xp = x_prompt.reshape(bp * tp, D_MODEL)
    p_states = (jnp.zeros((bp, GDN_CONV - 1, QKV), F32), jnp.zeros((bp, HEADS, DH, DH), F32),
                jnp.zeros((bp, HEADS, DH, DH), F32), jnp.zeros((bp, HEADS, DH), F32),
                jnp.zeros((bp, 1, HEADS), F32))
    assert tp % TM_FFN == 0 and (bp * tp) % TM_ROWS == 0
    yp, p_new, up_p, w = _layer(
        xp, bp, tp // CHUNK, 1, CHUNK, CHUNK, PROMPT_GDN_GROUP, PROMPT_ML_GROUP, p_states, w,
        tm_in=TM_ROWS, tm_merge=TM_ROWS, tm_ffn=TM_FFN, seq_tiles=tp // TM_FFN, ffn_state=None)
    p_ffn_conv = jnp.concatenate(
        [u.reshape(bp, tp // TM_FFN, SUBLANES, D_FF)[:, -1, SUBLANES - keep:, :] for u in up_p], axis=-1)

    xs = x_sample.reshape(bs * ts, D_MODEL)
    s_states = (state_gdn_conv[0], state_gdn_S[0], state_ml_C[0], state_ml_n[0],
                state_ml_m[0].reshape(bs, 1, HEADS))
    sample_rows = min(bs * ts, TM_ROWS)
    assert (bs * ts) % sample_rows == 0 and bs % SAMPLE_SEQS == 0
    ys, s_new, new_rows, _ = _layer(
        xs, bs, 1, SAMPLE_SEQS, SUBLANES, ts, (SAMPLE_GROUP, 1), SAMPLE_GROUP, s_states, w,
        tm_in=sample_rows, tm_merge=sample_rows, tm_ffn=sample_rows, seq_tiles=1,
        ffn_state=[state_ffn_conv[0, :, r, :] for r in range(keep)])
    s_ffn_conv = jnp.stack([jnp.concatenate([new_rows[r], new_rows[keep + r]], axis=-1) for r in range(keep)], axis=1)

    lead = lambda t: tuple(a[None] for a in t)
    return (yp.reshape(bp, tp, D_MODEL), ys.reshape(bs, ts, D_MODEL),
            *lead(p_new), p_ffn_conv[None], *lead(s_new), s_ffn_conv[None])
```
